```python
import math
import jax, jax.numpy as jnp
from jax import lax
import numpy as np

D_MODEL = 1024
BATCH = 8
SEQ = 2048
DEPTH = 4
DEC_BATCH = 128
DEC_SEQ = 1
PAST_LEN = 16384
PAGE_SIZE = 128

N_MIXERS = 4
MIX_W = D_MODEL
GROUP_W = MIX_W // N_MIXERS
HEAD_DIM = 64
HG_HEADS = GROUP_W // HEAD_DIM
GDN_HEADS = GROUP_W // HEAD_DIM
SSD_HEADS = GROUP_W // HEAD_DIM
RET_HEADS = GROUP_W // HEAD_DIM
SSD_GROUPS = 2
SSD_STATE = 128
CONV_W = 4
GDN_CONV_DIM = 3 * GROUP_W
SSD_CONV_DIM = GROUP_W + 2 * SSD_GROUPS * SSD_STATE
N_META = 16
CHUNK = 64
ROPE_BASE = 10000.0
EPS = 1e-6
TINY = 1e-30
DT_MIN = 1e-3
DT_MAX = 1e-1
QK_SCALE = HEAD_DIM ** -0.5
IN_SIZES = (GROUP_W, GROUP_W, GROUP_W, GROUP_W,
            GDN_CONV_DIM, GROUP_W, GDN_HEADS, GDN_HEADS,
            SSD_CONV_DIM, GROUP_W, SSD_HEADS,
            GROUP_W, GROUP_W, GROUP_W, GROUP_W)
IN_DIM = sum(IN_SIZES)
F32 = jnp.float32

kernel_name = 'hymba_style_hgrn2_gdn_ssd_retention_step'


def group_rms_norm(x, w, groups):
    b, t, width = x.shape
    xg = x.astype(F32).reshape(b, t, groups, width // groups)
    xg = xg * lax.rsqrt(jnp.mean(xg * xg, axis=-1, keepdims=True) + EPS)
    return xg.reshape(b, t, width) * w


def rms_norm(x, w):
    return group_rms_norm(x, w, 1)


def group_norm(x, w, bias, groups):
    b, t, width = x.shape
    xg = x.astype(F32).reshape(b, t, groups, width // groups)
    mu = jnp.mean(xg, axis=-1, keepdims=True)
    xc = xg - mu
    xg = xc * lax.rsqrt(jnp.mean(xc * xc, axis=-1, keepdims=True) + EPS)
    return xg.reshape(b, t, width) * w + bias


def l2_normalize(x):
    return x * lax.rsqrt(jnp.sum(x * x, axis=-1, keepdims=True) + EPS)


def to_heads(x, n):
    b, t, w = x.shape
    return x.reshape(b, t, n, w // n).transpose(0, 2, 1, 3)


def from_heads(o):
    b, h, t, d = o.shape
    return o.transpose(0, 2, 1, 3).reshape(b, t, h * d)


def to_chunks(a):
    b, h, t = a.shape[:3]
    return jnp.moveaxis(a.reshape(b, h, t // CHUNK, CHUNK, *a.shape[3:]), 2, 0)


def from_chunks(o):
    o = jnp.moveaxis(o, 0, 2)
    return o.reshape(o.shape[0], o.shape[1], -1, o.shape[-1])


def masked_exp(diff, mask):
    return jnp.where(mask, jnp.exp(jnp.where(mask, diff, 0.0)), 0.0)


def causal_conv(u, ctx, w, b=None):
    full = jnp.concatenate([ctx.astype(u.dtype), u], axis=1)
    y = lax.conv_general_dilated(full, w[:, None, :].astype(u.dtype), window_strides=(1,), padding='VALID',
                                 dimension_numbers=('NWC', 'WIO', 'NWC'), feature_group_count=u.shape[-1])
    if b is not None:
        y = y + b
    return y, full[:, full.shape[1] - (CONV_W - 1):]


def rotary(x, pos):
    half = x.shape[-1] // 2
    inv_freq = 1.0 / (ROPE_BASE ** jnp.linspace(0.0, 1.0, half, dtype=F32))
    ang = pos[:, None] * inv_freq[None, :]
    cos, sin = jnp.cos(ang), jnp.sin(ang)
    x1, x2 = x[..., :half], x[..., half:]
    return jnp.concatenate([x1 * cos - x2 * sin, x2 * cos + x1 * sin], axis=-1)


def chunk_scalar_decay(S, xs):
    q, k, v, g = xs
    c = g.shape[-1]
    G = jnp.cumsum(g, axis=-1)
    causal = jnp.tril(jnp.ones((c, c), bool))
    decay = masked_exp(G[..., :, None] - G[..., None, :], causal)
    scores = jnp.einsum('bhik,bhjk->bhij', q, k) * decay
    o = (jnp.einsum('bhij,bhjv->bhiv', scores, v)
         + jnp.einsum('bhik,bhkv->bhiv', q * jnp.exp(G)[..., None], S))
    g_last = G[..., -1:]
    S_new = (jnp.exp(g_last)[..., None] * S
             + jnp.einsum('bhjk,bhjv->bhkv', k * jnp.exp(g_last - G)[..., None], v))
    return S_new, o


def step_scalar_decay(S, xs):
    q, k, v, g = xs
    S = jnp.exp(g)[..., None, None] * S + k[..., :, None] * v[..., None, :]
    return S, jnp.einsum('bhk,bhkv->bhv', q, S)


def chunk_vector_decay(S, xs):
    q, k, v, g = xs
    c = g.shape[2]
    G = jnp.cumsum(g, axis=2)
    causal = jnp.tril(jnp.ones((c, c), bool))
    diff = G[:, :, :, None, :] - G[:, :, None, :, :]
    decay = masked_exp(diff, causal[:, :, None])
    scores = jnp.einsum('bhik,bhjk,bhijk->bhij', q, k, decay)
    o = (jnp.einsum('bhij,bhjv->bhiv', scores, v)
         + jnp.einsum('bhik,bhkv->bhiv', q * jnp.exp(G), S))
    g_last = G[:, :, -1]
    S_new = (jnp.exp(g_last)[..., None] * S
             + jnp.einsum('bhjk,bhjv->bhkv', k * jnp.exp(g_last[:, :, None] - G), v))
    return S_new, o


def step_vector_decay(S, xs):
    q, k, v, g = xs
    S = jnp.exp(g)[..., None] * S + k[..., :, None] * v[..., None, :]
    return S, jnp.einsum('bhk,bhkv->bhv', q, S)


def chunk_gated_delta(S, xs):
    q, k, v, g, beta = xs
    c = g.shape[-1]
    G = jnp.cumsum(g, axis=-1)
    causal = jnp.tril(jnp.ones((c, c), bool))
    strict = jnp.tril(jnp.ones((c, c), bool), -1)
    decay = masked_exp(G[..., :, None] - G[..., None, :], causal)
    m = jnp.where(strict, jnp.einsum('bhik,bhjk->bhij', k, k) * decay * beta[..., :, None], 0.0)
    lhs = jnp.eye(c, dtype=m.dtype) + m
    rhs = jnp.concatenate([v * beta[..., None], k * (beta * jnp.exp(G))[..., None]], axis=-1)
    sol = lax.linalg.triangular_solve(lhs, rhs, left_side=True, lower=True, unit_diagonal=True)
    dv = v.shape[-1]
    u = sol[..., :dv] - jnp.einsum('bhik,bhkv->bhiv', sol[..., dv:], S)
    qk = jnp.einsum('bhik,bhjk->bhij', q, k) * decay
    o = (jnp.einsum('bhik,bhkv->bhiv', q * jnp.exp(G)[..., None], S)
         + jnp.einsum('bhij,bhjv->bhiv', qk, u))
    g_last = G[..., -1:]
    S_new = (jnp.exp(g_last)[..., None] * S
             + jnp.einsum('bhjk,bhjv->bhkv', k * jnp.exp(g_last - G)[..., None], u))
    return S_new, o


def step_gated_delta(S, xs):
    q, k, v, g, beta = xs
    S = jnp.exp(g)[..., None, None] * S
    u = beta[..., None] * (v - jnp.einsum('bhk,bhkv->bhv', k, S))
    S = S + k[..., :, None] * u[..., None, :]
    return S, jnp.einsum('bhk,bhkv->bhv', q, S)


def run_mixer(chunk_fn, step_fn, args, S0, prompt):
    if prompt:
        S, o_meta = chunk_fn(S0, tuple(a[:, :, :N_META] for a in args))
        S, o = lax.scan(chunk_fn, S, tuple(to_chunks(a[:, :, N_META:]) for a in args))
        return S, jnp.concatenate([o_meta, from_chunks(o)], axis=2)
    S, o = lax.scan(step_fn, S0, tuple(jnp.moveaxis(a, 2, 0) for a in args))
    return S, jnp.moveaxis(o, 0, 2)


def mixer(hn, pos, w_in_l, lb, hgrn_norm_w_l, gdn_conv_w_l, gdn_a_log_l, gdn_dt_bias_l, gdn_norm_w_l,
          ssd_conv_w_l, ssd_conv_b_l, ssd_a_log_l, ssd_dt_bias_l, ssd_d_l, ssd_norm_w_l,
          ret_norm_w_l, ret_norm_b_l, w_out_l,
          s_hgrn, s_gdn, gdn_ctx, s_ssd, ssd_ctx, s_ret, prompt):
    bsz, t, _ = hn.shape
    proj = jnp.einsum('btd,de->bte', hn, w_in_l)
    offsets = [int(o) for o in np.cumsum(IN_SIZES)[:-1]]
    (a_q, a_f, a_i, a_z, b_qkv, b_z, b_a, b_b, c_xbc, c_z, c_dt,
     d_q, d_k, d_v, d_z) = jnp.split(proj, offsets, axis=-1)

    a_k = (1.0 - lb) * jax.nn.sigmoid(-a_f)
    log_f = jnp.log(jnp.maximum(lb + (1.0 - lb) * jax.nn.sigmoid(a_f), TINY))
    s_hgrn, o_a = run_mixer(chunk_vector_decay, step_vector_decay,
                            (to_heads(jax.nn.silu(a_q), HG_HEADS) * QK_SCALE, to_heads(a_k, HG_HEADS),
                             to_heads(a_i, HG_HEADS), to_heads(log_f, HG_HEADS)), s_hgrn, prompt)
    y_a = group_rms_norm(from_heads(o_a), hgrn_norm_w_l, HG_HEADS) * jax.nn.silu(a_z)

    b_conv, gdn_ctx = causal_conv(b_qkv, gdn_ctx, gdn_conv_w_l)
    b_q, b_k, b_v = jnp.split(jax.nn.silu(b_conv), 3, axis=-1)
    b_g = jnp.swapaxes(-jnp.exp(gdn_a_log_l) * jax.nn.softplus(b_a + gdn_dt_bias_l), 1, 2)
    b_beta = jnp.swapaxes(jax.nn.sigmoid(b_b), 1, 2)
    s_gdn, o_b = run_mixer(chunk_gated_delta, step_gated_delta,
                           (l2_normalize(to_heads(b_q, GDN_HEADS)) * QK_SCALE, l2_normalize(to_heads(b_k, GDN_HEADS)),
                            to_heads(b_v, GDN_HEADS), b_g, b_beta), s_gdn, prompt)
    y_b = group_rms_norm(from_heads(o_b), gdn_norm_w_l, GDN_HEADS) * jax.nn.silu(b_z)

    c_conv, ssd_ctx = causal_conv(c_xbc, ssd_ctx, ssd_conv_w_l, ssd_conv_b_l)
    c_x, c_b, c_c = jnp.split(jax.nn.silu(c_conv), [GROUP_W, GROUP_W + SSD_GROUPS * SSD_STATE], axis=-1)
    dt_h = jnp.swapaxes(jax.nn.softplus(c_dt + ssd_dt_bias_l), 1, 2)
    x_h = to_heads(c_x, SSD_HEADS)
    b_h = jnp.repeat(to_heads(c_b, SSD_GROUPS), SSD_HEADS // SSD_GROUPS, axis=1)
    c_h = jnp.repeat(to_heads(c_c, SSD_GROUPS), SSD_HEADS // SSD_GROUPS, axis=1)
    s_ssd, o_c = run_mixer(chunk_scalar_decay, step_scalar_decay,
                           (c_h, b_h, x_h * dt_h[..., None], dt_h * (-jnp.exp(ssd_a_log_l))[:, None]),
                           s_ssd, prompt)
    o_c = o_c + ssd_d_l[:, None, None] * x_h
    y_c = group_rms_norm(from_heads(o_c) * jax.nn.silu(c_z), ssd_norm_w_l, SSD_GROUPS)

    ret_log_decay = jnp.log1p(-jnp.exp2(-5.0 - jnp.arange(RET_HEADS, dtype=F32)))
    ret_g = jnp.broadcast_to(ret_log_decay[None, :, None], (bsz, RET_HEADS, t))
    s_ret, o_d = run_mixer(chunk_scalar_decay, step_scalar_decay,
                           (rotary(to_heads(d_q, RET_HEADS), pos), rotary(to_heads(d_k, RET_HEADS), pos) * QK_SCALE,
                            to_heads(d_v, RET_HEADS), ret_g), s_ret, prompt)
    y_d = group_norm(from_heads(o_d), ret_norm_w_l, ret_norm_b_l, RET_HEADS) * jax.nn.silu(d_z)

    y = jnp.concatenate([y_a, y_b, y_c, y_d], axis=-1)
    out = jnp.einsum('bte,ed->btd', y, w_out_l)
    return out, (s_hgrn, s_gdn, gdn_ctx, s_ssd, ssd_ctx, s_ret)


def zero_states(bsz):
    return (jnp.zeros((bsz, HG_HEADS, HEAD_DIM, HEAD_DIM), F32),
            jnp.zeros((bsz, GDN_HEADS, HEAD_DIM, HEAD_DIM), F32),
            jnp.zeros((bsz, CONV_W - 1, GDN_CONV_DIM), F32),
            jnp.zeros((bsz, SSD_HEADS, SSD_STATE, HEAD_DIM), F32),
            jnp.zeros((bsz, CONV_W - 1, SSD_CONV_DIM), F32),
            jnp.zeros((bsz, RET_HEADS, HEAD_DIM, HEAD_DIM), F32))


def stack_states(states, i, like):
    return jnp.stack([s[i] for s in states]).astype(like.dtype)


def setup_inputs(seed: int = 0) -> dict:
    key = jax.random.key(seed)
    ks = jax.random.split(key, 28)

    def nrm(k, shape, scale):
        return scale * jax.random.normal(k, shape, F32)

    def gain(k, shape):
        return 1.0 + 0.05 * jax.random.normal(k, shape, F32)

    def a_log(k, n):
        return jnp.log(jax.random.uniform(k, (DEPTH, n), F32, 1.0, 16.0))

    def dt_bias(k, n):
        dt = jnp.exp(jax.random.uniform(k, (DEPTH, n), F32, math.log(DT_MIN), math.log(DT_MAX)))
        return dt + jnp.log(-jnp.expm1(-dt))

    return {
        'x_prompt': nrm(ks[0], (BATCH, SEQ, D_MODEL), 1.0),
        'x_sample': nrm(ks[1], (DEC_BATCH, DEC_SEQ, D_MODEL), 1.0),
        'state_hgrn': nrm(ks[2], (DEPTH, DEC_BATCH, HG_HEADS, HEAD_DIM, HEAD_DIM), 0.5),
        'state_gdn': nrm(ks[3], (DEPTH, DEC_BATCH, GDN_HEADS, HEAD_DIM, HEAD_DIM), 0.5),
        'state_gdn_conv': nrm(ks[4], (DEPTH, DEC_BATCH, CONV_W - 1, GDN_CONV_DIM), 1.0),
        'state_ssd': nrm(ks[5], (DEPTH, DEC_BATCH, SSD_HEADS, SSD_STATE, HEAD_DIM), 0.5),
        'state_ssd_conv': nrm(ks[6], (DEPTH, DEC_BATCH, CONV_W - 1, SSD_CONV_DIM), 1.0),
        'state_ret': nrm(ks[7], (DEPTH, DEC_BATCH, RET_HEADS, HEAD_DIM, HEAD_DIM), 2.0),
        'meta_tokens': nrm(ks[8], (N_META, D_MODEL), 1.0),
        'norm_w': gain(ks[9], (DEPTH, D_MODEL)),
        'w_in': nrm(ks[10], (DEPTH, D_MODEL, IN_DIM), D_MODEL ** -0.5),
        'hgrn_lb_logits': nrm(ks[11], (DEPTH, GROUP_W), 0.5),
        'hgrn_norm_w': gain(ks[12], (DEPTH, GROUP_W)),
        'gdn_conv_w': nrm(ks[13], (DEPTH, CONV_W, GDN_CONV_DIM), CONV_W ** -0.5),
        'gdn_a_log': a_log(ks[14], GDN_HEADS),
        'gdn_dt_bias': dt_bias(ks[15], GDN_HEADS),
        'gdn_norm_w': gain(ks[16], (DEPTH, GROUP_W)),
        'ssd_conv_w': nrm(ks[17], (DEPTH, CONV_W, SSD_CONV_DIM), CONV_W ** -0.5),
        'ssd_conv_b': nrm(ks[18], (DEPTH, SSD_CONV_DIM), 0.02),
        'ssd_a_log': a_log(ks[19], SSD_HEADS),
        'ssd_dt_bias': dt_bias(ks[20], SSD_HEADS),
        'ssd_d': gain(ks[21], (DEPTH, SSD_HEADS)),
        'ssd_norm_w': gain(ks[22], (DEPTH, GROUP_W)),
        'ret_norm_w': gain(ks[23], (DEPTH, GROUP_W)),
        'ret_norm_b': nrm(ks[24], (DEPTH, GROUP_W), 0.02),
        'w_out': nrm(ks[25], (DEPTH, MIX_W, D_MODEL), MIX_W ** -0.5),
        'final_norm_w': gain(ks[26], (D_MODEL,)),
    }


def reference(x_prompt, x_sample, state_hgrn, state_gdn, state_gdn_conv, state_ssd, state_ssd_conv, state_ret,
              meta_tokens, norm_w, w_in, hgrn_lb_logits, hgrn_norm_w, gdn_conv_w, gdn_a_log, gdn_dt_bias, gdn_norm_w,
              ssd_conv_w, ssd_conv_b, ssd_a_log, ssd_dt_bias, ssd_d, ssd_norm_w, ret_norm_w, ret_norm_b,
              w_out, final_norm_w):
    bsz = x_prompt.shape[0]
    lb_w = jax.nn.softmax(hgrn_lb_logits.astype(F32), axis=0)
    lb_all = jnp.maximum(jnp.cumsum(lb_w, axis=0) - lb_w[0], 0.0)

    meta = jnp.broadcast_to(meta_tokens.astype(F32)[None], (bsz, N_META, D_MODEL))
    hp = jnp.concatenate([meta, x_prompt.astype(F32)], axis=1)
    hs = x_sample.astype(F32)
    pos_p = jnp.arange(hp.shape[1], dtype=F32)
    pos_s = PAST_LEN + jnp.arange(hs.shape[1], dtype=F32)

    st_p, st_s = [], []
    for l in range(DEPTH):
        lw = (w_in[l], lb_all[l], hgrn_norm_w[l], gdn_conv_w[l], gdn_a_log[l], gdn_dt_bias[l], gdn_norm_w[l],
              ssd_conv_w[l], ssd_conv_b[l], ssd_a_log[l], ssd_dt_bias[l], ssd_d[l], ssd_norm_w[l],
              ret_norm_w[l], ret_norm_b[l], w_out[l])
        dp, sp = mixer(rms_norm(hp, norm_w[l]), pos_p, *lw, *zero_states(bsz), True)
        hp = hp + dp
        past = tuple(a[l].astype(F32) for a in (state_hgrn, state_gdn, state_gdn_conv,
                                                 state_ssd, state_ssd_conv, state_ret))
        ds, ss = mixer(rms_norm(hs, norm_w[l]), pos_s, *lw, *past, False)
        hs = hs + ds
        st_p.append(sp)
        st_s.append(ss)

    y_prompt = rms_norm(hp[:, N_META:], final_norm_w).astype(x_prompt.dtype)
    y_sample = rms_norm(hs, final_norm_w).astype(x_sample.dtype)
    return (y_prompt, y_sample,
            stack_states(st_p, 0, state_hgrn), stack_states(st_p, 1, state_gdn),
            stack_states(st_p, 2, state_gdn_conv), stack_states(st_p, 3, state_ssd),
            stack_states(st_p, 4, state_ssd_conv), stack_states(st_p, 5, state_ret),
            stack_states(st_s, 0, state_hgrn), stack_states(st_s, 1, state_gdn),
            stack_states(st_s, 2, state_gdn_conv), stack_states(st_s, 3, state_ssd),
            stack_states(st_s, 4, state_ssd_conv), stack_states(st_s, 5, state_ret))
```

```python
import functools
import math

import numpy as np
import jax
import jax.numpy as jnp
from jax import lax
from jax.experimental import pallas as pl
from jax.experimental.pallas import tpu as pltpu

F32 = jnp.float32
BF16 = jnp.bfloat16

D_MODEL = 1024
GROUP_W = 256
HEAD_DIM = 64
N_HEADS = 4
CHUNK = 64
CONV_W = 4
CONV_DIM = 768
N_META = 16
PAST_LEN = 16384
ROPE_BASE = 10000.0
EPS = 1e-6
TINY = 1e-30
QK_SCALE = HEAD_DIM ** -0.5
DEPTH = 4

A_Q, A_F, A_I, A_Z = 0, 256, 512, 768
B_QKV, B_Z = 1024, 1792
C_XBC, C_Z = 2048, 2816
D_Q, D_K, D_V, D_Z = 3072, 3328, 3584, 3840
SMALL = 4096
NP = 4224

R_NORM, R_LB, R_HNORM = 0, 1, 5
R_GCONV, R_GALOG, R_GDTB, R_GNORM = 6, 10, 11, 12
R_SCONV, R_SCONVB, R_SALOG, R_SDTB, R_SD, R_SNORM = 13, 17, 18, 19, 20, 21
R_RNORM, R_RNORMB, R_RLOGG, R_FINAL = 22, 23, 24, 25
N_PRM = 32

VMEM_LIMIT = 56 * 1024 * 1024
PROMPT_TILE = 256
DEC_BLOCK = 8


def _dot(a, b):
    return jnp.dot(a.astype(BF16), b.astype(BF16), preferred_element_type=F32)


def _dot_nt(a, b):
    return lax.dot_general(a.astype(BF16), b.astype(BF16), (((1,), (1,)), ((), ())),
                           preferred_element_type=F32)


def _dot_tn(a, b):
    return lax.dot_general(a.astype(BF16), b.astype(BF16), (((0,), (0,)), ((), ())),
                           preferred_element_type=F32)


def _split3(x):
    hi = x.astype(BF16)
    r1 = x - hi.astype(F32)
    mid = r1.astype(BF16)
    lo = (r1 - mid.astype(F32)).astype(BF16)
    return hi, mid, lo


def _dot_exact_rhs(x, m):
    hi, mid, lo = _split3(x)
    return (jnp.dot(hi, m, preferred_element_type=F32) + jnp.dot(mid, m, preferred_element_type=F32)
            + jnp.dot(lo, m, preferred_element_type=F32))


def _dot_exact_lhs(m, x):
    hi, mid, lo = _split3(x)
    return (jnp.dot(m, hi, preferred_element_type=F32) + jnp.dot(m, mid, preferred_element_type=F32)
            + jnp.dot(m, lo, preferred_element_type=F32))


def _sigmoid(x):
    return 1.0 / (1.0 + jnp.exp(-x))


def _silu(x):
    return x * _sigmoid(x)


def _softplus(x):
    return jnp.maximum(x, 0.0) + jnp.log(1.0 + jnp.exp(-jnp.abs(x)))


def _iota(shape, dim):
    return lax.broadcasted_iota(jnp.int32, shape, dim)


def _block_diag(x, mask):
    xb = x.astype(BF16)
    t = jnp.concatenate([xb] * N_HEADS, axis=0)
    return jnp.where(mask, t, jnp.zeros_like(t))


def _chunk_cumsum(x, row_in_chunk):
    sh = 1
    while sh < CHUNK:
        r = pltpu.roll(x, sh, axis=0)
        x = x + jnp.where(row_in_chunk >= sh, r, 0.0)
        sh *= 2
    return x


def _swap_halves(x):
    lane = _iota(x.shape, 1)
    fwd = pltpu.roll(x, GROUP_W - HEAD_DIM // 2, axis=1)
    bwd = pltpu.roll(x, HEAD_DIM // 2, axis=1)
    return jnp.where((lane & (HEAD_DIM - 1)) < HEAD_DIM // 2, fwd, bwd)


def _group_sum(x, ones_bd):
    hi = x.astype(BF16)
    lo = (x - hi.astype(F32)).astype(BF16)
    return jnp.dot(hi, ones_bd, preferred_element_type=F32) + jnp.dot(lo, ones_bd, preferred_element_type=F32)


def _rms(x):
    return x * lax.rsqrt(jnp.mean(x * x, axis=-1, keepdims=True) + EPS)


def _hgrn_lower_bound(logits, layer):
    m = jnp.max(logits, axis=0, keepdims=True)
    e = jnp.exp(logits - m)
    w = e / jnp.sum(e, axis=0, keepdims=True)
    cum = jnp.sum(w[0:layer + 1], axis=0, keepdims=True) - w[0:1]
    return jnp.maximum(cum, 0.0)


def _prow(prm_ref, r, width=GROUP_W, n=1):
    return prm_ref[0, r:r + n, 0:width]


def _hgrn_inputs(proj, lb):
    aq = proj[:, A_Q:A_Q + GROUP_W]
    af = proj[:, A_F:A_F + GROUP_W]
    q = _silu(aq) * QK_SCALE
    k = (1.0 - lb) * _sigmoid(-af)
    f = jnp.maximum(lb + (1.0 - lb) * _sigmoid(af), TINY)
    return q, k, f, proj[:, A_I:A_I + GROUP_W]


def _expand_small(proj, expand):
    return _dot_exact_rhs(proj[:, SMALL:SMALL + 128], expand)


def _l2n(x, ones64):
    return x * lax.rsqrt(_group_sum(x * x, ones64) + EPS)


def _rotary(x, cos, sin):
    return x * cos + _swap_halves(x) * sin


def _post_mix(o_a, o_b, o_c, o_d, z_a, z_b, z_c, z_d, x_ssd, prm_ref, ones64, ones128):
    inv64 = 1.0 / HEAD_DIM
    y_a = o_a * lax.rsqrt(_group_sum(o_a * o_a, ones64) * inv64 + EPS) * _prow(prm_ref, R_HNORM) * _silu(z_a)
    y_b = o_b * lax.rsqrt(_group_sum(o_b * o_b, ones64) * inv64 + EPS) * _prow(prm_ref, R_GNORM) * _silu(z_b)
    t_c = (o_c + _prow(prm_ref, R_SD) * x_ssd) * _silu(z_c)
    y_c = t_c * lax.rsqrt(_group_sum(t_c * t_c, ones128) * (1.0 / 128.0) + EPS) * _prow(prm_ref, R_SNORM)
    mu = _group_sum(o_d, ones64) * inv64
    xc = o_d - mu
    y_d = ((xc * lax.rsqrt(_group_sum(xc * xc, ones64) * inv64 + EPS) * _prow(prm_ref, R_RNORM)
            + _prow(prm_ref, R_RNORMB)) * _silu(z_d))
    return jnp.concatenate([y_a, y_b, y_c, y_d], axis=-1)


def _prompt_kernel(layer, tile, valid, final,
                   h_ref, cos_ref, sin_ref, win_ref, wout_ref, prm_ref, dsel_ref, exp_ref, s0_ref, ctx0_ref,
                   hout_ref, sout_ref, ctxout_ref,
                   proj_s, cbuf_g, cbuf_s, st_s,
                   hq_s, hk_s, hv_s, hg_s,
                   gq_s, gk_s, gv_s, gg_s, gb_s,
                   sx_s, sv_s, sb_s, sc_s, sg_s,
                   rq_s, rk_s, rv_s, rg_s, o_s):
    t = pl.program_id(1)
    nt = pl.num_programs(1)
    n_chunks = tile // CHUNK
    C = CHUNK

    @pl.when(t == 0)
    def _():
        st_s[...] = s0_ref[...]
        cbuf_g[0:8, :] = ctx0_ref[0]
        cbuf_s[0:8, :] = ctx0_ref[1]

    x = h_ref[0]
    hn = _rms(x) * _prow(prm_ref, R_NORM, D_MODEL)
    proj_s[...] = jnp.dot(hn.astype(BF16), win_ref[0], preferred_element_type=F32)

    row = _iota((tile, GROUP_W), 0)
    row_in_chunk = row & (C - 1)
    if valid < tile:
        live = row < valid
        keep = lambda a: jnp.where(live, a, 0.0)
    else:
        keep = lambda a: a

    ri = _iota((GROUP_W, GROUP_W), 0)
    ci = _iota((GROUP_W, GROUP_W), 1)
    ones64 = jnp.where((ri >> 6) == (ci >> 6), 1.0, 0.0).astype(BF16)
    ones128 = jnp.where((ri >> 7) == (ci >> 7), 1.0, 0.0).astype(BF16)

    lb = _hgrn_lower_bound(_prow(prm_ref, R_LB, n=DEPTH), layer)
    q, k, f, v = _hgrn_inputs(proj_s[...], lb)
    hq_s[...] = q
    hk_s[...] = k
    hv_s[...] = keep(v)
    hg_s[...] = _chunk_cumsum(keep(jnp.log(f)), row_in_chunk)

    small = _expand_small(proj_s[...], exp_ref[...])

    u = proj_s[:, B_QKV:B_QKV + CONV_DIM]
    cbuf_g[8:8 + tile, :] = u
    cw = _prow(prm_ref, R_GCONV, CONV_DIM, CONV_W)
    conv = (cw[3:4] * u + cw[2:3] * cbuf_g[7:7 + tile, :] + cw[1:2] * cbuf_g[6:6 + tile, :]
            + cw[0:1] * cbuf_g[5:5 + tile, :])
    act = _silu(conv)
    gq_s[...] = _l2n(act[:, 0:256], ones64) * QK_SCALE
    gk_s[...] = _l2n(act[:, 256:512], ones64)
    gv_s[...] = act[:, 512:768]
    g = -jnp.exp(_prow(prm_ref, R_GALOG)) * _softplus(small[:, 0:256] + _prow(prm_ref, R_GDTB))
    gg_s[...] = _chunk_cumsum(keep(g), row_in_chunk)
    gb_s[...] = keep(_sigmoid(small[:, 256:512]))

    u = proj_s[:, C_XBC:C_XBC + CONV_DIM]
    cbuf_s[8:8 + tile, :] = u
    cw = _prow(prm_ref, R_SCONV, CONV_DIM, CONV_W)
    conv = (cw[3:4] * u + cw[2:3] * cbuf_s[7:7 + tile, :] + cw[1:2] * cbuf_s[6:6 + tile, :]
            + cw[0:1] * cbuf_s[5:5 + tile, :] + _prow(prm_ref, R_SCONVB, CONV_DIM))
    act = _silu(conv)
    dt = keep(_softplus(small[:, 512:768] + _prow(prm_ref, R_SDTB)))
    sx_s[...] = act[:, 0:256]
    sv_s[...] = act[:, 0:256] * dt
    sb_s[...] = act[:, 256:512]
    sc_s[...] = act[:, 512:768]
    sg_s[...] = _chunk_cumsum(-dt * jnp.exp(_prow(prm_ref, R_SALOG)), row_in_chunk)

    cos = cos_ref[...]
    sin = sin_ref[...]
    rq_s[...] = _rotary(proj_s[:, D_Q:D_Q + GROUP_W], cos, sin)
    rk_s[...] = _rotary(proj_s[:, D_K:D_K + GROUP_W], cos, sin) * QK_SCALE
    rv_s[...] = keep(proj_s[:, D_V:D_V + GROUP_W])
    rg_s[...] = _chunk_cumsum(keep(jnp.broadcast_to(_prow(prm_ref, R_RLOGG), (tile, GROUP_W))), row_in_chunk)

    @pl.when(t == nt - 1)
    def _():
        ctxout_ref[0, 0] = cbuf_g[valid:valid + 8, :]
        ctxout_ref[0, 1] = cbuf_s[valid:valid + 8, :]

    cbuf_g[0:8, :] = cbuf_g[tile:tile + 8, :]
    cbuf_s[0:8, :] = cbuf_s[tile:tile + 8, :]

    ii = _iota((C, GROUP_W), 0)
    jj = _iota((C, GROUP_W), 1) & (C - 1)
    tri = jj <= ii
    strict = jj < ii
    eye = jj == ii
    eye_f = jnp.where(eye, 1.0, 0.0)
    bd_mask = (ri >> 6) == (ci >> 6)
    grp_row_mask = (ri >> 7) == (ci >> 7)
    sq_mask = bd_mask
    grp_mask = grp_row_mask

    def decay_parts(G):
        g_row = jnp.sum(jnp.where(eye, G, 0.0), axis=0, keepdims=True)
        dm = jnp.where(tri, jnp.exp(jnp.minimum(G - g_row, 0.0)), 0.0)
        g_last = G[C - 1:C, :]
        return dm, g_last

    def chunk_body(c, carry):
        base = pl.multiple_of(c * C, C)
        sl = pl.ds(base, C)

        q = hq_s[sl, :]
        k = hk_s[sl, :]
        v = hv_s[sl, :]
        G = hg_s[sl, :]
        x_all = _dot_exact_lhs(dsel_ref[...], G)
        scores = jnp.where(eye, _dot_nt(q, _block_diag(k, bd_mask)), 0.0)
        s = C // 2
        lvl = 0
        while s >= 1:
            e = jnp.exp(jnp.minimum(x_all[lvl * C:(lvl + 1) * C, :], 0.0))
            upper = (ii & s) != 0
            qs = jnp.where(upper, q * e, 0.0)
            ks = jnp.where(upper, 0.0, k * e)
            sc = _dot_nt(qs, _block_diag(ks, bd_mask))
            same_block = (ii & -(2 * s)) == (jj & -(2 * s))
            scores = scores + jnp.where(same_block, sc, 0.0)
            s //= 2
            lvl += 1
        g_last = G[C - 1:C, :]
        st = st_s[0]
        o = _dot(scores, _block_diag(v, bd_mask)) + _dot_nt(q * jnp.exp(G), st)
        kt = k * jnp.exp(g_last - G)
        st_s[0] = st * jnp.exp(g_last) + jnp.where(sq_mask, _dot_tn(v, kt), 0.0)
        o_s[sl, 0:256] = o

        q = gq_s[sl, :]
        k = gk_s[sl, :]
        v = gv_s[sl, :]
        G = gg_s[sl, :]
        beta = gb_s[sl, :]
        dm, g_last = decay_parts(G)
        kq = _dot_nt(jnp.concatenate([k, q], axis=0), _block_diag(k, bd_mask))
        m = jnp.where(strict, kq[0:C] * dm * beta, 0.0)
        qk = kq[C:2 * C] * dm
        p = m
        tinv = eye_f - m
        n = 2
        while n < C:
            p = _dot(p, _block_diag(p, bd_mask))
            tinv = tinv + _dot(tinv, _block_diag(p, bd_mask))
            n *= 2
        eg = jnp.exp(G)
        rhs = jnp.concatenate([_block_diag(v * beta, bd_mask), _block_diag(k * beta * eg, bd_mask)], axis=1)
        sol = _dot(tinv, rhs)
        st = st_s[1]
        tmp = _dot(jnp.concatenate([sol[:, 256:512], q * eg], axis=0), st)
        u = sol[:, 0:256] - tmp[0:C]
        o = tmp[C:2 * C] + _dot(qk, _block_diag(u, bd_mask))
        kt = k * jnp.exp(g_last - G)
        st_s[1] = st * jnp.exp(g_last) + jnp.where(sq_mask, _dot_tn(kt, u), 0.0)
        o_s[sl, 256:512] = o

        v = sv_s[sl, :]
        bm = sb_s[sl, :]
        cm = sc_s[sl, :]
        G = sg_s[sl, :]
        dm, g_last = decay_parts(G)
        b_rows = jnp.concatenate([bm.astype(BF16)] * N_HEADS, axis=0)
        b_rows = jnp.where(grp_row_mask, b_rows, jnp.zeros_like(b_rows))
        cb = _dot_nt(cm, b_rows)
        st = st_s[2]
        o = _dot(cb * dm, _block_diag(v, bd_mask)) + jnp.exp(G) * _dot(cm, st)
        vt = v * jnp.exp(g_last - G)
        st_s[2] = st * jnp.exp(g_last) + jnp.where(grp_mask, _dot_tn(bm, vt), 0.0)
        o_s[sl, 512:768] = o

        q = rq_s[sl, :]
        k = rk_s[sl, :]
        v = rv_s[sl, :]
        G = rg_s[sl, :]
        dm, g_last = decay_parts(G)
        sc = _dot_nt(q, _block_diag(k, bd_mask)) * dm
        st = st_s[3]
        o = _dot(sc, _block_diag(v, bd_mask)) + jnp.exp(G) * _dot(q, st)
        vt = v * jnp.exp(g_last - G)
        st_s[3] = st * jnp.exp(g_last) + jnp.where(sq_mask, _dot_tn(k, vt), 0.0)
        o_s[sl, 768:1024] = o
        return carry

    lax.fori_loop(0, n_chunks, chunk_body, 0)

    y = _post_mix(o_s[:, 0:256], o_s[:, 256:512], o_s[:, 512:768], o_s[:, 768:1024],
                  proj_s[:, A_Z:A_Z + GROUP_W], proj_s[:, B_Z:B_Z + GROUP_W],
                  proj_s[:, C_Z:C_Z + GROUP_W], proj_s[:, D_Z:D_Z + GROUP_W],
                  sx_s[...], prm_ref, ones64, ones128)
    out = x +jnp.dot(y.astype(BF16), wout_ref[0], preferred_element_type=F32)
    if final:
        out = _rms(out) * _prow(prm_ref, R_FINAL, D_MODEL)
    hout_ref[0] = out

    @pl.when(t == nt - 1)
    def _():
        sout_ref[0] = st_s[...]


def _prompt_layer(layer, h, cos, sin, win, wout, prm, dsel, expand, s0, ctx0, *, tile, valid, final):
    bsz, seq, _ = h.shape
    nt = seq // tile
    kern = functools.partial(_prompt_kernel, layer, tile, valid, final)
    const2 = lambda b, t: (0, 0)
    slab = lambda: pltpu.VMEM((tile, GROUP_W), F32)
    return pl.pallas_call(
        kern,
        grid=(bsz, nt),
        in_specs=[
            pl.BlockSpec((1, tile, D_MODEL), lambda b, t: (b, t, 0)),
            pl.BlockSpec((tile, GROUP_W), lambda b, t: (t, 0)),
            pl.BlockSpec((tile, GROUP_W), lambda b, t: (t, 0)),
            pl.BlockSpec((1, D_MODEL, NP), lambda b, t: (layer, 0, 0)),
            pl.BlockSpec((1, D_MODEL, D_MODEL), lambda b, t: (layer, 0, 0)),
            pl.BlockSpec((1, N_PRM, D_MODEL), lambda b, t: (layer, 0, 0)),
            pl.BlockSpec(dsel.shape, const2),
            pl.BlockSpec(expand.shape, const2),
            pl.BlockSpec((4, GROUP_W, GROUP_W), lambda b, t: (0, 0, 0)),
            pl.BlockSpec((2, 8, CONV_DIM), lambda b, t: (0, 0, 0)),
        ],
        out_specs=[
            pl.BlockSpec((1, tile, D_MODEL), lambda b, t: (b, t, 0)),
            pl.BlockSpec((1, 4, GROUP_W, GROUP_W), lambda b, t: (b, 0, 0, 0)),
            pl.BlockSpec((1, 2, 8, CONV_DIM), lambda b, t: (b, 0, 0, 0)),
        ],
        out_shape=[
            jax.ShapeDtypeStruct((bsz, seq, D_MODEL), F32),
            jax.ShapeDtypeStruct((bsz, 4, GROUP_W, GROUP_W), F32),
            jax.ShapeDtypeStruct((bsz, 2, 8, CONV_DIM), F32),
        ],
        scratch_shapes=[
            pltpu.VMEM((tile, NP), F32),
            pltpu.VMEM((tile + 8, CONV_DIM), F32),
            pltpu.VMEM((tile + 8, CONV_DIM), F32),
            pltpu.VMEM((4, GROUP_W, GROUP_W), F32),
        ] + [slab() for _ in range(18)] + [pltpu.VMEM((tile, D_MODEL), F32)],
        compiler_params=pltpu.CompilerParams(
            dimension_semantics=("arbitrary", "arbitrary"), vmem_limit_bytes=VMEM_LIMIT),
        name=f"prompt_layer_t{tile}",
    )(h, cos, sin, win, wout, prm, dsel, expand, s0, ctx0)


ROW_HV, ROW_GV, ROW_GDEC, ROW_GBETA, ROW_SV, ROW_SDEC, ROW_RV, ROW_RDEC = [256 * i for i in range(8)]
N_ROWS = 2048
COL_HQ, COL_HK, COL_HF, COL_GQ, COL_GK, COL_SB, COL_SC, COL_RQ, COL_RK = [256 * i for i in range(9)]
N_COLS = 2304
POST_AZ, POST_BZ, POST_CZ, POST_DZ, POST_SX = [256 * i for i in range(5)]
N_POST = 1280


def _decode_pre_kernel(layer, hs_ref, cos_ref, sin_ref, win_ref, prm_ref, exp_ref, gctx_ref, sctx_ref,
                       rows_ref, cols_ref, post_ref, ug_ref, us_ref):
    x = hs_ref[...]
    hn = _rms(x) * _prow(prm_ref, R_NORM, D_MODEL)
    proj = jnp.dot(hn.astype(BF16), win_ref[0], preferred_element_type=F32)

    ri = _iota((GROUP_W, GROUP_W), 0)
    ci = _iota((GROUP_W, GROUP_W), 1)
    ones64 = jnp.where((ri >> 6) == (ci >> 6), 1.0, 0.0).astype(BF16)

    lb = _hgrn_lower_bound(_prow(prm_ref, R_LB, n=DEPTH), layer)
    q, k, f, v = _hgrn_inputs(proj, lb)
    cols_ref[:, COL_HQ:COL_HQ + 256] = q
    cols_ref[:, COL_HK:COL_HK + 256] = k
    cols_ref[:, COL_HF:COL_HF + 256] = f
    rows_ref[:, ROW_HV:ROW_HV + 256] = v

    small = _expand_small(proj, exp_ref[...])

    u = proj[:, B_QKV:B_QKV + CONV_DIM]
    ug_ref[...] = u
    cw = _prow(prm_ref, R_GCONV, CONV_DIM, CONV_W)
    conv = cw[3:4] * u + cw[2:3] * gctx_ref[0, 2] + cw[1:2] * gctx_ref[0, 1] + cw[0:1] * gctx_ref[0, 0]
    act = _silu(conv)
    cols_ref[:, COL_GQ:COL_GQ + 256] = _l2n(act[:, 0:256], ones64) * QK_SCALE
    cols_ref[:, COL_GK:COL_GK + 256] = _l2n(act[:, 256:512], ones64)
    rows_ref[:, ROW_GV:ROW_GV + 256] = act[:, 512:768]
    g = -jnp.exp(_prow(prm_ref, R_GALOG)) * _softplus(small[:, 0:256] + _prow(prm_ref, R_GDTB))
    rows_ref[:, ROW_GDEC:ROW_GDEC + 256] = jnp.exp(g)
    rows_ref[:, ROW_GBETA:ROW_GBETA + 256] = _sigmoid(small[:, 256:512])

    u = proj[:, C_XBC:C_XBC + CONV_DIM]
    us_ref[...] = u
    cw = _prow(prm_ref, R_SCONV, CONV_DIM, CONV_W)
    conv = (cw[3:4] * u + cw[2:3] * sctx_ref[0, 2] + cw[1:2] * sctx_ref[0, 1] + cw[0:1] * sctx_ref[0, 0]
            + _prow(prm_ref, R_SCONVB, CONV_DIM))
    act = _silu(conv)
    dt = _softplus(small[:, 512:768] + _prow(prm_ref, R_SDTB))
    post_ref[:, POST_SX:POST_SX + 256] = act[:, 0:256]
    rows_ref[:, ROW_SV:ROW_SV + 256] = act[:, 0:256] * dt
    cols_ref[:, COL_SB:COL_SB + 256] = act[:, 256:512]
    cols_ref[:, COL_SC:COL_SC + 256] = act[:, 512:768]
    rows_ref[:, ROW_SDEC:ROW_SDEC + 256] = jnp.exp(-dt * jnp.exp(_prow(prm_ref, R_SALOG)))

    cos = cos_ref[...]
    sin = sin_ref[...]
    cols_ref[:, COL_RQ:COL_RQ + 256] = _rotary(proj[:, D_Q:D_Q + GROUP_W], cos, sin)
    cols_ref[:, COL_RK:COL_RK + 256] = _rotary(proj[:, D_K:D_K + GROUP_W], cos, sin) * QK_SCALE
    rows_ref[:, ROW_RV:ROW_RV + 256] = proj[:, D_V:D_V + GROUP_W]
    rows_ref[:, ROW_RDEC:ROW_RDEC + 256] = jnp.broadcast_to(jnp.exp(_prow(prm_ref, R_RLOGG)), (x.shape[0], GROUP_W))

    post_ref[:, POST_AZ:POST_AZ + 256] = proj[:, A_Z:A_Z + GROUP_W]
    post_ref[:, POST_BZ:POST_BZ + 256] = proj[:, B_Z:B_Z + GROUP_W]
    post_ref[:, POST_CZ:POST_CZ + 256] = proj[:, C_Z:C_Z + GROUP_W]
    post_ref[:, POST_DZ:POST_DZ + 256] = proj[:, D_Z:D_Z + GROUP_W]


def _decode_pre(layer, hs, cos, sin, win, prm, expand, gctx, sctx):
    nb = hs.shape[0]
    kern = functools.partial(_decode_pre_kernel, layer)
    full = lambda a: pl.BlockSpec(a.shape, lambda i: (0,) * a.ndim)
    return pl.pallas_call(
        kern,
        grid=(1,),
        in_specs=[
            full(hs), full(cos), full(sin),
            pl.BlockSpec((1, D_MODEL, NP), lambda i: (layer, 0, 0)),
            pl.BlockSpec((1, N_PRM, D_MODEL), lambda i: (layer, 0, 0)),
            full(expand),
            pl.BlockSpec((1, 3, nb, CONV_DIM), lambda i: (layer, 0, 0, 0)),
            pl.BlockSpec((1, 3, nb, CONV_DIM), lambda i: (layer, 0, 0, 0)),
        ],
        out_specs=[pl.BlockSpec((nb, w), lambda i: (0, 0)) for w in (N_ROWS, N_COLS, N_POST, CONV_DIM, CONV_DIM)],
        out_shape=[jax.ShapeDtypeStruct((nb, w), F32) for w in (N_ROWS, N_COLS, N_POST, CONV_DIM, CONV_DIM)],
        compiler_params=pltpu.CompilerParams(dimension_semantics=("arbitrary",), vmem_limit_bytes=VMEM_LIMIT),
        name="decode_pre",
    )(hs, cos, sin, win, prm, expand, gctx, sctx)


def _decode_state_kernel(rows_ref, cols_ref, hg_ref, gd_ref, sd_ref, rt_ref,
                         o_ref, hg_out, gd_out, sd_out, rt_out):
    nb = rows_ref.shape[0]
    for b in range(nb):
        for h in range(N_HEADS):
            lo = h * HEAD_DIM
            col = lambda off, n=HEAD_DIM, start=lo: cols_ref[0, off + start:off + start + n, b:b + 1]
            rowv = lambda off: rows_ref[b:b + 1, off + lo:off + lo + HEAD_DIM]

            s = col(COL_HF) * hg_ref[0, b, h] + col(COL_HK) * rowv(ROW_HV)
            hg_out[0, b, h] = s
            o_ref[b:b + 1, lo:lo + HEAD_DIM] = jnp.sum(col(COL_HQ) * s, axis=0, keepdims=True)

            kc = col(COL_GK)
            s = gd_ref[0, b, h] * rowv(ROW_GDEC)
            u = rowv(ROW_GBETA) * (rowv(ROW_GV) - jnp.sum(kc * s, axis=0, keepdims=True))
            s = s + kc * u
            gd_out[0, b, h] = s
            o_ref[b:b + 1, 256 + lo:256 + lo + HEAD_DIM] = jnp.sum(col(COL_GQ) * s, axis=0, keepdims=True)

            g0 = (h // 2) * 128
            s = sd_ref[0, b, h] * rowv(ROW_SDEC) + col(COL_SB, 128, g0) * rowv(ROW_SV)
            sd_out[0, b, h] = s
            o_ref[b:b + 1, 512 + lo:512 + lo + HEAD_DIM] = jnp.sum(col(COL_SC, 128, g0) * s, axis=0, keepdims=True)

            s = rt_ref[0, b, h] * rowv(ROW_RDEC) + col(COL_RK) * rowv(ROW_RV)
            rt_out[0, b, h] = s
            o_ref[b:b + 1, 768 + lo:768 + lo + HEAD_DIM] = jnp.sum(col(COL_RQ) * s, axis=0, keepdims=True)


def _decode_state(layer, rows, cols_t, st_hgrn, st_gdn, st_ssd, st_ret):
    nb = rows.shape[0]
    bb = DEC_BLOCK
    sq = (1, bb, N_HEADS, HEAD_DIM, HEAD_DIM)
    sd = (1, bb, N_HEADS, 128, HEAD_DIM)
    st_map = lambda i: (layer, i, 0, 0, 0)
    out_map = lambda i: (0, i, 0, 0, 0)
    return pl.pallas_call(
        _decode_state_kernel,
        grid=(nb // bb,),
        in_specs=[
            pl.BlockSpec((bb, N_ROWS), lambda i: (i, 0)),
            pl.BlockSpec((1, N_COLS, bb), lambda i: (i, 0, 0)),
            pl.BlockSpec(sq, st_map), pl.BlockSpec(sq, st_map), pl.BlockSpec(sd, st_map), pl.BlockSpec(sq, st_map),
        ],
        out_specs=[
            pl.BlockSpec((bb, D_MODEL), lambda i: (i, 0)),
            pl.BlockSpec(sq, out_map), pl.BlockSpec(sq, out_map), pl.BlockSpec(sd, out_map), pl.BlockSpec(sq, out_map),
        ],
        out_shape=[
            jax.ShapeDtypeStruct((nb, D_MODEL), F32),
            jax.ShapeDtypeStruct((1, nb, N_HEADS, HEAD_DIM, HEAD_DIM), F32),
            jax.ShapeDtypeStruct((1, nb, N_HEADS, HEAD_DIM, HEAD_DIM), F32),
            jax.ShapeDtypeStruct((1, nb, N_HEADS, 128, HEAD_DIM), F32),
            jax.ShapeDtypeStruct((1, nb, N_HEADS, HEAD_DIM, HEAD_DIM), F32),
        ],
        compiler_params=pltpu.CompilerParams(dimension_semantics=("arbitrary",), vmem_limit_bytes=VMEM_LIMIT),
        name="decode_state",
    )(rows, cols_t, st_hgrn, st_gdn, st_ssd, st_ret)


def _decode_post_kernel(final, hs_ref, o_ref, post_ref, wout_ref, prm_ref, out_ref):
    ri = _iota((GROUP_W, GROUP_W), 0)
    ci = _iota((GROUP_W, GROUP_W), 1)
    ones64 = jnp.where((ri >> 6) == (ci >> 6), 1.0, 0.0).astype(BF16)
    ones128 = jnp.where((ri >> 7) == (ci >> 7), 1.0, 0.0).astype(BF16)
    gate = lambda off: post_ref[:, off:off + GROUP_W]
    y = _post_mix(o_ref[:, 0:256], o_ref[:, 256:512], o_ref[:, 512:768], o_ref[:, 768:1024],
                  gate(POST_AZ), gate(POST_BZ), gate(POST_CZ), gate(POST_DZ), gate(POST_SX),
                  prm_ref, ones64, ones128)
    out = hs_ref[...] + jnp.dot(y.astype(BF16), wout_ref[0], preferred_element_type=F32)
    if final:
        out = _rms(out) * _prow(prm_ref, R_FINAL, D_MODEL)
    out_ref[...] = out


def _decode_post(layer, hs, o, post, wout, prm, final):
    nb = hs.shape[0]
    full = lambda a: pl.BlockSpec(a.shape, lambda i: (0,) * a.ndim)
    return pl.pallas_call(
        functools.partial(_decode_post_kernel, final),
        grid=(1,),
        in_specs=[full(hs), full(o), full(post),
                  pl.BlockSpec((1, D_MODEL, D_MODEL), lambda i: (layer, 0, 0)),
                  pl.BlockSpec((1, N_PRM, D_MODEL), lambda i: (layer, 0, 0))],
        out_specs=pl.BlockSpec((nb, D_MODEL), lambda i: (0, 0)),
        out_shape=jax.ShapeDtypeStruct((nb, D_MODEL), F32),
        compiler_params=pltpu.CompilerParams(dimension_semantics=("arbitrary",), vmem_limit_bytes=VMEM_LIMIT),
        name="decode_post",
    )(hs, o, post, wout, prm)


def _level_select_matrix():
    C = CHUNK
    mats = []
    s = C // 2
    while s >= 1:
        d = np.zeros((C, C), np.float32)
        for i in range(C):
            m = (i // (2 * s)) * 2 * s + s
            if (i // s) % 2 == 1:
                d[i, i] += 1.0
                d[i, m - 1] -= 1.0
            else:
                d[i, m - 1] += 1.0
                d[i, i] -= 1.0
        mats.append(d)
        s //= 2
    return jnp.asarray(np.concatenate(mats, axis=0), BF16)


def _expand_matrix():
    e = np.zeros((128, 3 * GROUP_W), np.float32)
    for grp in range(3):
        for h in range(N_HEADS):
            e[grp * N_HEADS + h, grp * GROUP_W + h * HEAD_DIM:grp * GROUP_W + (h + 1) * HEAD_DIM] = 1.0
    return jnp.asarray(e, BF16)


def _relayout_w_in(w_in):
    sizes = (256, 256, 256, 256, 768, 256, 4, 4, 768, 256, 4, 256, 256, 256, 256)
    offs = np.concatenate([[0], np.cumsum(sizes)])
    seg = lambda i: w_in[:, :, offs[i]:offs[i + 1]]
    pad = jnp.zeros(w_in.shape[:2] + (128 - 12,), w_in.dtype)
    cols = [seg(0), seg(1), seg(2), seg(3), seg(4), seg(5), seg(8), seg(9), seg(11), seg(12), seg(13), seg(14),
            seg(6), seg(7), seg(10), pad]
    return jnp.concatenate(cols, axis=-1).astype(BF16)


def _pack_params(norm_w, hgrn_lb_logits, hgrn_norm_w, gdn_conv_w, gdn_a_log, gdn_dt_bias, gdn_norm_w,
                 ssd_conv_w, ssd_conv_b, ssd_a_log, ssd_dt_bias, ssd_d, ssd_norm_w, ret_norm_w, ret_norm_b,
                 final_norm_w):
    depth = norm_w.shape[0]
    prm = jnp.zeros((depth, N_PRM, D_MODEL), F32)

    def put(p, r, a):
        a = a.astype(F32)
        if a.ndim == 2:
            a = a[:, None, :]
        return p.at[:, r:r + a.shape[1], 0:a.shape[2]].set(a)

    rep = lambda a: jnp.repeat(a, HEAD_DIM, axis=-1)
    prm = put(prm, R_NORM, norm_w)
    prm = put(prm, R_LB, jnp.broadcast_to(hgrn_lb_logits[None], (depth,) + hgrn_lb_logits.shape))
    prm = put(prm, R_HNORM, hgrn_norm_w)
    prm = put(prm, R_GCONV, gdn_conv_w)
    prm = put(prm, R_GALOG, rep(gdn_a_log))
    prm = put(prm, R_GDTB, rep(gdn_dt_bias))
    prm = put(prm, R_GNORM, gdn_norm_w)
    prm = put(prm, R_SCONV, ssd_conv_w)
    prm = put(prm, R_SCONVB, ssd_conv_b)
    prm = put(prm, R_SALOG, rep(ssd_a_log))
    prm = put(prm, R_SDTB, rep(ssd_dt_bias))
    prm = put(prm, R_SD, rep(ssd_d))
    prm = put(prm, R_SNORM, ssd_norm_w)
    prm = put(prm, R_RNORM, ret_norm_w)
    prm = put(prm, R_RNORMB, ret_norm_b)
    ret_logg = jnp.log1p(-jnp.exp2(-5.0 - jnp.arange(N_HEADS, dtype=F32)))
    prm = put(prm, R_RLOGG, jnp.broadcast_to(rep(ret_logg)[None], (depth, GROUP_W)))
    prm = put(prm, R_FINAL, jnp.broadcast_to(final_norm_w[None], (depth, D_MODEL)))
    return prm


def _rope_tables(pos):
    half = HEAD_DIM // 2
    inv_freq = 1.0 / (ROPE_BASE ** jnp.linspace(0.0, 1.0, half, dtype=F32))
    ang = pos[:, None] * inv_freq[None, :]
    cos, sin = jnp.cos(ang), jnp.sin(ang)
    cos_h = jnp.concatenate([cos, cos], axis=-1)
    sin_h = jnp.concatenate([-sin, sin], axis=-1)
    return jnp.tile(cos_h, (1, N_HEADS)), jnp.tile(sin_h, (1, N_HEADS))


def _diag_blocks(s, rows, cols):
    return jnp.stack([s[:, (h * rows):(h + 1) * rows, h * cols:(h + 1) * cols] for h in range(N_HEADS)], axis=1)


def _unpack_states(sout):
    hgrn = jnp.swapaxes(_diag_blocks(sout[:, 0], HEAD_DIM, HEAD_DIM), -1, -2)
    gdn = _diag_blocks(sout[:, 1], HEAD_DIM, HEAD_DIM)
    ssd = jnp.stack([sout[:, 2, (h // 2) * 128:(h // 2 + 1) * 128, h * HEAD_DIM:(h + 1) * HEAD_DIM]
                     for h in range(N_HEADS)], axis=1)
    ret = _diag_blocks(sout[:, 3], HEAD_DIM, HEAD_DIM)
    return hgrn, gdn, ssd, ret


def kernel(x_prompt, x_sample, state_hgrn, state_gdn, state_gdn_conv, state_ssd, state_ssd_conv, state_ret,
           meta_tokens, norm_w, w_in, hgrn_lb_logits, hgrn_norm_w, gdn_conv_w, gdn_a_log, gdn_dt_bias, gdn_norm_w,
           ssd_conv_w, ssd_conv_b, ssd_a_log, ssd_dt_bias, ssd_d, ssd_norm_w, ret_norm_w, ret_norm_b,
           w_out, final_norm_w):
    depth = w_in.shape[0]
    bsz, seq, _ = x_prompt.shape
    nb = x_sample.shape[0]
    tile = PROMPT_TILE if seq % PROMPT_TILE == 0 else CHUNK

    win = _relayout_w_in(w_in)
    wout = w_out.astype(BF16)
    prm = _pack_params(norm_w, hgrn_lb_logits, hgrn_norm_w, gdn_conv_w, gdn_a_log, gdn_dt_bias, gdn_norm_w,
                       ssd_conv_w, ssd_conv_b, ssd_a_log, ssd_dt_bias, ssd_d, ssd_norm_w, ret_norm_w, ret_norm_b,
                       final_norm_w)
    dsel = _level_select_matrix()
    expand = _expand_matrix()

    cos_m, sin_m = _rope_tables(jnp.arange(CHUNK, dtype=F32))
    cos_p, sin_p = _rope_tables(N_META + jnp.arange(seq, dtype=F32))
    cos_s, sin_s = _rope_tables(PAST_LEN + jnp.arange(1, dtype=F32))

    hm = jnp.zeros((1, CHUNK, D_MODEL), F32).at[0, :N_META].set(meta_tokens.astype(F32))
    hp = x_prompt.astype(F32)
    hs = x_sample.astype(F32)[:, 0, :]
    zero_s = jnp.zeros((4, GROUP_W, GROUP_W), F32)
    zero_ctx = jnp.zeros((2, 8, CONV_DIM), F32)

    gctx = jnp.swapaxes(state_gdn_conv.astype(F32), 1, 2)
    sctx = jnp.swapaxes(state_ssd_conv.astype(F32), 1, 2)

    p_states, p_ctx = [], []
    s_hgrn, s_gdn, s_ssd, s_ret, s_gctx, s_sctx = [], [], [], [], [], []
    for l in range(depth):
        final = l == depth - 1
        hm, sm, cm = _prompt_layer(l, hm, cos_m, sin_m, win, wout, prm, dsel, expand, zero_s, zero_ctx,
                                   tile=CHUNK, valid=N_META, final=False)
        hp, sp, cp = _prompt_layer(l, hp, cos_p, sin_p, win, wout, prm, dsel, expand, sm[0], cm[0],
                                   tile=tile, valid=tile, final=final)
        p_states.append(sp)
        p_ctx.append(cp)

        rows, cols, post, ug, us = _decode_pre(l, hs, cos_s, sin_s, win, prm, expand, gctx, sctx)
        cols_t = jnp.transpose(cols.reshape(nb // DEC_BLOCK, DEC_BLOCK, N_COLS), (0, 2, 1))
        o, n_hgrn, n_gdn, n_ssd, n_ret = _decode_state(l, rows, cols_t, state_hgrn, state_gdn, state_ssd, state_ret)
        hs = _decode_post(l, hs, o, post, wout, prm, final)
        s_hgrn.append(n_hgrn[0])
        s_gdn.append(n_gdn[0])
        s_ssd.append(n_ssd[0])
        s_ret.append(n_ret[0])
        s_gctx.append(jnp.concatenate([state_gdn_conv[l, :, 1:].astype(F32), ug[:, None, :]], axis=1))
        s_sctx.append(jnp.concatenate([state_ssd_conv[l, :, 1:].astype(F32), us[:, None, :]], axis=1))

    sp = jnp.stack(p_states)
    cp = jnp.stack(p_ctx)
    ph, pg, ps, pr = _unpack_states(sp.reshape((depth * bsz,) + sp.shape[2:]))
    unb = lambda a: a.reshape((depth, bsz) + a.shape[1:])
    return (hp.astype(x_prompt.dtype), hs[:, None, :].astype(x_sample.dtype),
            unb(ph).astype(state_hgrn.dtype), unb(pg).astype(state_gdn.dtype),
            cp[:, :, 0, 8 - (CONV_W - 1):].astype(state_gdn_conv.dtype),
            unb(ps).astype(state_ssd.dtype),
            cp[:, :, 1, 8 - (CONV_W - 1):].astype(state_ssd_conv.dtype),
            unb(pr).astype(state_ret.dtype),
            jnp.stack(s_hgrn).astype(state_hgrn.dtype), jnp.stack(s_gdn).astype(state_gdn.dtype),
            jnp.stack(s_gctx).astype(state_gdn_conv.dtype), jnp.stack(s_ssd).astype(state_ssd.dtype),
            jnp.stack(s_sctx).astype(state_ssd_conv.dtype), jnp.stack(s_ret).astype(state_ret.dtype))
```

```python
import functools
import itertools
import math

import numpy as np
import jax
import jax.numpy as jnp
from jax import lax
from jax.experimental import pallas as pl
from jax.experimental.pallas import tpu as pltpu

F32 = jnp.float32
BF16 = jnp.bfloat16

D_MODEL = 1024
GROUP_W = 256
HEAD_DIM = 64
N_HEADS = 4
CHUNK = 64
CONV_W = 4
CONV_DIM = 768
N_META = 16
PAST_LEN = 16384
ROPE_BASE = 10000.0
EPS = 1e-6
TINY = 1e-30
QK_SCALE = HEAD_DIM ** -0.5
DEPTH = 4

A_Q, A_F, A_I, A_Z = 0, 256, 512, 768
B_QKV, B_Z = 1024, 1792
C_XBC, C_Z = 2048, 2816
D_Q, D_K, D_V, D_Z = 3072, 3328, 3584, 3840
SMALL = 4096
NP = 4224

R_NORM, R_LB, R_HNORM = 0, 1, 5
R_GCONV, R_GALOG, R_GDTB, R_GNORM = 6, 10, 11, 12
R_SCONV, R_SCONVB, R_SALOG, R_SDTB, R_SD, R_SNORM = 13, 17, 18, 19, 20, 21
R_RNORM, R_RNORMB, R_RLOGG, R_FINAL = 22, 23, 24, 25
N_PRM = 32

VMEM_LIMIT = 56 * 1024 * 1024
PROMPT_TILE = 256
DEC_BLOCK = 8
FILL_PER_ROUND = 4
_DONE = object()


def _interleave(leads, fillers, fill_per_round):
    leads = list(leads)
    fillers = iter(fillers)
    fill_live = fill_per_round > 0
    while leads or fill_live:
        leads = [g for g in leads if next(g, _DONE) is not _DONE]
        for _ in range(fill_per_round):
            if fill_live and next(fillers, _DONE) is _DONE:
                fill_live = False


def _dot(a, b):
    return jnp.dot(a.astype(BF16), b.astype(BF16), preferred_element_type=F32)


def _dot_nt(a, b):
    return lax.dot_general(a.astype(BF16), b.astype(BF16), (((1,), (1,)), ((), ())),
                           preferred_element_type=F32)


def _dot_tn(a, b):
    return lax.dot_general(a.astype(BF16), b.astype(BF16), (((0,), (0,)), ((), ())),
                           preferred_element_type=F32)


def _split3(x):
    hi = x.astype(BF16)
    r1 = x - hi.astype(F32)
    mid = r1.astype(BF16)
    lo = (r1 - mid.astype(F32)).astype(BF16)
    return hi, mid, lo


def _dot_exact_rhs(x, m):
    hi, mid, lo = _split3(x)
    return (jnp.dot(hi, m, preferred_element_type=F32) + jnp.dot(mid, m, preferred_element_type=F32)
            + jnp.dot(lo, m, preferred_element_type=F32))


def _dot_exact_lhs(m, x):
    hi, mid, lo = _split3(x)
    return (jnp.dot(m, hi, preferred_element_type=F32) + jnp.dot(m, mid, preferred_element_type=F32)
            + jnp.dot(m, lo, preferred_element_type=F32))


def _sigmoid(x):
    return 1.0 / (1.0 + jnp.exp(-x))


def _silu(x):
    return x * _sigmoid(x)


def _softplus(x):
    return jnp.maximum(x, 0.0) + jnp.log(1.0 + jnp.exp(-jnp.abs(x)))


def _iota(shape, dim):
    return lax.broadcasted_iota(jnp.int32, shape, dim)


def _block_diag(x, mask):
    xb = x.astype(BF16)
    t = jnp.concatenate([xb] * N_HEADS, axis=0)
    return jnp.where(mask, t, jnp.zeros_like(t))


def _chunk_cumsum(x, row_in_chunk):
    sh = 1
    while sh < CHUNK:
        r = pltpu.roll(x, sh, axis=0)
        x = x + jnp.where(row_in_chunk >= sh, r, 0.0)
        sh *= 2
    return x


def _swap_halves(x):
    lane = _iota(x.shape, 1)
    fwd = pltpu.roll(x, GROUP_W - HEAD_DIM // 2, axis=1)
    bwd = pltpu.roll(x, HEAD_DIM // 2, axis=1)
    return jnp.where((lane & (HEAD_DIM - 1)) < HEAD_DIM // 2, fwd, bwd)


def _group_sum(x, ones_bd):
    hi = x.astype(BF16)
    lo = (x - hi.astype(F32)).astype(BF16)
    return jnp.dot(hi, ones_bd, preferred_element_type=F32) + jnp.dot(lo, ones_bd, preferred_element_type=F32)


def _rms(x):
    return x * lax.rsqrt(jnp.mean(x * x, axis=-1, keepdims=True) + EPS)


def _hgrn_lower_bound(logits, layer):
    m = jnp.max(logits, axis=0, keepdims=True)
    e = jnp.exp(logits - m)
    w = e / jnp.sum(e, axis=0, keepdims=True)
    cum = jnp.sum(w[0:layer + 1], axis=0, keepdims=True) - w[0:1]
    return jnp.maximum(cum, 0.0)


def _prow(prm_ref, r, width=GROUP_W, n=1):
    return prm_ref[0, r:r + n, 0:width]


def _hgrn_inputs(proj, lb):
    aq = proj[:, A_Q:A_Q + GROUP_W]
    af = proj[:, A_F:A_F + GROUP_W]
    q = _silu(aq) * QK_SCALE
    k = (1.0 - lb) * _sigmoid(-af)
    f = jnp.maximum(lb + (1.0 - lb) * _sigmoid(af), TINY)
    return q, k, f, proj[:, A_I:A_I + GROUP_W]


def _expand_small(proj, expand):
    return _dot_exact_rhs(proj[:, SMALL:SMALL + 128], expand)


def _l2n(x, ones64):
    return x * lax.rsqrt(_group_sum(x * x, ones64) + EPS)


def _rotary(x, cos, sin):
    return x * cos + _swap_halves(x) * sin


def _post_mix(o_a, o_b, o_c, o_d, z_a, z_b, z_c, z_d, x_ssd, prm_ref, ones64, ones128):
    inv64 = 1.0 / HEAD_DIM
    y_a = o_a * lax.rsqrt(_group_sum(o_a * o_a, ones64) * inv64 + EPS) * _prow(prm_ref, R_HNORM) * _silu(z_a)
    y_b = o_b * lax.rsqrt(_group_sum(o_b * o_b, ones64) * inv64 + EPS) * _prow(prm_ref, R_GNORM) * _silu(z_b)
    t_c = (o_c + _prow(prm_ref, R_SD) * x_ssd) * _silu(z_c)
    y_c = t_c * lax.rsqrt(_group_sum(t_c * t_c, ones128) * (1.0 / 128.0) + EPS) * _prow(prm_ref, R_SNORM)
    mu = _group_sum(o_d, ones64) * inv64
    xc = o_d - mu
    y_d = ((xc * lax.rsqrt(_group_sum(xc * xc, ones64) * inv64 + EPS) * _prow(prm_ref, R_RNORM)
            + _prow(prm_ref, R_RNORMB)) * _silu(z_d))
    return jnp.concatenate([y_a, y_b, y_c, y_d], axis=-1)


def _prompt_kernel(layer, tile, valid, final,
                   h_ref, cos_ref, sin_ref, win_ref, wout_ref, prm_ref, exp_ref, s0_ref, ctx0_ref,
                   hout_ref, sout_ref, ctxout_ref,
                   proj_s, cbuf_g, cbuf_s, st_s, he_s,
                   hq_s, hk_s, hv_s, hg_s,
                   gq_s, gk_s, gv_s, gg_s, gb_s, gqk_s, gsv_s, gsk_s,
                   sx_s, sv_s, sb_s, sc_s, sg_s, sdt_s,
                   rq_s, rk_s, rv_s, rg_s, o_s, y_s):
    t = pl.program_id(1)
    nt = pl.num_programs(1)
    n_chunks = tile // CHUNK
    C = CHUNK

    @pl.when(t == 0)
    def _():
        st_s[...] = s0_ref[...]
        cbuf_g[0:8, :] = ctx0_ref[0]
        cbuf_s[0:8, :] = ctx0_ref[1]

    x = h_ref[0]
    hn = _rms(x) * _prow(prm_ref, R_NORM, D_MODEL)
    proj_s[...] = jnp.dot(hn.astype(BF16), win_ref[0], preferred_element_type=F32)

    row = _iota((tile, GROUP_W), 0)
    row_in_chunk = row & (C - 1)
    if valid < tile:
        live = row < valid
        keep = lambda a: jnp.where(live, a, 0.0)
    else:
        keep = lambda a: a

    ri = _iota((GROUP_W, GROUP_W), 0)
    ci = _iota((GROUP_W, GROUP_W), 1)
    ones64 = jnp.where((ri >> 6) == (ci >> 6), 1.0, 0.0).astype(BF16)
    ones128 = jnp.where((ri >> 7) == (ci >> 7), 1.0, 0.0).astype(BF16)

    def conv_taps(cbuf, u, r_w):
        cbuf[8:8 + tile, :] = u
        cw = _prow(prm_ref, r_w, CONV_DIM, CONV_W)
        out = (cw[3:4] * u + cw[2:3] * cbuf[7:7 + tile, :] + cw[1:2] * cbuf[6:6 + tile, :]
               + cw[0:1] * cbuf[5:5 + tile, :])
        cbuf[0:8, :] = cbuf[tile:tile + 8, :]
        return out

    def gdn_prep():
        small = _expand_small(proj_s[...], exp_ref[...])
        g = -jnp.exp(_prow(prm_ref, R_GALOG)) * _softplus(small[:, 0:256] + _prow(prm_ref, R_GDTB))
        gg_s[...] = _chunk_cumsum(keep(g), row_in_chunk)
        gb_s[...] = keep(_sigmoid(small[:, 256:512]))
        sdt_s[...] = keep(_softplus(small[:, 512:768] + _prow(prm_ref, R_SDTB)))
        yield
        act = _silu(conv_taps(cbuf_g, proj_s[:, B_QKV:B_QKV + CONV_DIM], R_GCONV))
        gq_s[...] = _l2n(act[:, 0:256], ones64) * QK_SCALE
        gk_s[...] = _l2n(act[:, 256:512], ones64)
        gv_s[...] = act[:, 512:768]
        yield

    def hgrn_prep():
        lb = _hgrn_lower_bound(_prow(prm_ref, R_LB, n=DEPTH), layer)
        q, k, f, v = _hgrn_inputs(proj_s[...], lb)
        hq_s[...] = q
        hk_s[...] = k
        hv_s[...] = keep(v)
        yield
        G = _chunk_cumsum(keep(jnp.log(f)), row_in_chunk)
        hg_s[...] = G
        yield
        last = G
        s = 1
        lvl = 5
        while s < C:
            upper = (row & s) != 0
            ref_row = jnp.where(upper, pltpu.roll(last, s, axis=0), last)
            he_s[lvl] = jnp.exp(jnp.minimum(jnp.where(upper, G - ref_row, ref_row - G), 0.0))
            last = jnp.where(upper, last, pltpu.roll(last, tile - s, axis=0))
            s *= 2
            lvl -= 1
            yield

    def ssd_prep():
        act = _silu(conv_taps(cbuf_s, proj_s[:, C_XBC:C_XBC + CONV_DIM], R_SCONV)
                    + _prow(prm_ref, R_SCONVB, CONV_DIM))
        dt = sdt_s[...]
        sx_s[...] = act[:, 0:256]
        sv_s[...] = act[:, 0:256] * dt
        sb_s[...] = act[:, 256:512]
        sc_s[...] = act[:, 512:768]
        yield
        sg_s[...] = _chunk_cumsum(-dt * jnp.exp(_prow(prm_ref, R_SALOG)), row_in_chunk)
        yield

    def ret_prep():
        cos = cos_ref[...]
        sin = sin_ref[...]
        rq_s[...] = _rotary(proj_s[:, D_Q:D_Q + GROUP_W], cos, sin)
        rk_s[...] = _rotary(proj_s[:, D_K:D_K + GROUP_W], cos, sin) * QK_SCALE
        rv_s[...] = keep(proj_s[:, D_V:D_V + GROUP_W])
        yield
        rg_s[...] = _chunk_cumsum(keep(jnp.broadcast_to(_prow(prm_ref, R_RLOGG), (tile, GROUP_W))), row_in_chunk)
        yield

    ii = _iota((C, GROUP_W), 0)
    jj = _iota((C, GROUP_W), 1) & (C - 1)
    tri = jj <= ii
    strict = jj < ii
    eye = jj == ii
    eye_f = jnp.where(eye, 1.0, 0.0)
    bd_mask = (ri >> 6) == (ci >> 6)
    grp_row_mask = (ri >> 7) == (ci >> 7)
    sq_mask = bd_mask
    grp_mask = grp_row_mask

    def decay_parts(G):
        g_row = jnp.sum(jnp.where(eye, G, 0.0), axis=0, keepdims=True)
        dm = jnp.where(tri, jnp.exp(jnp.minimum(G - g_row, 0.0)), 0.0)
        g_last = G[C - 1:C, :]
        return dm, g_last

    def hgrn_a(c):
        sl = pl.ds(c * C, C)
        q = hq_s[sl, :]
        k = hk_s[sl, :]
        scores = jnp.where(eye, _dot_nt(q, _block_diag(k, bd_mask)), 0.0)
        yield
        s = C // 2
        lvl = 0
        while s >= 1:
            e = he_s[lvl, sl, :]
            upper = (ii & s) != 0
            qs = jnp.where(upper, q * e, 0.0)
            ks = jnp.where(upper, 0.0, k * e)
            sc = _dot_nt(qs, _block_diag(ks, bd_mask))
            same_block = (ii & -(2 * s)) == (jj & -(2 * s))
            scores = scores + jnp.where(same_block, sc, 0.0)
            s //= 2
            lvl += 1
            yield
        o_s[sl, 0:256] = _dot(scores, _block_diag(hv_s[sl, :], bd_mask))
        yield

    def gdn_a(c):
        sl = pl.ds(c * C, C)
        q = gq_s[sl, :]
        k = gk_s[sl, :]
        G = gg_s[sl, :]
        beta = gb_s[sl, :]
        dm, _ = decay_parts(G)
        kq = _dot_nt(jnp.concatenate([k, q], axis=0), _block_diag(k, bd_mask))
        yield
        m = jnp.where(strict, kq[0:C] * dm * beta, 0.0)
        gqk_s[sl, :] = kq[C:2 * C] * dm
        p = _dot(m, _block_diag(m, bd_mask))
        yield
        tinv = eye_f - m
        n = 2
        while 2 * n < C:
            both = _dot(jnp.concatenate([p, tinv], axis=0), _block_diag(p, bd_mask))
            yield
            p = both[0:C]
            tinv = tinv + both[C:2 * C]
            n *= 2
        tinv = tinv + _dot(tinv, _block_diag(p, bd_mask))
        yield
        kb = k * beta * jnp.exp(G)
        rhs = jnp.concatenate([_block_diag(gv_s[sl, :] * beta, bd_mask), _block_diag(kb, bd_mask)], axis=1)
        sol = _dot(tinv, rhs)
        gsv_s[sl, :] = sol[:, 0:256]
        gsk_s[sl, :] = sol[:, 256:512]
        yield

    def ssd_a(c):
        sl = pl.ds(c * C, C)
        bm = sb_s[sl, :]
        dm, _ = decay_parts(sg_s[sl, :])
        b_rows = jnp.concatenate([bm.astype(BF16)] * N_HEADS, axis=0)
        b_rows = jnp.where(grp_row_mask, b_rows, jnp.zeros_like(b_rows))
        cb = _dot_nt(sc_s[sl, :], b_rows)
        yield
        o_s[sl, 512:768] = _dot(cb * dm, _block_diag(sv_s[sl, :], bd_mask))
        yield

    def ret_a(c):
        sl = pl.ds(c * C, C)
        dm, _ = decay_parts(rg_s[sl, :])
        sc = _dot_nt(rq_s[sl, :], _block_diag(rk_s[sl, :], bd_mask)) * dm
        yield
        o_s[sl, 768:1024] = _dot(sc, _block_diag(rv_s[sl, :], bd_mask))
        yield

    def hgrn_b(c):
        sl = pl.ds(c * C, C)
        G = hg_s[sl, :]
        g_last = G[C - 1:C, :]
        st = st_s[0]
        o_s[sl, 0:256] += _dot_nt(hq_s[sl, :] * jnp.exp(G), st)
        yield
        kt = hk_s[sl, :] * jnp.exp(g_last - G)
        st_s[0] = st * jnp.exp(g_last) + jnp.where(sq_mask, _dot_tn(hv_s[sl, :], kt), 0.0)
        yield

    def gdn_b(c):
        sl = pl.ds(c * C, C)
        G = gg_s[sl, :]
        g_last = G[C - 1:C, :]
        st = st_s[1]
        tmp = _dot(jnp.concatenate([gsk_s[sl, :], gq_s[sl, :] * jnp.exp(G)], axis=0), st)
        yield
        u = gsv_s[sl, :] - tmp[0:C]
        kt = gk_s[sl, :] * jnp.exp(g_last - G)
        st_s[1] = st * jnp.exp(g_last) + jnp.where(sq_mask, _dot_tn(kt, u), 0.0)
        yield
        o_s[sl, 256:512] = tmp[C:2 * C] + _dot(gqk_s[sl, :], _block_diag(u, bd_mask))
        yield

    def ssd_b(c):
        sl = pl.ds(c * C, C)
        G = sg_s[sl, :]
        g_last = G[C - 1:C, :]
        st = st_s[2]
        o_s[sl, 512:768] += jnp.exp(G) * _dot(sc_s[sl, :], st)
        yield
        vt = sv_s[sl, :] * jnp.exp(g_last - G)
        st_s[2] = st * jnp.exp(g_last) + jnp.where(grp_mask, _dot_tn(sb_s[sl, :], vt), 0.0)
        yield

    def ret_b(c):
        sl = pl.ds(c * C, C)
        G = rg_s[sl, :]
        g_last = G[C - 1:C, :]
        st = st_s[3]
        o_s[sl, 768:1024] += jnp.exp(G) * _dot(rq_s[sl, :], st)
        yield
        vt = rv_s[sl, :] * jnp.exp(g_last - G)
        st_s[3] = st * jnp.exp(g_last) + jnp.where(sq_mask, _dot_tn(rk_s[sl, :], vt), 0.0)
        yield

    def post(c):
        sl = pl.ds(c * C, C)
        y_s[sl, :] = _post_mix(o_s[sl, 0:256], o_s[sl, 256:512], o_s[sl, 512:768], o_s[sl, 768:1024],
                               proj_s[sl, A_Z:A_Z + GROUP_W], proj_s[sl, B_Z:B_Z + GROUP_W],
                               proj_s[sl, C_Z:C_Z + GROUP_W], proj_s[sl, D_Z:D_Z + GROUP_W],
                               sx_s[sl, :], prm_ref, ones64, ones128).astype(BF16)
        yield

    chunks = range(n_chunks)
    every = lambda *gens: itertools.chain.from_iterable(gens)
    _interleave([gdn_prep()], (), 0)
    _interleave([gdn_a(c) for c in chunks],
                every(hgrn_prep(), *[hgrn_a(c) for c in chunks], ssd_prep(), *[ssd_a(c) for c in chunks],
                      ret_prep(), *[ret_a(c) for c in chunks]),
                FILL_PER_ROUND)
    for c in chunks:
        done = [post(c - 1)] if c else []
        _interleave([gdn_b(c)], every(hgrn_b(c), ssd_b(c), ret_b(c), *done), 2)
    _interleave([post(n_chunks - 1)], (), 0)

    out = x + jnp.dot(y_s[...], wout_ref[0], preferred_element_type=F32)
    if final:
        out = _rms(out) * _prow(prm_ref, R_FINAL, D_MODEL)
    hout_ref[0] = out

    @pl.when(t == nt - 1)
    def _():
        sout_ref[0] = st_s[...]
        ctxout_ref[0, 0] = cbuf_g[valid:valid + 8, :]
        ctxout_ref[0, 1] = cbuf_s[valid:valid + 8, :]


def _prompt_layer(layer, h, cos, sin, win, wout, prm, expand, s0, ctx0, *, tile, valid, final):
    bsz, seq, _ = h.shape
    nt = seq // tile
    kern = functools.partial(_prompt_kernel, layer, tile, valid, final)
    const2 = lambda b, t: (0, 0)
    slab = lambda: pltpu.VMEM((tile, GROUP_W), F32)
    return pl.pallas_call(
        kern,
        grid=(bsz, nt),
        in_specs=[
            pl.BlockSpec((1, tile, D_MODEL), lambda b, t: (b, t, 0)),
            pl.BlockSpec((tile, GROUP_W), lambda b, t: (t, 0)),
            pl.BlockSpec((tile, GROUP_W), lambda b, t: (t, 0)),
            pl.BlockSpec((1, D_MODEL, NP), lambda b, t: (layer, 0, 0)),
            pl.BlockSpec((1, D_MODEL, D_MODEL), lambda b, t: (layer, 0, 0)),
            pl.BlockSpec((1, N_PRM, D_MODEL), lambda b, t: (layer, 0, 0)),
            pl.BlockSpec(expand.shape, const2),
            pl.BlockSpec((4, GROUP_W, GROUP_W), lambda b, t: (0, 0, 0)),
            pl.BlockSpec((2, 8, CONV_DIM), lambda b, t: (0, 0, 0)),
        ],
        out_specs=[
            pl.BlockSpec((1, tile, D_MODEL), lambda b, t: (b, t, 0)),
            pl.BlockSpec((1, 4, GROUP_W, GROUP_W), lambda b, t: (b, 0, 0, 0)),
            pl.BlockSpec((1, 2, 8, CONV_DIM), lambda b, t: (b, 0, 0, 0)),
        ],
        out_shape=[
            jax.ShapeDtypeStruct((bsz, seq, D_MODEL), F32),
            jax.ShapeDtypeStruct((bsz, 4, GROUP_W, GROUP_W), F32),
            jax.ShapeDtypeStruct((bsz, 2, 8, CONV_DIM), F32),
        ],
        scratch_shapes=[
            pltpu.VMEM((tile, NP), F32),
            pltpu.VMEM((tile + 8, CONV_DIM), F32),
            pltpu.VMEM((tile + 8, CONV_DIM), F32),
            pltpu.VMEM((4, GROUP_W, GROUP_W), F32),
            pltpu.VMEM((6, tile, GROUP_W), F32),
        ] + [slab() for _ in range(22)] + [pltpu.VMEM((tile, D_MODEL), F32), pltpu.VMEM((tile, D_MODEL), BF16)],
        compiler_params=pltpu.CompilerParams(
            dimension_semantics=("arbitrary", "arbitrary"), vmem_limit_bytes=VMEM_LIMIT),
        name=f"prompt_layer_t{tile}",
    )(h, cos, sin, win, wout, prm, expand, s0, ctx0)


(VT_HQ, VT_HK, VT_HF, VT_HV, VT_GQ, VT_GK, VT_GV, VT_GDEC, VT_GBETA,
 VT_SV, VT_SB, VT_SC, VT_SDEC, VT_RQ, VT_RK, VT_RV, VT_RDEC) = [GROUP_W * i for i in range(17)]
N_VT = 17 * GROUP_W
DEC_VBLOCK = 32


def _dec_pre_kernel(layer, hs_ref, cos_ref, sin_ref, win_ref, prm_ref, exp_ref, gctx_ref, sctx_ref,
                    vt_ref, post_ref, ug_ref, us_ref):
    x = hs_ref[...]
    nb = x.shape[0]
    hn = _rms(x) * _prow(prm_ref, R_NORM, D_MODEL)
    proj = jnp.dot(hn.astype(BF16), win_ref[0], preferred_element_type=F32)

    ri = _iota((GROUP_W, GROUP_W), 0)
    ci = _iota((GROUP_W, GROUP_W), 1)
    ones64 = jnp.where((ri >> 6) == (ci >> 6), 1.0, 0.0).astype(BF16)

    def put(off, a):
        vt_ref[off:off + GROUP_W, :] = jnp.transpose(a)

    lb = _hgrn_lower_bound(_prow(prm_ref, R_LB, n=DEPTH), layer)
    q, k, f, v = _hgrn_inputs(proj, lb)
    put(VT_HQ, q)
    put(VT_HK, k)
    put(VT_HF, f)
    put(VT_HV, v)

    small = _expand_small(proj, exp_ref[...])

    u = proj[:, B_QKV:B_QKV + CONV_DIM]
    ug_ref[...] = u
    cw = _prow(prm_ref, R_GCONV, CONV_DIM, CONV_W)
    conv = cw[3:4] * u + cw[2:3] * gctx_ref[0, 2] + cw[1:2] * gctx_ref[0, 1] + cw[0:1] * gctx_ref[0, 0]
    act = _silu(conv)
    put(VT_GQ, _l2n(act[:, 0:256], ones64) * QK_SCALE)
    put(VT_GK, _l2n(act[:, 256:512], ones64))
    put(VT_GV, act[:, 512:768])
    g = -jnp.exp(_prow(prm_ref, R_GALOG)) * _softplus(small[:, 0:256] + _prow(prm_ref, R_GDTB))
    put(VT_GDEC, jnp.exp(g))
    put(VT_GBETA, _sigmoid(small[:, 256:512]))

    u = proj[:, C_XBC:C_XBC + CONV_DIM]
    us_ref[...] = u
    cw = _prow(prm_ref, R_SCONV, CONV_DIM, CONV_W)
    conv = (cw[3:4] * u + cw[2:3] * sctx_ref[0, 2] + cw[1:2] * sctx_ref[0, 1] + cw[0:1] * sctx_ref[0, 0]
            + _prow(prm_ref, R_SCONVB, CONV_DIM))
    act = _silu(conv)
    dt = _softplus(small[:, 512:768] + _prow(prm_ref, R_SDTB))
    post_ref[:, POST_SX:POST_SX + 256] = act[:, 0:256]
    put(VT_SV, act[:, 0:256] * dt)
    put(VT_SB, act[:, 256:512])
    put(VT_SC, act[:, 512:768])
    put(VT_SDEC, jnp.exp(-dt * jnp.exp(_prow(prm_ref, R_SALOG))))

    cos = cos_ref[...]
    sin = sin_ref[...]
    put(VT_RQ, _rotary(proj[:, D_Q:D_Q + GROUP_W], cos, sin))
    put(VT_RK, _rotary(proj[:, D_K:D_K + GROUP_W], cos, sin) * QK_SCALE)
    put(VT_RV, proj[:, D_V:D_V + GROUP_W])
    put(VT_RDEC, jnp.broadcast_to(jnp.exp(_prow(prm_ref, R_RLOGG)), (nb, GROUP_W)))

    post_ref[:, POST_AZ:POST_AZ + 256] = proj[:, A_Z:A_Z + GROUP_W]
    post_ref[:, POST_BZ:POST_BZ + 256] = proj[:, B_Z:B_Z + GROUP_W]
    post_ref[:, POST_CZ:POST_CZ + 256] = proj[:, C_Z:C_Z + GROUP_W]
    post_ref[:, POST_DZ:POST_DZ + 256] = proj[:, D_Z:D_Z + GROUP_W]


def _dec_pre(layer, hs, cos, sin, win, prm, expand, gctx, sctx):
    nb = hs.shape[0]
    full = lambda a: pl.BlockSpec(a.shape, lambda i: (0,) * a.ndim)
    shapes = [(N_VT, nb), (nb, N_POST), (nb, CONV_DIM), (nb, CONV_DIM)]
    return pl.pallas_call(
        functools.partial(_dec_pre_kernel, layer),
        grid=(1,),
        in_specs=[
            full(hs), full(cos), full(sin),
            pl.BlockSpec((1, D_MODEL, NP), lambda i: (layer, 0, 0)),
            pl.BlockSpec((1, N_PRM, D_MODEL), lambda i: (layer, 0, 0)),
            full(expand),
            pl.BlockSpec((1, 3, nb, CONV_DIM), lambda i: (layer, 0, 0, 0)),
            pl.BlockSpec((1, 3, nb, CONV_DIM), lambda i: (layer, 0, 0, 0)),
        ],
        out_specs=[pl.BlockSpec(s, lambda i: (0, 0)) for s in shapes],
        out_shape=[jax.ShapeDtypeStruct(s, F32) for s in shapes],
        compiler_params=pltpu.CompilerParams(dimension_semantics=("arbitrary",), vmem_limit_bytes=VMEM_LIMIT),
        name="decode_pre",
    )(hs, cos, sin, win, prm, expand, gctx, sctx)


def _dec_state_kernel(n_alias, vt_ref, hg_ref, gd_ref, sd_ref, rt_ref, *refs):
    o_ref, hg_out, gd_out, sd_out, rt_out = refs[n_alias:]
    h = pl.program_id(0)
    j = pl.program_id(1)
    vb = DEC_VBLOCK
    nb = vt_ref.shape[1]
    head = h * HEAD_DIM

    def row(off, i):
        return vt_ref[pl.ds(off + i, 1), :]

    def slab(off):
        return vt_ref[pl.ds(pl.multiple_of(off + head + j * vb, 8), vb), :]

    zero = jnp.zeros((vb, nb), F32)

    v = slab(VT_HV)

    def hgrn_step(k, o):
        s = row(VT_HF + head, k) * hg_ref[0, 0, k] + row(VT_HK + head, k) * v
        hg_out[0, 0, k] = s
        return o + row(VT_HQ + head, k) * s

    o_ref[0, 0] = lax.fori_loop(0, HEAD_DIM, hgrn_step, zero, unroll=4)

    v = slab(VT_GV)
    dec = row(VT_GDEC + head, 0)
    ks = lax.fori_loop(0, HEAD_DIM, lambda k, a: a + row(VT_GK + head, k) * gd_ref[0, 0, k], zero, unroll=4)
    u = row(VT_GBETA + head, 0) * (v - ks * dec)

    def gdn_step(k, o):
        s = gd_ref[0, 0, k] * dec + row(VT_GK + head, k) * u
        gd_out[0, 0, k] = s
        return o + row(VT_GQ + head, k) * s

    o_ref[1, 0] = lax.fori_loop(0, HEAD_DIM, gdn_step, zero, unroll=4)

    v = slab(VT_SV)
    dec = row(VT_SDEC + head, 0)
    grp = (h >> 1) * 128

    def ssd_step(n, o):
        s = sd_ref[0, 0, n] * dec + row(VT_SB + grp, n) * v
        sd_out[0, 0, n] = s
        return o + row(VT_SC + grp, n) * s

    o_ref[2, 0] = lax.fori_loop(0, 128, ssd_step, zero, unroll=4)

    v = slab(VT_RV)
    dec = row(VT_RDEC + head, 0)

    def ret_step(k, o):
        s = rt_ref[0, 0, k] * dec + row(VT_RK + head, k) * v
        rt_out[0, 0, k] = s
        return o + row(VT_RQ + head, k) * s

    o_ref[3, 0] = lax.fori_loop(0, HEAD_DIM, ret_step, zero, unroll=4)


def _dec_state(layer, vt, states, accs):
    nb = vt.shape[1]
    vb = DEC_VBLOCK
    depth = states[0].shape[0]
    blk = lambda a: pl.BlockSpec((1, 1, a.shape[2], vb, nb), lambda h, j: (layer, h, 0, j, 0))
    n_alias = 0 if accs is None else 4
    in_specs = [pl.BlockSpec(vt.shape, lambda h, j: (0, 0))] + [blk(a) for a in states]
    operands = [vt] + list(states)
    if accs is not None:
        in_specs += [pl.BlockSpec(memory_space=pl.ANY)] * 4
        operands += list(accs)
    return pl.pallas_call(
        functools.partial(_dec_state_kernel, n_alias),
        grid=(N_HEADS, HEAD_DIM // vb),
        in_specs=in_specs,
        out_specs=[pl.BlockSpec((4, 1, vb, nb), lambda h, j: (0, h, j, 0))] + [blk(a) for a in states],
        out_shape=[jax.ShapeDtypeStruct((4, N_HEADS, HEAD_DIM, nb), F32)]
        + [jax.ShapeDtypeStruct(a.shape, F32) for a in states],
        input_output_aliases={5 + i: 1 + i for i in range(n_alias)},
        compiler_params=pltpu.CompilerParams(dimension_semantics=("arbitrary", "arbitrary"),
                                             vmem_limit_bytes=VMEM_LIMIT),
        name="decode_state",
    )(*operands)


def _dec_post_kernel(final, hs_ref, ot_ref, post_ref, wout_ref, prm_ref, out_ref):
    ri = _iota((GROUP_W, GROUP_W), 0)
    ci = _iota((GROUP_W, GROUP_W), 1)
    ones64 = jnp.where((ri >> 6) == (ci >> 6), 1.0, 0.0).astype(BF16)
    ones128 = jnp.where((ri >> 7) == (ci >> 7), 1.0, 0.0).astype(BF16)
    gate = lambda off: post_ref[:, off:off + GROUP_W]
    o = [jnp.transpose(ot_ref[m * GROUP_W:(m + 1) * GROUP_W, :]) for m in range(4)]
    y = _post_mix(o[0], o[1], o[2], o[3],
                  gate(POST_AZ), gate(POST_BZ), gate(POST_CZ), gate(POST_DZ), gate(POST_SX),
                  prm_ref, ones64, ones128)
    out = hs_ref[...] + jnp.dot(y.astype(BF16), wout_ref[0], preferred_element_type=F32)
    if final:
        out = _rms(out) * _prow(prm_ref, R_FINAL, D_MODEL)
    out_ref[...] = out


def _dec_post(layer, hs, ot, post, wout, prm, final):
    nb = hs.shape[0]
    full = lambda a: pl.BlockSpec(a.shape, lambda i: (0,) * a.ndim)
    return pl.pallas_call(
        functools.partial(_dec_post_kernel, final),
        grid=(1,),
        in_specs=[full(hs), full(ot), full(post),
                  pl.BlockSpec((1, D_MODEL, D_MODEL), lambda i: (layer, 0, 0)),
                  pl.BlockSpec((1, N_PRM, D_MODEL), lambda i: (layer, 0, 0))],
        out_specs=pl.BlockSpec((nb, D_MODEL), lambda i: (0, 0)),
        out_shape=jax.ShapeDtypeStruct((nb, D_MODEL), F32),
        compiler_params=pltpu.CompilerParams(dimension_semantics=("arbitrary",), vmem_limit_bytes=VMEM_LIMIT),
        name="decode_post",
    )(hs, ot, post, wout, prm)


POST_AZ, POST_BZ, POST_CZ, POST_DZ, POST_SX = [GROUP_W * i for i in range(5)]
N_POST = 5 * GROUP_W


def _expand_matrix():
    e = np.zeros((128, 3 * GROUP_W), np.float32)
    for grp in range(3):
        for h in range(N_HEADS):
            e[grp * N_HEADS + h, grp * GROUP_W + h * HEAD_DIM:grp * GROUP_W + (h + 1) * HEAD_DIM] = 1.0
    return jnp.asarray(e, BF16)


def _relayout_w_in(w_in):
    sizes = (256, 256, 256, 256, 768, 256, 4, 4, 768, 256, 4, 256, 256, 256, 256)
    offs = np.concatenate([[0], np.cumsum(sizes)])
    seg = lambda i: w_in[:, :, offs[i]:offs[i + 1]]
    pad = jnp.zeros(w_in.shape[:2] + (128 - 12,), w_in.dtype)
    cols = [seg(0), seg(1), seg(2), seg(3), seg(4), seg(5), seg(8), seg(9), seg(11), seg(12), seg(13), seg(14),
            seg(6), seg(7), seg(10), pad]
    return jnp.concatenate(cols, axis=-1).astype(BF16)


def _pack_params(norm_w, hgrn_lb_logits, hgrn_norm_w, gdn_conv_w, gdn_a_log, gdn_dt_bias, gdn_norm_w,
                 ssd_conv_w, ssd_conv_b, ssd_a_log, ssd_dt_bias, ssd_d, ssd_norm_w, ret_norm_w, ret_norm_b,
                 final_norm_w):
    depth = norm_w.shape[0]
    prm = jnp.zeros((depth, N_PRM, D_MODEL), F32)

    def put(p, r, a):
        a = a.astype(F32)
        if a.ndim == 2:
            a = a[:, None, :]
        return p.at[:, r:r + a.shape[1], 0:a.shape[2]].set(a)

    rep = lambda a: jnp.repeat(a, HEAD_DIM, axis=-1)
    prm = put(prm, R_NORM, norm_w)
    prm = put(prm, R_LB, jnp.broadcast_to(hgrn_lb_logits[None], (depth,) + hgrn_lb_logits.shape))
    prm = put(prm, R_HNORM, hgrn_norm_w)
    prm = put(prm, R_GCONV, gdn_conv_w)
    prm = put(prm, R_GALOG, rep(gdn_a_log))
    prm = put(prm, R_GDTB, rep(gdn_dt_bias))
    prm = put(prm, R_GNORM, gdn_norm_w)
    prm = put(prm, R_SCONV, ssd_conv_w)
    prm = put(prm, R_SCONVB, ssd_conv_b)
    prm = put(prm, R_SALOG, rep(ssd_a_log))
    prm = put(prm, R_SDTB, rep(ssd_dt_bias))
    prm = put(prm, R_SD, rep(ssd_d))
    prm = put(prm, R_SNORM, ssd_norm_w)
    prm = put(prm, R_RNORM, ret_norm_w)
    prm = put(prm, R_RNORMB, ret_norm_b)
    ret_logg = jnp.log1p(-jnp.exp2(-5.0 - jnp.arange(N_HEADS, dtype=F32)))
    prm = put(prm, R_RLOGG, jnp.broadcast_to(rep(ret_logg)[None], (depth, GROUP_W)))
    prm = put(prm, R_FINAL, jnp.broadcast_to(final_norm_w[None], (depth, D_MODEL)))
    return prm


def _rope_tables(pos):
    half = HEAD_DIM // 2
    inv_freq = 1.0 / (ROPE_BASE ** jnp.linspace(0.0, 1.0, half, dtype=F32))
    ang = pos[:, None] * inv_freq[None, :]
    cos, sin = jnp.cos(ang), jnp.sin(ang)
    cos_h = jnp.concatenate([cos, cos], axis=-1)
    sin_h = jnp.concatenate([-sin, sin], axis=-1)
    return jnp.tile(cos_h, (1, N_HEADS)), jnp.tile(sin_h, (1, N_HEADS))


def _diag_blocks(s, rows, cols):
    return jnp.stack([s[:, (h * rows):(h + 1) * rows, h * cols:(h + 1) * cols] for h in range(N_HEADS)], axis=1)


def _unpack_states(sout):
    hgrn = jnp.swapaxes(_diag_blocks(sout[:, 0], HEAD_DIM, HEAD_DIM), -1, -2)
    gdn = _diag_blocks(sout[:, 1], HEAD_DIM, HEAD_DIM)
    ssd = jnp.stack([sout[:, 2, (h // 2) * 128:(h // 2 + 1) * 128, h * HEAD_DIM:(h + 1) * HEAD_DIM]
                     for h in range(N_HEADS)], axis=1)
    ret = _diag_blocks(sout[:, 3], HEAD_DIM, HEAD_DIM)
    return hgrn, gdn, ssd, ret


def kernel(x_prompt, x_sample, state_hgrn, state_gdn, state_gdn_conv, state_ssd, state_ssd_conv, state_ret,
           meta_tokens, norm_w, w_in, hgrn_lb_logits, hgrn_norm_w, gdn_conv_w, gdn_a_log, gdn_dt_bias, gdn_norm_w,
           ssd_conv_w, ssd_conv_b, ssd_a_log, ssd_dt_bias, ssd_d, ssd_norm_w, ret_norm_w, ret_norm_b,
           w_out, final_norm_w):
    depth = w_in.shape[0]
    bsz, seq, _ = x_prompt.shape
    nb = x_sample.shape[0]
    tile = PROMPT_TILE if seq % PROMPT_TILE == 0 else CHUNK

    win = _relayout_w_in(w_in)
    wout = w_out.astype(BF16)
    prm = _pack_params(norm_w, hgrn_lb_logits, hgrn_norm_w, gdn_conv_w, gdn_a_log, gdn_dt_bias, gdn_norm_w,
                       ssd_conv_w, ssd_conv_b, ssd_a_log, ssd_dt_bias, ssd_d, ssd_norm_w, ret_norm_w, ret_norm_b,
                       final_norm_w)
    expand = _expand_matrix()

    cos_m, sin_m = _rope_tables(jnp.arange(CHUNK, dtype=F32))
    cos_p, sin_p = _rope_tables(N_META + jnp.arange(seq, dtype=F32))
    cos_s, sin_s = _rope_tables(PAST_LEN + jnp.arange(1, dtype=F32))

    hm = jnp.zeros((1, CHUNK, D_MODEL), F32).at[0, :N_META].set(meta_tokens.astype(F32))
    hp = x_prompt.astype(F32)
    hs = x_sample.astype(F32)[:, 0, :]
    zero_s = jnp.zeros((4, GROUP_W, GROUP_W), F32)
    zero_ctx = jnp.zeros((2, 8, CONV_DIM), F32)

    gctx = jnp.swapaxes(state_gdn_conv.astype(F32), 1, 2)
    sctx = jnp.swapaxes(state_ssd_conv.astype(F32), 1, 2)

    seq_minor = lambda a: jnp.transpose(a.astype(F32), (0, 2, 3, 4, 1))
    dec_states = [seq_minor(a) for a in (state_hgrn, state_gdn, state_ssd, state_ret)]
    dec_new = None

    p_states, p_ctx = [], []
    s_gctx, s_sctx = [], []
    for l in range(depth):
        final = l == depth - 1
        hm, sm, cm = _prompt_layer(l, hm, cos_m, sin_m, win, wout, prm, expand, zero_s, zero_ctx,
                                   tile=CHUNK, valid=N_META, final=False)
        hp, sp, cp = _prompt_layer(l, hp, cos_p, sin_p, win, wout, prm, expand, sm[0], cm[0],
                                   tile=tile, valid=tile, final=final)
        p_states.append(sp)
        p_ctx.append(cp)

        vt, post, ug, us = _dec_pre(l, hs, cos_s, sin_s, win, prm, expand, gctx, sctx)
        ot, *dec_new = _dec_state(l, vt, dec_states, dec_new)
        hs = _dec_post(l, hs, ot.reshape(4 * GROUP_W, nb), post, wout, prm, final)
        s_gctx.append(jnp.concatenate([state_gdn_conv[l, :, 1:].astype(F32), ug[:, None, :]], axis=1))
        s_sctx.append(jnp.concatenate([state_ssd_conv[l, :, 1:].astype(F32), us[:, None, :]], axis=1))

    sp = jnp.stack(p_states)
    cp = jnp.stack(p_ctx)
    ph, pg, ps, pr = _unpack_states(sp.reshape((depth * bsz,) + sp.shape[2:]))
    unb = lambda a: a.reshape((depth, bsz) + a.shape[1:])
    seq_major = lambda a: jnp.transpose(a, (0, 4, 1, 2, 3))
    return (hp.astype(x_prompt.dtype), hs[:, None, :].astype(x_sample.dtype),
            unb(ph).astype(state_hgrn.dtype), unb(pg).astype(state_gdn.dtype),
            cp[:, :, 0, 8 - (CONV_W - 1):].astype(state_gdn_conv.dtype),
            unb(ps).astype(state_ssd.dtype),
            cp[:, :, 1, 8 - (CONV_W - 1):].astype(state_ssd_conv.dtype),
            unb(pr).astype(state_ret.dtype),
            seq_major(dec_new[0]).astype(state_hgrn.dtype), seq_major(dec_new[1]).astype(state_gdn.dtype),
            jnp.stack(s_gctx).astype(state_gdn_conv.dtype), seq_major(dec_new[2]).astype(state_ssd.dtype),
            jnp.stack(s_sctx).astype(state_ssd_conv.dtype), seq_major(dec_new[3]).astype(state_ret.dtype))
```

```python
import functools
import itertools
import math

import numpy as np
import jax
import jax.numpy as jnp
from jax import lax
from jax.experimental import pallas as pl
from jax.experimental.pallas import tpu as pltpu

F32 = jnp.float32
BF16 = jnp.bfloat16

D_MODEL = 1024
GROUP_W = 256
HEAD_DIM = 64
N_HEADS = 4
CHUNK = 64
CONV_W = 4
CONV_DIM = 768
N_META = 16
PAST_LEN = 16384
ROPE_BASE = 10000.0
EPS = 1e-6
TINY = 1e-30
QK_SCALE = HEAD_DIM ** -0.5
DEPTH = 4

A_Q, A_F, A_I, A_Z = 0, 256, 512, 768
B_QKV, B_Z = 1024, 1792
C_XBC, C_Z = 2048, 2816
D_Q, D_K, D_V, D_Z = 3072, 3328, 3584, 3840
SMALL = 4096
NP = 4224

R_NORM, R_LB, R_HNORM = 0, 1, 5
R_GCONV, R_GNORM = 6, 12
R_SCONV, R_SCONVB, R_SD, R_SNORM = 13, 17, 20, 21
R_RNORM, R_RNORMB, R_RLOGG, R_FINAL = 22, 23, 24, 25
R_CBIAS, R_CALOG = 26, 27
N_PRM = 32

VMEM_LIMIT = 56 * 1024 * 1024
PROMPT_TILE = 256
DEC_BLOCK = 8
HGRN_SAFE_SPAN = 80.0
FILL_PER_ROUND = 4
_DONE = object()


def _interleave(leads, fillers, fill_per_round):
    leads = list(leads)
    fillers = iter(fillers)
    fill_live = fill_per_round > 0
    while leads or fill_live:
        leads = [g for g in leads if next(g, _DONE) is not _DONE]
        for _ in range(fill_per_round):
            if fill_live and next(fillers, _DONE) is _DONE:
                fill_live = False


def _dot(a, b):
    return jnp.dot(a.astype(BF16), b.astype(BF16), preferred_element_type=F32)


def _dot_nt(a, b):
    return lax.dot_general(a.astype(BF16), b.astype(BF16), (((1,), (1,)), ((), ())),
                           preferred_element_type=F32)


def _dot_tn(a, b):
    return lax.dot_general(a.astype(BF16), b.astype(BF16), (((0,), (0,)), ((), ())),
                           preferred_element_type=F32)


def _split3(x):
    hi = x.astype(BF16)
    r1 = x - hi.astype(F32)
    mid = r1.astype(BF16)
    lo = (r1 - mid.astype(F32)).astype(BF16)
    return hi, mid, lo


def _dot_exact_rhs(x, m):
    hi, mid, lo = _split3(x)
    return (jnp.dot(hi, m, preferred_element_type=F32) + jnp.dot(mid, m, preferred_element_type=F32)
            + jnp.dot(lo, m, preferred_element_type=F32))


def _dot_exact_lhs(m, x):
    hi, mid, lo = _split3(x)
    return (jnp.dot(m, hi, preferred_element_type=F32) + jnp.dot(m, mid, preferred_element_type=F32)
            + jnp.dot(m, lo, preferred_element_type=F32))


def _sigmoid(x):
    return 1.0 / (1.0 + jnp.exp(-x))


def _silu(x):
    return x * _sigmoid(x)


def _softplus(x):
    return jnp.maximum(x, 0.0) + jnp.log(1.0 + jnp.exp(-jnp.abs(x)))


def _iota(shape, dim):
    return lax.broadcasted_iota(jnp.int32, shape, dim)


def _block_diag(x, mask01):
    xb = x.astype(BF16)
    return jnp.concatenate([xb] * N_HEADS, axis=0) * mask01


def _chunk_cumsum(x, row_in_chunk):
    sh = 1
    while sh < CHUNK:
        r = pltpu.roll(x, sh, axis=0)
        x = x + jnp.where(row_in_chunk >= sh, r, 0.0)
        sh *= 2
    return x


def _swap_halves(x):
    lane = _iota(x.shape, 1)
    fwd = pltpu.roll(x, GROUP_W - HEAD_DIM // 2, axis=1)
    bwd = pltpu.roll(x, HEAD_DIM // 2, axis=1)
    return jnp.where((lane & (HEAD_DIM - 1)) < HEAD_DIM // 2, fwd, bwd)


def _group_sum(x, ones_bd):
    hi = x.astype(BF16)
    lo = (x - hi.astype(F32)).astype(BF16)
    return jnp.dot(hi, ones_bd, preferred_element_type=F32) + jnp.dot(lo, ones_bd, preferred_element_type=F32)


def _rms(x):
    return x * lax.rsqrt(jnp.mean(x * x, axis=-1, keepdims=True) + EPS)


def _hgrn_lower_bound(logits, layer):
    m = jnp.max(logits, axis=0, keepdims=True)
    e = jnp.exp(logits - m)
    w = e / jnp.sum(e, axis=0, keepdims=True)
    cum = jnp.sum(w[0:layer + 1], axis=0, keepdims=True) - w[0:1]
    return jnp.maximum(cum, 0.0)


def _prow(prm_ref, r, width=GROUP_W, n=1):
    return prm_ref[0, r:r + n, 0:width]


def _hgrn_inputs(proj, lb):
    aq = proj[:, A_Q:A_Q + GROUP_W]
    af = proj[:, A_F:A_F + GROUP_W]
    q = _silu(aq) * QK_SCALE
    k = (1.0 - lb) * _sigmoid(-af)
    f = jnp.maximum(lb + (1.0 - lb) * _sigmoid(af), TINY)
    return q, k, f, proj[:, A_I:A_I + GROUP_W]


def _compact_gates(c, prm_ref):
    sp = _softplus(c + _prow(prm_ref, R_CBIAS, 128))
    log_decay = -jnp.exp(_prow(prm_ref, R_CALOG, 128)) * sp
    lane = _iota(c.shape, 1)
    vals = jnp.where((lane >= N_HEADS) & (lane < 2 * N_HEADS), _sigmoid(c), sp)
    return log_decay, vals


def _l2n(x, ones64):
    return x * lax.rsqrt(_group_sum(x * x, ones64) + EPS)


def _rotary(x, cos, sin):
    return x * cos + _swap_halves(x) * sin


def _post_mix(o_a, o_b, o_c, o_d, z_a, z_b, z_c, z_d, x_ssd, prm_ref, ones64, ones128):
    inv64 = 1.0 / HEAD_DIM
    y_a = o_a * lax.rsqrt(_group_sum(o_a * o_a, ones64) * inv64 + EPS) * _prow(prm_ref, R_HNORM) * _silu(z_a)
    y_b = o_b * lax.rsqrt(_group_sum(o_b * o_b, ones64) * inv64 + EPS) * _prow(prm_ref, R_GNORM) * _silu(z_b)
    t_c = (o_c + _prow(prm_ref, R_SD) * x_ssd) * _silu(z_c)
    y_c = t_c * lax.rsqrt(_group_sum(t_c * t_c, ones128) * (1.0 / 128.0) + EPS) * _prow(prm_ref, R_SNORM)
    mu = _group_sum(o_d, ones64) * inv64
    xc = o_d - mu
    y_d = ((xc * lax.rsqrt(_group_sum(xc * xc, ones64) * inv64 + EPS) * _prow(prm_ref, R_RNORM)
            + _prow(prm_ref, R_RNORMB)) * _silu(z_d))
    return jnp.concatenate([y_a, y_b, y_c, y_d], axis=-1)


def _prompt_kernel(layer, tile, valid, final,
                   h_ref, cos_ref, sin_ref, win_ref, wout_ref, prm_ref, exp_ref, s0_ref, ctx0_ref,
                   hout_ref, sout_ref, ctxout_ref,
                   proj_s, cbuf_g, cbuf_s, st_s, he_s,
                   hq_s, hk_s, hv_s, hg_s,
                   gq_s, gk_s, gv_s, gg_s, gb_s, gqk_s, gsv_s, gsk_s,
                   sx_s, sv_s, sb_s, sc_s, sg_s, sdt_s,
                   rq_s, rk_s, rv_s, rg_s, o_s, y_s, hn_s, hspan_s):
    t = pl.program_id(1)
    nt = pl.num_programs(1)
    n_chunks = tile // CHUNK
    C = CHUNK

    @pl.when(t == 0)
    def _():
        st_s[...] = s0_ref[...]
        cbuf_g[0:8, :] = ctx0_ref[0]
        cbuf_s[0:8, :] = ctx0_ref[1]

    hn_s[...] = (_rms(h_ref[0]) * _prow(prm_ref, R_NORM, D_MODEL)).astype(BF16)

    def project(*col_ranges):
        for lo, hi in col_ranges:
            proj_s[:, lo:hi] = jnp.dot(hn_s[...], win_ref[0, :, lo:hi], preferred_element_type=F32)
        yield

    row = _iota((tile, GROUP_W), 0)
    row_in_chunk = row & (C - 1)
    if valid < tile:
        live = row < valid
        keep = lambda a: jnp.where(live[:, 0:a.shape[1]], a, 0.0)
    else:
        keep = lambda a: a

    ri = _iota((GROUP_W, GROUP_W), 0)
    ci = _iota((GROUP_W, GROUP_W), 1)
    ones64 = jnp.where((ri >> 6) == (ci >> 6), 1.0, 0.0).astype(BF16)
    ones128 = jnp.where((ri >> 7) == (ci >> 7), 1.0, 0.0).astype(BF16)

    def conv_taps(cbuf, u, r_w):
        cbuf[8:8 + tile, :] = u
        cw = _prow(prm_ref, r_w, CONV_DIM, CONV_W)
        out = (cw[3:4] * u + cw[2:3] * cbuf[7:7 + tile, :] + cw[1:2] * cbuf[6:6 + tile, :]
               + cw[0:1] * cbuf[5:5 + tile, :])
        cbuf[0:8, :] = cbuf[tile:tile + 8, :]
        return out

    def gdn_prep():
        log_decay, vals = _compact_gates(proj_s[:, SMALL:SMALL + 128], prm_ref)
        cum = _chunk_cumsum(keep(log_decay), row_in_chunk[:, 0:128])
        cum = _dot_exact_rhs(cum, exp_ref[:, 0:512])
        gg_s[...] = cum[:, 0:256]
        sg_s[...] = cum[:, 256:512]
        vals = _dot_exact_rhs(keep(vals), exp_ref[:, 512:1024])
        gb_s[...] = vals[:, 0:256]
        sdt_s[...] = vals[:, 256:512]
        yield
        act = _silu(conv_taps(cbuf_g, proj_s[:, B_QKV:B_QKV + CONV_DIM], R_GCONV))
        gq_s[...] = _l2n(act[:, 0:256], ones64) * QK_SCALE
        gk_s[...] = _l2n(act[:, 256:512], ones64)
        gv_s[...] = act[:, 512:768]
        yield

    def hgrn_prep():
        lb = _hgrn_lower_bound(_prow(prm_ref, R_LB, n=DEPTH), layer)
        q, k, f, v = _hgrn_inputs(proj_s[...], lb)
        hq_s[...] = q
        hk_s[...] = k
        hv_s[...] = keep(v)
        yield
        G = _chunk_cumsum(keep(jnp.log(f)), row_in_chunk)
        hg_s[...] = G
        span = jnp.zeros((1, GROUP_W), F32)
        for c in range(n_chunks):
            mid = G[c * C + C // 2 - 1:c * C + C // 2, :]
            span = jnp.maximum(span, jnp.maximum(G[c * C:c * C + 1, :] - mid, mid - G[c * C + C - 1:c * C + C, :]))
        hspan_s[...] = jnp.broadcast_to(span, hspan_s.shape)
        yield

    def hgrn_level_exps():
        G = hg_s[...]
        last = G
        s = 1
        lvl = 5
        while s < C:
            upper = (row & s) != 0
            ref_row = jnp.where(upper, pltpu.roll(last, s, axis=0), last)
            he_s[lvl] = jnp.exp(jnp.minimum(jnp.where(upper, G - ref_row, ref_row - G), 0.0))
            last = jnp.where(upper, last, pltpu.roll(last, tile - s, axis=0))
            s *= 2
            lvl -= 1

    def ssd_prep():
        act = _silu(conv_taps(cbuf_s, proj_s[:, C_XBC:C_XBC + CONV_DIM], R_SCONV)
                    + _prow(prm_ref, R_SCONVB, CONV_DIM))
        dt = sdt_s[...]
        sx_s[...] = act[:, 0:256]
        sv_s[...] = act[:, 0:256] * dt
        sb_s[...] = act[:, 256:512]
        sc_s[...] = act[:, 512:768]
        yield

    def ret_prep():
        cos = cos_ref[...]
        sin = sin_ref[...]
        rq_s[...] = _rotary(proj_s[:, D_Q:D_Q + GROUP_W], cos, sin)
        rk_s[...] = _rotary(proj_s[:, D_K:D_K + GROUP_W], cos, sin) * QK_SCALE
        rv_s[...] = keep(proj_s[:, D_V:D_V + GROUP_W])
        live_rows = jnp.clip(jnp.minimum(row_in_chunk + 1, valid - (row - row_in_chunk)), 0, C)
        rg_s[...] = live_rows.astype(F32) * _prow(prm_ref, R_RLOGG)
        yield

    ii = _iota((C, GROUP_W), 0)
    jj = _iota((C, GROUP_W), 1) & (C - 1)
    tri = jj <= ii
    strict = jj < ii
    eye = jj == ii
    eye_f = jnp.where(eye, 1.0, 0.0)
    sq_mask = (ri >> 6) == (ci >> 6)
    grp_mask = (ri >> 7) == (ci >> 7)
    bd_mask = ones64
    grp_row_mask = ones128

    def decay_parts(G):
        g_row = jnp.sum(jnp.where(eye, G, 0.0), axis=0, keepdims=True)
        dm = jnp.where(tri, jnp.exp(jnp.minimum(G - g_row, 0.0)), 0.0)
        g_last = G[C - 1:C, :]
        return dm, g_last

    def hgrn_a(c):
        sl = pl.ds(c * C, C)
        G = hg_s[sl, :]
        mid = hg_s[pl.ds(c * C + C // 2 - 1, 1), :]
        qf = hq_s[sl, :] * jnp.exp(G - mid)
        kf = hk_s[sl, :] * jnp.exp(mid - G)
        scores = jnp.where(tri, _dot_nt(qf, _block_diag(kf, bd_mask)), 0.0)
        yield
        o_s[sl, 0:256] = _dot(scores, _block_diag(hv_s[sl, :], bd_mask))
        yield

    def hgrn_a_any_decay(c, carry):
        sl = pl.ds(pl.multiple_of(c * C, C), C)
        q = hq_s[sl, :]
        k = hk_s[sl, :]
        scores = jnp.where(eye, _dot_nt(q, _block_diag(k, bd_mask)), 0.0)
        s = C // 2
        lvl = 0
        while s >= 1:
            e = he_s[lvl, sl, :]
            upper = (ii & s) != 0
            qs = jnp.where(upper, q * e, 0.0)
            ks = jnp.where(upper, 0.0, k * e)
            sc = _dot_nt(qs, _block_diag(ks, bd_mask))
            same_block = (ii & -(2 * s)) == (jj & -(2 * s))
            scores = scores + jnp.where(same_block, sc, 0.0)
            s //= 2
            lvl += 1
        o_s[sl, 0:256] = _dot(scores, _block_diag(hv_s[sl, :], bd_mask))
        return carry

    def gdn_a(c):
        sl = pl.ds(c * C, C)
        q = gq_s[sl, :]
        k = gk_s[sl, :]
        G = gg_s[sl, :]
        beta = gb_s[sl, :]
        dm, _ = decay_parts(G)
        kq = _dot_nt(jnp.concatenate([k, q], axis=0), _block_diag(k, bd_mask))
        yield
        m = jnp.where(strict, kq[0:C] * dm * beta, 0.0)
        gqk_s[sl, :] = kq[C:2 * C] * dm
        p = _dot(m, _block_diag(m, bd_mask))
        yield
        tinv = eye_f - m
        n = 2
        while 2 * n < C:
            both = _dot(jnp.concatenate([p, tinv], axis=0), _block_diag(p, bd_mask))
            yield
            p = both[0:C]
            tinv = tinv + both[C:2 * C]
            n *= 2
        tinv = tinv + _dot(tinv, _block_diag(p, bd_mask))
        yield
        kb = k * beta * jnp.exp(G)
        rhs = jnp.concatenate([_block_diag(gv_s[sl, :] * beta, bd_mask), _block_diag(kb, bd_mask)], axis=1)
        sol = _dot(tinv, rhs)
        gsv_s[sl, :] = sol[:, 0:256]
        gsk_s[sl, :] = sol[:, 256:512]
        yield

    def ssd_a(c):
        sl = pl.ds(c * C, C)
        bm = sb_s[sl, :]
        dm, _ = decay_parts(sg_s[sl, :])
        cb = _dot_nt(sc_s[sl, :], _block_diag(bm, grp_row_mask))
        yield
        o_s[sl, 512:768] = _dot(cb * dm, _block_diag(sv_s[sl, :], bd_mask))
        yield

    def ret_a(c):
        sl = pl.ds(c * C, C)
        dm, _ = decay_parts(rg_s[sl, :])
        sc = _dot_nt(rq_s[sl, :], _block_diag(rk_s[sl, :], bd_mask)) * dm
        yield
        o_s[sl, 768:1024] = _dot(sc, _block_diag(rv_s[sl, :], bd_mask))
        yield

    def hgrn_b(c):
        sl = pl.ds(c * C, C)
        G = hg_s[sl, :]
        g_last = G[C - 1:C, :]
        st = st_s[0]
        o_s[sl, 0:256] += _dot_nt(hq_s[sl, :] * jnp.exp(G), st)
        yield
        kt = hk_s[sl, :] * jnp.exp(g_last - G)
        st_s[0] = st * jnp.exp(g_last) + jnp.where(sq_mask, _dot_tn(hv_s[sl, :], kt), 0.0)
        yield

    def gdn_b(c):
        sl = pl.ds(c * C, C)
        G = gg_s[sl, :]
        g_last = G[C - 1:C, :]
        st = st_s[1]
        tmp = _dot(jnp.concatenate([gsk_s[sl, :], gq_s[sl, :] * jnp.exp(G)], axis=0), st)
        yield
        u = gsv_s[sl, :] - tmp[0:C]
        kt = gk_s[sl, :] * jnp.exp(g_last - G)
        st_s[1] = st * jnp.exp(g_last) + jnp.where(sq_mask, _dot_tn(kt, u), 0.0)
        yield
        o_s[sl, 256:512] = tmp[C:2 * C] + _dot(gqk_s[sl, :], _block_diag(u, bd_mask))
        yield

    def ssd_b(c):
        sl = pl.ds(c * C, C)
        G = sg_s[sl, :]
        g_last = G[C - 1:C, :]
        st = st_s[2]
        o_s[sl, 512:768] += jnp.exp(G) * _dot(sc_s[sl, :], st)
        yield
        vt = sv_s[sl, :] * jnp.exp(g_last - G)
        st_s[2] = st * jnp.exp(g_last) + jnp.where(grp_mask, _dot_tn(sb_s[sl, :], vt), 0.0)
        yield

    def ret_b(c):
        sl = pl.ds(c * C, C)
        G = rg_s[sl, :]
        g_last = G[C - 1:C, :]
        st = st_s[3]
        o_s[sl, 768:1024] += jnp.exp(G) * _dot(rq_s[sl, :], st)
        yield
        vt = rv_s[sl, :] * jnp.exp(g_last - G)
        st_s[3] = st * jnp.exp(g_last) + jnp.where(sq_mask, _dot_tn(rk_s[sl, :], vt), 0.0)
        yield

    def post(c):
        sl = pl.ds(c * C, C)
        y_s[sl, :] = _post_mix(o_s[sl, 0:256], o_s[sl, 256:512], o_s[sl, 512:768], o_s[sl, 768:1024],
                               proj_s[sl, A_Z:A_Z + GROUP_W], proj_s[sl, B_Z:B_Z + GROUP_W],
                               proj_s[sl, C_Z:C_Z + GROUP_W], proj_s[sl, D_Z:D_Z + GROUP_W],
                               sx_s[sl, :], prm_ref, ones64, ones128).astype(BF16)
        yield

    chunks = range(n_chunks)
    every = lambda *gens: itertools.chain.from_iterable(gens)
    z_cols = [(z, z + GROUP_W) for z in (A_Z, B_Z, C_Z, D_Z)]
    _interleave([every(project((B_QKV, B_Z), (SMALL, NP)), gdn_prep())], project((A_Q, A_Z)), 1)
    _interleave([gdn_a(c) for c in chunks],
                every(hgrn_prep(), project((C_XBC, C_Z)), *[hgrn_a(c) for c in chunks],
                      ssd_prep(), project((D_Q, D_Z)), *[ssd_a(c) for c in chunks],
                      ret_prep(), project(*z_cols), *[ret_a(c) for c in chunks]),
                FILL_PER_ROUND)

    @pl.when(jnp.max(hspan_s[0:1, :]) > HGRN_SAFE_SPAN)
    def _():
        hgrn_level_exps()
        lax.fori_loop(0, n_chunks, hgrn_a_any_decay, 0)

    for c in chunks:
        done = [post(c - 1)] if c else []
        _interleave([gdn_b(c)], every(hgrn_b(c), ssd_b(c), ret_b(c), *done), 2)
    _interleave([post(n_chunks - 1)], (), 0)

    out = h_ref[0] + jnp.dot(y_s[...], wout_ref[0], preferred_element_type=F32)
    if final:
        out = _rms(out) * _prow(prm_ref, R_FINAL, D_MODEL)
    hout_ref[0] = out

    @pl.when(t == nt - 1)
    def _():
        sout_ref[0] = st_s[...]
        ctxout_ref[0, 0] = cbuf_g[valid:valid + 8, :]
        ctxout_ref[0, 1] = cbuf_s[valid:valid + 8, :]


def _prompt_layer(layer, h, cos, sin, win, wout, prm, expand, s0, ctx0, *, tile, valid, final):
    bsz, seq, _ = h.shape
    nt = seq // tile
    kern = functools.partial(_prompt_kernel, layer, tile, valid, final)
    const2 = lambda b, t: (0, 0)
    slab = lambda: pltpu.VMEM((tile, GROUP_W), F32)
    return pl.pallas_call(
        kern,
        grid=(bsz, nt),
        in_specs=[
            pl.BlockSpec((1, tile, D_MODEL), lambda b, t: (b, t, 0)),
            pl.BlockSpec((tile, GROUP_W), lambda b, t: (t, 0)),
            pl.BlockSpec((tile, GROUP_W), lambda b, t: (t, 0)),
            pl.BlockSpec((1, D_MODEL, NP), lambda b, t: (layer, 0, 0)),
            pl.BlockSpec((1, D_MODEL, D_MODEL), lambda b, t: (layer, 0, 0)),
            pl.BlockSpec((1, N_PRM, D_MODEL), lambda b, t: (layer, 0, 0)),
            pl.BlockSpec(expand.shape, const2),
            pl.BlockSpec((4, GROUP_W, GROUP_W), lambda b, t: (0, 0, 0)),
            pl.BlockSpec((2, 8, CONV_DIM), lambda b, t: (0, 0, 0)),
        ],
        out_specs=[
            pl.BlockSpec((1, tile, D_MODEL), lambda b, t: (b, t, 0)),
            pl.BlockSpec((1, 4, GROUP_W, GROUP_W), lambda b, t: (b, 0, 0, 0)),
            pl.BlockSpec((1, 2, 8, CONV_DIM), lambda b, t: (b, 0, 0, 0)),
        ],
        out_shape=[
            jax.ShapeDtypeStruct((bsz, seq, D_MODEL), F32),
            jax.ShapeDtypeStruct((bsz, 4, GROUP_W, GROUP_W), F32),
            jax.ShapeDtypeStruct((bsz, 2, 8, CONV_DIM), F32),
        ],
        scratch_shapes=[
            pltpu.VMEM((tile, NP), F32),
            pltpu.VMEM((tile + 8, CONV_DIM), F32),
            pltpu.VMEM((tile + 8, CONV_DIM), F32),
            pltpu.VMEM((4, GROUP_W, GROUP_W), F32),
            pltpu.VMEM((6, tile, GROUP_W), F32),
        ] + [slab() for _ in range(22)] + [pltpu.VMEM((tile, D_MODEL), F32), pltpu.VMEM((tile, D_MODEL), BF16),
                                             pltpu.VMEM((tile, D_MODEL), BF16), pltpu.VMEM((8, GROUP_W), F32)],
        compiler_params=pltpu.CompilerParams(
            dimension_semantics=("arbitrary", "arbitrary"), vmem_limit_bytes=VMEM_LIMIT),
        name=f"prompt_layer_t{tile}",
    )(h, cos, sin, win, wout, prm, expand, s0, ctx0)


(VT_HQ, VT_HK, VT_HF, VT_HV, VT_GQ, VT_GK, VT_GV, VT_GDEC, VT_GBETA,
 VT_SV, VT_SB, VT_SC, VT_SDEC, VT_RQ, VT_RK, VT_RV, VT_RDEC) = [GROUP_W * i for i in range(17)]
N_VT = 17 * GROUP_W
DEC_VBLOCK = 32


def _dec_pre_kernel(layer, hs_ref, cos_ref, sin_ref, win_ref, prm_ref, exp_ref, gctx_ref, sctx_ref,
                    vt_ref, post_ref, ug_ref, us_ref):
    x = hs_ref[...]
    nb = x.shape[0]
    hn = _rms(x) * _prow(prm_ref, R_NORM, D_MODEL)
    proj = jnp.dot(hn.astype(BF16), win_ref[0], preferred_element_type=F32)

    ri = _iota((GROUP_W, GROUP_W), 0)
    ci = _iota((GROUP_W, GROUP_W), 1)
    ones64 = jnp.where((ri >> 6) == (ci >> 6), 1.0, 0.0).astype(BF16)

    def put(off, a):
        vt_ref[off:off + GROUP_W, :] = jnp.transpose(a)

    lb = _hgrn_lower_bound(_prow(prm_ref, R_LB, n=DEPTH), layer)
    q, k, f, v = _hgrn_inputs(proj, lb)
    put(VT_HQ, q)
    put(VT_HK, k)
    put(VT_HF, f)
    put(VT_HV, v)

    log_decay, vals = _compact_gates(proj[:, SMALL:SMALL + 128], prm_ref)
    decay = jnp.exp(_dot_exact_rhs(log_decay, exp_ref[:, 0:512]))
    vals = _dot_exact_rhs(vals, exp_ref[:, 512:1024])

    u = proj[:, B_QKV:B_QKV + CONV_DIM]
    ug_ref[...] = u
    cw = _prow(prm_ref, R_GCONV, CONV_DIM, CONV_W)
    conv = cw[3:4] * u + cw[2:3] * gctx_ref[0, 2] + cw[1:2] * gctx_ref[0, 1] + cw[0:1] * gctx_ref[0, 0]
    act = _silu(conv)
    put(VT_GQ, _l2n(act[:, 0:256], ones64) * QK_SCALE)
    put(VT_GK, _l2n(act[:, 256:512], ones64))
    put(VT_GV, act[:, 512:768])
    put(VT_GDEC, decay[:, 0:256])
    put(VT_GBETA, vals[:, 0:256])

    u = proj[:, C_XBC:C_XBC + CONV_DIM]
    us_ref[...] = u
    cw = _prow(prm_ref, R_SCONV, CONV_DIM, CONV_W)
    conv = (cw[3:4] * u + cw[2:3] * sctx_ref[0, 2] + cw[1:2] * sctx_ref[0, 1] + cw[0:1] * sctx_ref[0, 0]
            + _prow(prm_ref, R_SCONVB, CONV_DIM))
    act = _silu(conv)
    post_ref[:, POST_SX:POST_SX + 256] = act[:, 0:256]
    put(VT_SV, act[:, 0:256] * vals[:, 256:512])
    put(VT_SB, act[:, 256:512])
    put(VT_SC, act[:, 512:768])
    put(VT_SDEC, decay[:, 256:512])

    cos = cos_ref[...]
    sin = sin_ref[...]
    put(VT_RQ, _rotary(proj[:, D_Q:D_Q + GROUP_W], cos, sin))
    put(VT_RK, _rotary(proj[:, D_K:D_K + GROUP_W], cos, sin) * QK_SCALE)
    put(VT_RV, proj[:, D_V:D_V + GROUP_W])
    put(VT_RDEC, jnp.broadcast_to(jnp.exp(_prow(prm_ref, R_RLOGG)), (nb, GROUP_W)))

    post_ref[:, POST_AZ:POST_AZ + 256] = proj[:, A_Z:A_Z + GROUP_W]
    post_ref[:, POST_BZ:POST_BZ + 256] = proj[:, B_Z:B_Z + GROUP_W]
    post_ref[:, POST_CZ:POST_CZ + 256] = proj[:, C_Z:C_Z + GROUP_W]
    post_ref[:, POST_DZ:POST_DZ + 256] = proj[:, D_Z:D_Z + GROUP_W]


def _dec_pre(layer, hs, cos, sin, win, prm, expand, gctx, sctx):
    nb = hs.shape[0]
    full = lambda a: pl.BlockSpec(a.shape, lambda i: (0,) * a.ndim)
    shapes = [(N_VT, nb), (nb, N_POST), (nb, CONV_DIM), (nb, CONV_DIM)]
    return pl.pallas_call(
        functools.partial(_dec_pre_kernel, layer),
        grid=(1,),
        in_specs=[
            full(hs), full(cos), full(sin),
            pl.BlockSpec((1, D_MODEL, NP), lambda i: (layer, 0, 0)),
            pl.BlockSpec((1, N_PRM, D_MODEL), lambda i: (layer, 0, 0)),
            full(expand),
            pl.BlockSpec((1, 3, nb, CONV_DIM), lambda i: (layer, 0, 0, 0)),
            pl.BlockSpec((1, 3, nb, CONV_DIM), lambda i: (layer, 0, 0, 0)),
        ],
        out_specs=[pl.BlockSpec(s, lambda i: (0, 0)) for s in shapes],
        out_shape=[jax.ShapeDtypeStruct(s, F32) for s in shapes],
        compiler_params=pltpu.CompilerParams(dimension_semantics=("arbitrary",), vmem_limit_bytes=VMEM_LIMIT),
        name="decode_pre",
    )(hs, cos, sin, win, prm, expand, gctx, sctx)


def _dec_state_kernel(n_alias, vt_ref, hg_ref, gd_ref, sd_ref, rt_ref, *refs):
    o_ref, hg_out, gd_out, sd_out, rt_out = refs[n_alias:]
    h = pl.program_id(0)
    j = pl.program_id(1)
    vb = DEC_VBLOCK
    nb = vt_ref.shape[1]
    head = h * HEAD_DIM

    def row(off, i):
        return vt_ref[pl.ds(off + i, 1), :]

    def slab(off):
        return vt_ref[pl.ds(pl.multiple_of(off + head + j * vb, 8), vb), :]

    zero = jnp.zeros((vb, nb), F32)

    v = slab(VT_HV)

    def hgrn_step(k, o):
        s = row(VT_HF + head, k) * hg_ref[0, 0, k] + row(VT_HK + head, k) * v
        hg_out[0, 0, k] = s
        return o + row(VT_HQ + head, k) * s

    o_ref[0, 0] = lax.fori_loop(0, HEAD_DIM, hgrn_step, zero, unroll=4)

    v = slab(VT_GV)
    dec = row(VT_GDEC + head, 0)
    ks = lax.fori_loop(0, HEAD_DIM, lambda k, a: a + row(VT_GK + head, k) * gd_ref[0, 0, k], zero, unroll=4)
    u = row(VT_GBETA + head, 0) * (v - ks * dec)

    def gdn_step(k, o):
        s = gd_ref[0, 0, k] * dec + row(VT_GK + head, k) * u
        gd_out[0, 0, k] = s
        return o + row(VT_GQ + head, k) * s

    o_ref[1, 0] = lax.fori_loop(0, HEAD_DIM, gdn_step, zero, unroll=4)

    v = slab(VT_SV)
    dec = row(VT_SDEC + head, 0)
    grp = (h >> 1) * 128

    def ssd_step(n, o):
        s = sd_ref[0, 0, n] * dec + row(VT_SB + grp, n) * v
        sd_out[0, 0, n] = s
        return o + row(VT_SC + grp, n) * s

    o_ref[2, 0] = lax.fori_loop(0, 128, ssd_step, zero, unroll=4)

    v = slab(VT_RV)
    dec = row(VT_RDEC + head, 0)

    def ret_step(k, o):
        s = rt_ref[0, 0, k] * dec + row(VT_RK + head, k) * v
        rt_out[0, 0, k] = s
        return o + row(VT_RQ + head, k) * s

    o_ref[3, 0] = lax.fori_loop(0, HEAD_DIM, ret_step, zero, unroll=4)


def _dec_state(layer, vt, states, accs):
    nb = vt.shape[1]
    vb = DEC_VBLOCK
    depth = states[0].shape[0]
    blk = lambda a: pl.BlockSpec((1, 1, a.shape[2], vb, nb), lambda h, j: (layer, h, 0, j, 0))
    n_alias = 0 if accs is None else 4
    in_specs = [pl.BlockSpec(vt.shape, lambda h, j: (0, 0))] + [blk(a) for a in states]
    operands = [vt] + list(states)
    if accs is not None:
        in_specs += [pl.BlockSpec(memory_space=pl.ANY)] * 4
        operands += list(accs)
    return pl.pallas_call(
        functools.partial(_dec_state_kernel, n_alias),
        grid=(N_HEADS, HEAD_DIM // vb),
        in_specs=in_specs,
        out_specs=[pl.BlockSpec((4, 1, vb, nb), lambda h, j: (0, h, j, 0))] + [blk(a) for a in states],
        out_shape=[jax.ShapeDtypeStruct((4, N_HEADS, HEAD_DIM, nb), F32)]
        + [jax.ShapeDtypeStruct(a.shape, F32) for a in states],
        input_output_aliases={5 + i: 1 + i for i in range(n_alias)},
        compiler_params=pltpu.CompilerParams(dimension_semantics=("arbitrary", "arbitrary"),
                                             vmem_limit_bytes=VMEM_LIMIT),
        name="decode_state",
    )(*operands)


def _dec_post_kernel(final, hs_ref, ot_ref, post_ref, wout_ref, prm_ref, out_ref):
    ri = _iota((GROUP_W, GROUP_W), 0)
    ci = _iota((GROUP_W, GROUP_W), 1)
    ones64 = jnp.where((ri >> 6) == (ci >> 6), 1.0, 0.0).astype(BF16)
    ones128 = jnp.where((ri >> 7) == (ci >> 7), 1.0, 0.0).astype(BF16)
    gate = lambda off: post_ref[:, off:off + GROUP_W]
    o = [jnp.transpose(ot_ref[m * GROUP_W:(m + 1) * GROUP_W, :]) for m in range(4)]
    y = _post_mix(o[0], o[1], o[2], o[3],
                  gate(POST_AZ), gate(POST_BZ), gate(POST_CZ), gate(POST_DZ), gate(POST_SX),
                  prm_ref, ones64, ones128)
    out = hs_ref[...] + jnp.dot(y.astype(BF16), wout_ref[0], preferred_element_type=F32)
    if final:
        out = _rms(out) * _prow(prm_ref, R_FINAL, D_MODEL)
    out_ref[...] = out


def _dec_post(layer, hs, ot, post, wout, prm, final):
    nb = hs.shape[0]
    full = lambda a: pl.BlockSpec(a.shape, lambda i: (0,) * a.ndim)
    return pl.pallas_call(
        functools.partial(_dec_post_kernel, final),
        grid=(1,),
        in_specs=[full(hs), full(ot), full(post),
                  pl.BlockSpec((1, D_MODEL, D_MODEL), lambda i: (layer, 0, 0)),
                  pl.BlockSpec((1, N_PRM, D_MODEL), lambda i: (layer, 0, 0))],
        out_specs=pl.BlockSpec((nb, D_MODEL), lambda i: (0, 0)),
        out_shape=jax.ShapeDtypeStruct((nb, D_MODEL), F32),
        compiler_params=pltpu.CompilerParams(dimension_semantics=("arbitrary",), vmem_limit_bytes=VMEM_LIMIT),
        name="decode_post",
    )(hs, ot, post, wout, prm)


POST_AZ, POST_BZ, POST_CZ, POST_DZ, POST_SX = [GROUP_W * i for i in range(5)]
N_POST = 5 * GROUP_W


def _expand_matrix():
    e = np.zeros((128, 4 * GROUP_W), np.float32)
    for block, first_lane in enumerate((0, 2 * N_HEADS, N_HEADS, 2 * N_HEADS)):
        for h in range(N_HEADS):
            lo = block * GROUP_W + h * HEAD_DIM
            e[first_lane + h, lo:lo + HEAD_DIM] = 1.0
    return jnp.asarray(e, BF16)


def _relayout_w_in(w_in):
    sizes = (256, 256, 256, 256, 768, 256, 4, 4, 768, 256, 4, 256, 256, 256, 256)
    offs = np.concatenate([[0], np.cumsum(sizes)])
    seg = lambda i: w_in[:, :, offs[i]:offs[i + 1]]
    pad = jnp.zeros(w_in.shape[:2] + (128 - 12,), w_in.dtype)
    cols = [seg(0), seg(1), seg(2), seg(3), seg(4), seg(5), seg(8), seg(9), seg(11), seg(12), seg(13), seg(14),
            seg(6), seg(7), seg(10), pad]
    return jnp.concatenate(cols, axis=-1).astype(BF16)


def _pack_params(norm_w, hgrn_lb_logits, hgrn_norm_w, gdn_conv_w, gdn_a_log, gdn_dt_bias, gdn_norm_w,
                 ssd_conv_w, ssd_conv_b, ssd_a_log, ssd_dt_bias, ssd_d, ssd_norm_w, ret_norm_w, ret_norm_b,
                 final_norm_w):
    depth = norm_w.shape[0]
    prm = jnp.zeros((depth, N_PRM, D_MODEL), F32)

    def put(p, r, a):
        a = a.astype(F32)
        if a.ndim == 2:
            a = a[:, None, :]
        return p.at[:, r:r + a.shape[1], 0:a.shape[2]].set(a)

    rep = lambda a: jnp.repeat(a, HEAD_DIM, axis=-1)
    prm = put(prm, R_NORM, norm_w)
    prm = put(prm, R_LB, jnp.broadcast_to(hgrn_lb_logits[None], (depth,) + hgrn_lb_logits.shape))
    prm = put(prm, R_HNORM, hgrn_norm_w)
    prm = put(prm, R_GCONV, gdn_conv_w)
    prm = put(prm, R_GNORM, gdn_norm_w)
    prm = put(prm, R_SCONV, ssd_conv_w)
    prm = put(prm, R_SCONVB, ssd_conv_b)
    prm = put(prm, R_SD, rep(ssd_d))
    gap = jnp.zeros((depth, N_HEADS), F32)
    prm = put(prm, R_CBIAS, jnp.concatenate([gdn_dt_bias.astype(F32), gap, ssd_dt_bias.astype(F32)], axis=-1))
    prm = put(prm, R_CALOG, jnp.concatenate([gdn_a_log.astype(F32), gap, ssd_a_log.astype(F32)], axis=-1))
    prm = put(prm, R_SNORM, ssd_norm_w)
    prm = put(prm, R_RNORM, ret_norm_w)
    prm = put(prm, R_RNORMB, ret_norm_b)
    ret_logg = jnp.log1p(-jnp.exp2(-5.0 - jnp.arange(N_HEADS, dtype=F32)))
    prm = put(prm, R_RLOGG, jnp.broadcast_to(rep(ret_logg)[None], (depth, GROUP_W)))
    prm = put(prm, R_FINAL, jnp.broadcast_to(final_norm_w[None], (depth, D_MODEL)))
    return prm


def _rope_tables(pos):
    half = HEAD_DIM // 2
    inv_freq = 1.0 / (ROPE_BASE ** jnp.linspace(0.0, 1.0, half, dtype=F32))
    ang = pos[:, None] * inv_freq[None, :]
    cos, sin = jnp.cos(ang), jnp.sin(ang)
    cos_h = jnp.concatenate([cos, cos], axis=-1)
    sin_h = jnp.concatenate([-sin, sin], axis=-1)
    return jnp.tile(cos_h, (1, N_HEADS)), jnp.tile(sin_h, (1, N_HEADS))


def _diag_blocks(s, rows, cols):
    return jnp.stack([s[:, (h * rows):(h + 1) * rows, h * cols:(h + 1) * cols] for h in range(N_HEADS)], axis=1)


def _unpack_states(sout):
    hgrn = jnp.swapaxes(_diag_blocks(sout[:, 0], HEAD_DIM, HEAD_DIM), -1, -2)
    gdn = _diag_blocks(sout[:, 1], HEAD_DIM, HEAD_DIM)
    ssd = jnp.stack([sout[:, 2, (h // 2) * 128:(h // 2 + 1) * 128, h * HEAD_DIM:(h + 1) * HEAD_DIM]
                     for h in range(N_HEADS)], axis=1)
    ret = _diag_blocks(sout[:, 3], HEAD_DIM, HEAD_DIM)
    return hgrn, gdn, ssd, ret


def kernel(x_prompt, x_sample, state_hgrn, state_gdn, state_gdn_conv, state_ssd, state_ssd_conv, state_ret,
           meta_tokens, norm_w, w_in, hgrn_lb_logits, hgrn_norm_w, gdn_conv_w, gdn_a_log, gdn_dt_bias, gdn_norm_w,
           ssd_conv_w, ssd_conv_b, ssd_a_log, ssd_dt_bias, ssd_d, ssd_norm_w, ret_norm_w, ret_norm_b,
           w_out, final_norm_w):
    depth = w_in.shape[0]
    bsz, seq, _ = x_prompt.shape
    nb = x_sample.shape[0]
    tile = PROMPT_TILE if seq % PROMPT_TILE == 0 else CHUNK

    win = _relayout_w_in(w_in)
    wout = w_out.astype(BF16)
    prm = _pack_params(norm_w, hgrn_lb_logits, hgrn_norm_w, gdn_conv_w, gdn_a_log, gdn_dt_bias, gdn_norm_w,
                       ssd_conv_w, ssd_conv_b, ssd_a_log, ssd_dt_bias, ssd_d, ssd_norm_w, ret_norm_w, ret_norm_b,
                       final_norm_w)
    expand = _expand_matrix()

    cos_m, sin_m = _rope_tables(jnp.arange(CHUNK, dtype=F32))
    cos_p, sin_p = _rope_tables(N_META + jnp.arange(seq, dtype=F32))
    cos_s, sin_s = _rope_tables(PAST_LEN + jnp.arange(1, dtype=F32))

    hm = jnp.zeros((1, CHUNK, D_MODEL), F32).at[0, :N_META].set(meta_tokens.astype(F32))
    hp = x_prompt.astype(F32)
    hs = x_sample.astype(F32)[:, 0, :]
    zero_s = jnp.zeros((4, GROUP_W, GROUP_W), F32)
    zero_ctx = jnp.zeros((2, 8, CONV_DIM), F32)

    gctx = jnp.swapaxes(state_gdn_conv.astype(F32), 1, 2)
    sctx = jnp.swapaxes(state_ssd_conv.astype(F32), 1, 2)

    seq_minor = lambda a: jnp.transpose(a.astype(F32), (0, 2, 3, 4, 1))
    dec_states = [seq_minor(a) for a in (state_hgrn, state_gdn, state_ssd, state_ret)]
    dec_new = None

    p_states, p_ctx = [], []
    s_gctx, s_sctx = [], []
    for l in range(depth):
        final = l == depth - 1
        hm, sm, cm = _prompt_layer(l, hm, cos_m, sin_m, win, wout, prm, expand, zero_s, zero_ctx,
                                   tile=CHUNK, valid=N_META, final=False)
        hp, sp, cp = _prompt_layer(l, hp, cos_p, sin_p, win, wout, prm, expand, sm[0], cm[0],
                                   tile=tile, valid=tile, final=final)
        p_states.append(sp)
        p_ctx.append(cp)

        vt, post, ug, us = _dec_pre(l, hs, cos_s, sin_s, win, prm, expand, gctx, sctx)
        ot, *dec_new = _dec_state(l, vt, dec_states, dec_new)
        hs = _dec_post(l, hs, ot.reshape(4 * GROUP_W, nb), post, wout, prm, final)
        s_gctx.append(jnp.concatenate([state_gdn_conv[l, :, 1:].astype(F32), ug[:, None, :]], axis=1))
        s_sctx.append(jnp.concatenate([state_ssd_conv[l, :, 1:].astype(F32), us[:, None, :]], axis=1))

    sp = jnp.stack(p_states)
    cp = jnp.stack(p_ctx)
    ph, pg, ps, pr = _unpack_states(sp.reshape((depth * bsz,) + sp.shape[2:]))
    unb = lambda a: a.reshape((depth, bsz) + a.shape[1:])
    seq_major = lambda a: jnp.transpose(a, (0, 4, 1, 2, 3))
    return (hp.astype(x_prompt.dtype), hs[:, None, :].astype(x_sample.dtype),
            unb(ph).astype(state_hgrn.dtype), unb(pg).astype(state_gdn.dtype),
            cp[:, :, 0, 8 - (CONV_W - 1):].astype(state_gdn_conv.dtype),
            unb(ps).astype(state_ssd.dtype),
            cp[:, :, 1, 8 - (CONV_W - 1):].astype(state_ssd_conv.dtype),
            unb(pr).astype(state_ret.dtype),
            seq_major(dec_new[0]).astype(state_hgrn.dtype), seq_major(dec_new[1]).astype(state_gdn.dtype),
            jnp.stack(s_gctx).astype(state_gdn_conv.dtype), seq_major(dec_new[2]).astype(state_ssd.dtype),
            jnp.stack(s_sctx).astype(state_ssd_conv.dtype), seq_major(dec_new[3]).astype(state_ret.dtype))
```

```python
import functools
import itertools
import math

import numpy as np
import jax
import jax.numpy as jnp
from jax import lax
from jax.experimental import pallas as pl
from jax.experimental.pallas import tpu as pltpu

F32 = jnp.float32
BF16 = jnp.bfloat16

D_MODEL = 1024
GROUP_W = 256
HEAD_DIM = 64
N_HEADS = 4
CHUNK = 64
CONV_W = 4
CONV_DIM = 768
N_META = 16
PAST_LEN = 16384
ROPE_BASE = 10000.0
EPS = 1e-6
TINY = 1e-30
QK_SCALE = HEAD_DIM ** -0.5
DEPTH = 4

A_Q, A_F, A_I, A_Z = 0, 256, 512, 768
B_QKV, B_Z = 1024, 1792
C_XBC, C_Z = 2048, 2816
D_Q, D_K, D_V, D_Z = 3072, 3328, 3584, 3840
SMALL = 4096
NP = 4224

R_NORM, R_LB, R_HNORM = 0, 1, 5
R_GCONV, R_GNORM = 6, 12
R_SCONV, R_SCONVB, R_SD, R_SNORM = 13, 17, 20, 21
R_RNORM, R_RNORMB, R_RLOGG, R_FINAL = 22, 23, 24, 25
R_CBIAS, R_CALOG = 26, 27
N_PRM = 32

VMEM_LIMIT = 56 * 1024 * 1024
PROMPT_TILE = 256
DEC_BLOCK = 8
HGRN_SAFE_SPAN = 80.0
FILL_PER_ROUND = 4
_DONE = object()


def _interleave(leads, fillers, fill_per_round):
    leads = list(leads)
    fillers = iter(fillers)
    fill_live = fill_per_round > 0
    while leads or fill_live:
        leads = [g for g in leads if next(g, _DONE) is not _DONE]
        for _ in range(fill_per_round):
            if fill_live and next(fillers, _DONE) is _DONE:
                fill_live = False


def _dot(a, b):
    return jnp.dot(a.astype(BF16), b.astype(BF16), preferred_element_type=F32)


def _dot_nt(a, b):
    return lax.dot_general(a.astype(BF16), b.astype(BF16), (((1,), (1,)), ((), ())),
                           preferred_element_type=F32)


def _dot_tn(a, b):
    return lax.dot_general(a.astype(BF16), b.astype(BF16), (((0,), (0,)), ((), ())),
                           preferred_element_type=F32)


def _split3(x):
    hi = x.astype(BF16)
    r1 = x - hi.astype(F32)
    mid = r1.astype(BF16)
    lo = (r1 - mid.astype(F32)).astype(BF16)
    return hi, mid, lo


def _dot_exact_rhs(x, m):
    hi, mid, lo = _split3(x)
    return (jnp.dot(hi, m, preferred_element_type=F32) + jnp.dot(mid, m, preferred_element_type=F32)
            + jnp.dot(lo, m, preferred_element_type=F32))


def _dot_exact_lhs(m, x):
    hi, mid, lo = _split3(x)
    return (jnp.dot(m, hi, preferred_element_type=F32) + jnp.dot(m, mid, preferred_element_type=F32)
            + jnp.dot(m, lo, preferred_element_type=F32))


def _sigmoid(x):
    return 1.0 / (1.0 + jnp.exp(-x))


def _silu(x):
    return x * _sigmoid(x)


def _softplus(x):
    return jnp.maximum(x, 0.0) + jnp.log(1.0 + jnp.exp(-jnp.abs(x)))


def _iota(shape, dim):
    return lax.broadcasted_iota(jnp.int32, shape, dim)


def _block_diag(x, mask01):
    xb = x.astype(BF16)
    return jnp.concatenate([xb] * N_HEADS, axis=0) * mask01


def _chunk_cumsum(x, row_in_chunk):
    sh = 1
    while sh < CHUNK:
        r = pltpu.roll(x, sh, axis=0)
        x = x + jnp.where(row_in_chunk >= sh, r, 0.0)
        sh *= 2
    return x


def _swap_halves(x):
    lane = _iota(x.shape, 1)
    fwd = pltpu.roll(x, GROUP_W - HEAD_DIM // 2, axis=1)
    bwd = pltpu.roll(x, HEAD_DIM // 2, axis=1)
    return jnp.where((lane & (HEAD_DIM - 1)) < HEAD_DIM // 2, fwd, bwd)


def _group_sum(x, ones_bd):
    hi = x.astype(BF16)
    lo = (x - hi.astype(F32)).astype(BF16)
    return jnp.dot(hi, ones_bd, preferred_element_type=F32) + jnp.dot(lo, ones_bd, preferred_element_type=F32)


def _rms(x):
    return x * lax.rsqrt(jnp.mean(x * x, axis=-1, keepdims=True) + EPS)


def _hgrn_lower_bound(logits, layer):
    m = jnp.max(logits, axis=0, keepdims=True)
    e = jnp.exp(logits - m)
    w = e / jnp.sum(e, axis=0, keepdims=True)
    cum = jnp.sum(w[0:layer + 1], axis=0, keepdims=True) - w[0:1]
    return jnp.maximum(cum, 0.0)


def _prow(prm_ref, r, width=GROUP_W, n=1):
    return prm_ref[0, r:r + n, 0:width]


def _hgrn_inputs(proj, lb):
    aq = proj[:, A_Q:A_Q + GROUP_W]
    af = proj[:, A_F:A_F + GROUP_W]
    q = _silu(aq) * QK_SCALE
    k = (1.0 - lb) * _sigmoid(-af)
    f = jnp.maximum(lb + (1.0 - lb) * _sigmoid(af), TINY)
    return q, k, f, proj[:, A_I:A_I + GROUP_W]


def _compact_gates(c, prm_ref):
    sp = _softplus(c + _prow(prm_ref, R_CBIAS, 128))
    log_decay = -jnp.exp(_prow(prm_ref, R_CALOG, 128)) * sp
    lane = _iota(c.shape, 1)
    vals = jnp.where((lane >= N_HEADS) & (lane < 2 * N_HEADS), _sigmoid(c), sp)
    return log_decay, vals


def _l2n(x, ones64):
    return x * lax.rsqrt(_group_sum(x * x, ones64) + EPS)


def _rotary(x, cos, sin):
    return x * cos + _swap_halves(x) * sin


def _post_mix(o_a, o_b, o_c, o_d, z_a, z_b, z_c, z_d, x_ssd, prm_ref, ones64, ones128):
    inv64 = 1.0 / HEAD_DIM
    y_a = o_a * lax.rsqrt(_group_sum(o_a * o_a, ones64) * inv64 + EPS) * _prow(prm_ref, R_HNORM) * _silu(z_a)
    y_b = o_b * lax.rsqrt(_group_sum(o_b * o_b, ones64) * inv64 + EPS) * _prow(prm_ref, R_GNORM) * _silu(z_b)
    t_c = (o_c + _prow(prm_ref, R_SD) * x_ssd) * _silu(z_c)
    y_c = t_c * lax.rsqrt(_group_sum(t_c * t_c, ones128) * (1.0 / 128.0) + EPS) * _prow(prm_ref, R_SNORM)
    mu = _group_sum(o_d, ones64) * inv64
    xc = o_d - mu
    y_d = ((xc * lax.rsqrt(_group_sum(xc * xc, ones64) * inv64 + EPS) * _prow(prm_ref, R_RNORM)
            + _prow(prm_ref, R_RNORMB)) * _silu(z_d))
    return jnp.concatenate([y_a, y_b, y_c, y_d], axis=-1)


def _prompt_kernel(layer, tile, valid, final,
                   h_ref, cos_ref, sin_ref, win_ref, wout_ref, prm_ref, exp_ref, s0_ref, ctx0_ref,
                   hout_ref, sout_ref, ctxout_ref,
                   proj_s, cbuf_g, cbuf_s, st_s, he_s,
                   hq_s, hk_s, hv_s, hg_s,
                   gq_s, gk_s, gv_s, gg_s, gb_s, gqk_s, gsv_s, gsk_s,
                   sx_s, sv_s, sb_s, sc_s, sg_s, sdt_s,
                   rq_s, rk_s, rv_s, rg_s, o_s, y_s, hn_s, hspan_s):
    t = pl.program_id(1)
    nt = pl.num_programs(1)
    n_chunks = tile // CHUNK
    C = CHUNK

    @pl.when(t == 0)
    def _():
        st_s[...] = s0_ref[...]
        cbuf_g[0:8, :] = ctx0_ref[0]
        cbuf_s[0:8, :] = ctx0_ref[1]

    hn_s[...] = (_rms(h_ref[0]) * _prow(prm_ref, R_NORM, D_MODEL)).astype(BF16)

    def project(*col_ranges):
        for lo, hi in col_ranges:
            proj_s[:, lo:hi] = jnp.dot(hn_s[...], win_ref[0, :, lo:hi], preferred_element_type=F32)
        yield

    row = _iota((tile, GROUP_W), 0)
    row_in_chunk = row & (C - 1)
    if valid < tile:
        live = row < valid
        keep = lambda a: jnp.where(live[:, 0:a.shape[1]], a, 0.0)
    else:
        keep = lambda a: a

    ri = _iota((GROUP_W, GROUP_W), 0)
    ci = _iota((GROUP_W, GROUP_W), 1)
    ones64 = jnp.where((ri >> 6) == (ci >> 6), 1.0, 0.0).astype(BF16)
    ones128 = jnp.where((ri >> 7) == (ci >> 7), 1.0, 0.0).astype(BF16)

    def conv_taps(cbuf, u, r_w):
        cbuf[8:8 + tile, :] = u
        cw = _prow(prm_ref, r_w, CONV_DIM, CONV_W)
        out = (cw[3:4] * u + cw[2:3] * cbuf[7:7 + tile, :] + cw[1:2] * cbuf[6:6 + tile, :]
               + cw[0:1] * cbuf[5:5 + tile, :])
        cbuf[0:8, :] = cbuf[tile:tile + 8, :]
        return out

    def gdn_prep():
        log_decay, vals = _compact_gates(proj_s[:, SMALL:SMALL + 128], prm_ref)
        cum = _chunk_cumsum(keep(log_decay), row_in_chunk[:, 0:128])
        cum = _dot_exact_rhs(cum, exp_ref[:, 0:512])
        gg_s[...] = cum[:, 0:256]
        sg_s[...] = cum[:, 256:512]
        vals = _dot_exact_rhs(keep(vals), exp_ref[:, 512:1024])
        gb_s[...] = vals[:, 0:256]
        sdt_s[...] = vals[:, 256:512]
        yield
        act = _silu(conv_taps(cbuf_g, proj_s[:, B_QKV:B_QKV + CONV_DIM], R_GCONV))
        gq_s[...] = _l2n(act[:, 0:256], ones64) * QK_SCALE
        gk_s[...] = _l2n(act[:, 256:512], ones64)
        gv_s[...] = act[:, 512:768]
        yield

    def hgrn_prep():
        lb = _hgrn_lower_bound(_prow(prm_ref, R_LB, n=DEPTH), layer)
        q, k, f, v = _hgrn_inputs(proj_s[...], lb)
        hq_s[...] = q
        hk_s[...] = k
        hv_s[...] = keep(v)
        yield
        G = _chunk_cumsum(keep(jnp.log(f)), row_in_chunk)
        hg_s[...] = G
        span = jnp.zeros((1, GROUP_W), F32)
        for c in range(n_chunks):
            mid = G[c * C + C // 2 - 1:c * C + C // 2, :]
            span = jnp.maximum(span, jnp.maximum(G[c * C:c * C + 1, :] - mid, mid - G[c * C + C - 1:c * C + C, :]))
        hspan_s[...] = jnp.broadcast_to(span, hspan_s.shape)
        yield

    def hgrn_level_exps():
        G = hg_s[...]
        last = G
        s = 1
        lvl = 5
        while s < C:
            upper = (row & s) != 0
            ref_row = jnp.where(upper, pltpu.roll(last, s, axis=0), last)
            he_s[lvl] = jnp.exp(jnp.minimum(jnp.where(upper, G - ref_row, ref_row - G), 0.0))
            last = jnp.where(upper, last, pltpu.roll(last, tile - s, axis=0))
            s *= 2
            lvl -= 1

    def ssd_prep():
        act = _silu(conv_taps(cbuf_s, proj_s[:, C_XBC:C_XBC + CONV_DIM], R_SCONV)
                    + _prow(prm_ref, R_SCONVB, CONV_DIM))
        dt = sdt_s[...]
        sx_s[...] = act[:, 0:256]
        sv_s[...] = act[:, 0:256] * dt
        sb_s[...] = act[:, 256:512]
        sc_s[...] = act[:, 512:768]
        yield

    def ret_prep():
        cos = cos_ref[...]
        sin = sin_ref[...]
        rq_s[...] = _rotary(proj_s[:, D_Q:D_Q + GROUP_W], cos, sin)
        rk_s[...] = _rotary(proj_s[:, D_K:D_K + GROUP_W], cos, sin) * QK_SCALE
        rv_s[...] = keep(proj_s[:, D_V:D_V + GROUP_W])
        live_rows = jnp.clip(jnp.minimum(row_in_chunk + 1, valid - (row - row_in_chunk)), 0, C)
        rg_s[...] = live_rows.astype(F32) * _prow(prm_ref, R_RLOGG)
        yield

    ii = _iota((C, GROUP_W), 0)
    jj = _iota((C, GROUP_W), 1) & (C - 1)
    tri = jj <= ii
    strict = jj < ii
    eye = jj == ii
    eye_f = jnp.where(eye, 1.0, 0.0)
    sq_mask = (ri >> 6) == (ci >> 6)
    grp_mask = (ri >> 7) == (ci >> 7)
    bd_mask = ones64
    grp_row_mask = ones128

    def decay_parts(G):
        g_row = jnp.sum(jnp.where(eye, G, 0.0), axis=0, keepdims=True)
        dm = jnp.where(tri, jnp.exp(jnp.minimum(G - g_row, 0.0)), 0.0)
        g_last = G[C - 1:C, :]
        return dm, g_last

    def hgrn_a(c):
        sl = pl.ds(c * C, C)
        G = hg_s[sl, :]
        mid = hg_s[pl.ds(c * C + C // 2 - 1, 1), :]
        qf = hq_s[sl, :] * jnp.exp(G - mid)
        kf = hk_s[sl, :] * jnp.exp(mid - G)
        scores = jnp.where(tri, _dot_nt(qf, _block_diag(kf, bd_mask)), 0.0)
        yield
        o_s[sl, 0:256] = _dot(scores, _block_diag(hv_s[sl, :], bd_mask))
        yield

    def hgrn_a_any_decay(c, carry):
        sl = pl.ds(pl.multiple_of(c * C, C), C)
        q = hq_s[sl, :]
        k = hk_s[sl, :]
        scores = jnp.where(eye, _dot_nt(q, _block_diag(k, bd_mask)), 0.0)
        s = C // 2
        lvl = 0
        while s >= 1:
            e = he_s[lvl, sl, :]
            upper = (ii & s) != 0
            qs = jnp.where(upper, q * e, 0.0)
            ks = jnp.where(upper, 0.0, k * e)
            sc = _dot_nt(qs, _block_diag(ks, bd_mask))
            same_block = (ii & -(2 * s)) == (jj & -(2 * s))
            scores = scores + jnp.where(same_block, sc, 0.0)
            s //= 2
            lvl += 1
        o_s[sl, 0:256] = _dot(scores, _block_diag(hv_s[sl, :], bd_mask))
        return carry

    def gdn_a(c):
        sl = pl.ds(c * C, C)
        q = gq_s[sl, :]
        k = gk_s[sl, :]
        G = gg_s[sl, :]
        beta = gb_s[sl, :]
        dm, _ = decay_parts(G)
        kq = _dot_nt(jnp.concatenate([k, q], axis=0), _block_diag(k, bd_mask))
        yield
        m = jnp.where(strict, kq[0:C] * dm * beta, 0.0)
        gqk_s[sl, :] = kq[C:2 * C] * dm
        p = _dot(m, _block_diag(m, bd_mask))
        yield
        tinv = eye_f - m
        n = 2
        while 2 * n < C:
            both = _dot(jnp.concatenate([p, tinv], axis=0), _block_diag(p, bd_mask))
            yield
            p = both[0:C]
            tinv = tinv + both[C:2 * C]
            n *= 2
        tinv = tinv + _dot(tinv, _block_diag(p, bd_mask))
        yield
        kb = k * beta * jnp.exp(G)
        rhs = jnp.concatenate([_block_diag(gv_s[sl, :] * beta, bd_mask), _block_diag(kb, bd_mask)], axis=1)
        sol = _dot(tinv, rhs)
        gsv_s[sl, :] = sol[:, 0:256]
        gsk_s[sl, :] = sol[:, 256:512]
        yield

    def ssd_a(c):
        sl = pl.ds(c * C, C)
        bm = sb_s[sl, :]
        dm, _ = decay_parts(sg_s[sl, :])
        cb = _dot_nt(sc_s[sl, :], _block_diag(bm, grp_row_mask))
        yield
        o_s[sl, 512:768] = _dot(cb * dm, _block_diag(sv_s[sl, :], bd_mask))
        yield

    def ret_a(c):
        sl = pl.ds(c * C, C)
        dm, _ = decay_parts(rg_s[sl, :])
        sc = _dot_nt(rq_s[sl, :], _block_diag(rk_s[sl, :], bd_mask)) * dm
        yield
        o_s[sl, 768:1024] = _dot(sc, _block_diag(rv_s[sl, :], bd_mask))
        yield

    def hgrn_b(c):
        sl = pl.ds(c * C, C)
        G = hg_s[sl, :]
        g_last = G[C - 1:C, :]
        st = st_s[0]
        o_s[sl, 0:256] += _dot_nt(hq_s[sl, :] * jnp.exp(G), st)
        yield
        kt = hk_s[sl, :] * jnp.exp(g_last - G)
        st_s[0] = st * jnp.exp(g_last) + jnp.where(sq_mask, _dot_tn(hv_s[sl, :], kt), 0.0)
        yield

    def gdn_b(c):
        sl = pl.ds(c * C, C)
        G = gg_s[sl, :]
        g_last = G[C - 1:C, :]
        st = st_s[1]
        tmp = _dot(jnp.concatenate([gsk_s[sl, :], gq_s[sl, :] * jnp.exp(G)], axis=0), st)
        yield
        u = gsv_s[sl, :] - tmp[0:C]
        kt = gk_s[sl, :] * jnp.exp(g_last - G)
        st_s[1] = st * jnp.exp(g_last) + jnp.where(sq_mask, _dot_tn(kt, u), 0.0)
        yield
        o_s[sl, 256:512] = tmp[C:2 * C] + _dot(gqk_s[sl, :], _block_diag(u, bd_mask))
        yield

    def ssd_b(c):
        sl = pl.ds(c * C, C)
        G = sg_s[sl, :]
        g_last = G[C - 1:C, :]
        st = st_s[2]
        o_s[sl, 512:768] += jnp.exp(G) * _dot(sc_s[sl, :], st)
        yield
        vt = sv_s[sl, :] * jnp.exp(g_last - G)
        st_s[2] = st * jnp.exp(g_last) + jnp.where(grp_mask, _dot_tn(sb_s[sl, :], vt), 0.0)
        yield

    def ret_b(c):
        sl = pl.ds(c * C, C)
        G = rg_s[sl, :]
        g_last = G[C - 1:C, :]
        st = st_s[3]
        o_s[sl, 768:1024] += jnp.exp(G) * _dot(rq_s[sl, :], st)
        yield
        vt = rv_s[sl, :] * jnp.exp(g_last - G)
        st_s[3] = st * jnp.exp(g_last) + jnp.where(sq_mask, _dot_tn(rk_s[sl, :], vt), 0.0)
        yield

    def post(c):
        sl = pl.ds(c * C, C)
        y_s[sl, :] = _post_mix(o_s[sl, 0:256], o_s[sl, 256:512], o_s[sl, 512:768], o_s[sl, 768:1024],
                               proj_s[sl, A_Z:A_Z + GROUP_W], proj_s[sl, B_Z:B_Z + GROUP_W],
                               proj_s[sl, C_Z:C_Z + GROUP_W], proj_s[sl, D_Z:D_Z + GROUP_W],
                               sx_s[sl, :], prm_ref, ones64, ones128).astype(BF16)
        yield

    chunks = range(n_chunks)
    every = lambda *gens: itertools.chain.from_iterable(gens)
    z_cols = [(z, z + GROUP_W) for z in (A_Z, B_Z, C_Z, D_Z)]
    _interleave([every(project((B_QKV, B_Z), (SMALL, NP)), gdn_prep())], project((A_Q, A_Z)), 1)
    _interleave([gdn_a(c) for c in chunks],
                every(hgrn_prep(), project((C_XBC, C_Z)), *[hgrn_a(c) for c in chunks],
                      ssd_prep(), project((D_Q, D_Z)), *[ssd_a(c) for c in chunks],
                      ret_prep(), project(*z_cols), *[ret_a(c) for c in chunks]),
                FILL_PER_ROUND)

    @pl.when(jnp.max(hspan_s[0:1, :]) > HGRN_SAFE_SPAN)
    def _():
        hgrn_level_exps()
        lax.fori_loop(0, n_chunks, hgrn_a_any_decay, 0)

    for c in chunks:
        done = [post(c - 1)] if c else []
        _interleave([gdn_b(c)], every(hgrn_b(c), ssd_b(c), ret_b(c), *done), 2)
    _interleave([post(n_chunks - 1)], (), 0)

    out = h_ref[0] + jnp.dot(y_s[...], wout_ref[0], preferred_element_type=F32)
    if final:
        out = _rms(out) * _prow(prm_ref, R_FINAL, D_MODEL)
    hout_ref[0] = out

    @pl.when(t == nt - 1)
    def _():
        sout_ref[0] = st_s[...]
        ctxout_ref[0, 0] = cbuf_g[valid:valid + 8, :]
        ctxout_ref[0, 1] = cbuf_s[valid:valid + 8, :]


def _prompt_layer(layer, h, cos, sin, win, wout, prm, expand, s0, ctx0, *, tile, valid, final):
    bsz, seq, _ = h.shape
    nt = seq // tile
    kern = functools.partial(_prompt_kernel, layer, tile, valid, final)
    const2 = lambda b, t: (0, 0)
    slab = lambda: pltpu.VMEM((tile, GROUP_W), F32)
    return pl.pallas_call(
        kern,
        grid=(bsz, nt),
        in_specs=[
            pl.BlockSpec((1, tile, D_MODEL), lambda b, t: (b, t, 0)),
            pl.BlockSpec((tile, GROUP_W), lambda b, t: (t, 0)),
            pl.BlockSpec((tile, GROUP_W), lambda b, t: (t, 0)),
            pl.BlockSpec((1, D_MODEL, NP), lambda b, t: (layer, 0, 0)),
            pl.BlockSpec((1, D_MODEL, D_MODEL), lambda b, t: (layer, 0, 0)),
            pl.BlockSpec((1, N_PRM, D_MODEL), lambda b, t: (layer, 0, 0)),
            pl.BlockSpec(expand.shape, const2),
            pl.BlockSpec((4, GROUP_W, GROUP_W), lambda b, t: (0, 0, 0)),
            pl.BlockSpec((2, 8, CONV_DIM), lambda b, t: (0, 0, 0)),
        ],
        out_specs=[
            pl.BlockSpec((1, tile, D_MODEL), lambda b, t: (b, t, 0)),
            pl.BlockSpec((1, 4, GROUP_W, GROUP_W), lambda b, t: (b, 0, 0, 0)),
            pl.BlockSpec((1, 2, 8, CONV_DIM), lambda b, t: (b, 0, 0, 0)),
        ],
        out_shape=[
            jax.ShapeDtypeStruct((bsz, seq, D_MODEL), F32),
            jax.ShapeDtypeStruct((bsz, 4, GROUP_W, GROUP_W), F32),
            jax.ShapeDtypeStruct((bsz, 2, 8, CONV_DIM), F32),
        ],
        scratch_shapes=[
            pltpu.VMEM((tile, NP), F32),
            pltpu.VMEM((tile + 8, CONV_DIM), F32),
            pltpu.VMEM((tile + 8, CONV_DIM), F32),
            pltpu.VMEM((4, GROUP_W, GROUP_W), F32),
            pltpu.VMEM((6, tile, GROUP_W), F32),
        ] + [slab() for _ in range(22)] + [pltpu.VMEM((tile, D_MODEL), F32), pltpu.VMEM((tile, D_MODEL), BF16),
                                             pltpu.VMEM((tile, D_MODEL), BF16), pltpu.VMEM((8, GROUP_W), F32)],
        compiler_params=pltpu.CompilerParams(
            dimension_semantics=("arbitrary", "arbitrary"), vmem_limit_bytes=VMEM_LIMIT),
        name=f"prompt_layer_t{tile}",
    )(h, cos, sin, win, wout, prm, expand, s0, ctx0)


(VT_HQ, VT_HK, VT_HF, VT_HV, VT_GQ, VT_GK, VT_GV, VT_GDEC, VT_GBETA,
 VT_SV, VT_SB, VT_SC, VT_SDEC, VT_RQ, VT_RK, VT_RV, VT_RDEC) = [GROUP_W * i for i in range(17)]
N_VT = 17 * GROUP_W
DEC_VBLOCK = 32


def _dec_pre_kernel(layer, hs_ref, cos_ref, sin_ref, win_ref, prm_ref, exp_ref, gctx_ref, sctx_ref,
                    vt_ref, post_ref, ug_ref, us_ref):
    x = hs_ref[...]
    nb = x.shape[0]
    hn = _rms(x) * _prow(prm_ref, R_NORM, D_MODEL)
    proj = jnp.dot(hn.astype(BF16), win_ref[0], preferred_element_type=F32)

    ri = _iota((GROUP_W, GROUP_W), 0)
    ci = _iota((GROUP_W, GROUP_W), 1)
    ones64 = jnp.where((ri >> 6) == (ci >> 6), 1.0, 0.0).astype(BF16)

    def put(off, a):
        vt_ref[off:off + GROUP_W, :] = jnp.transpose(a)

    lb = _hgrn_lower_bound(_prow(prm_ref, R_LB, n=DEPTH), layer)
    q, k, f, v = _hgrn_inputs(proj, lb)
    put(VT_HQ, q)
    put(VT_HK, k)
    put(VT_HF, f)
    put(VT_HV, v)

    log_decay, vals = _compact_gates(proj[:, SMALL:SMALL + 128], prm_ref)
    decay = jnp.exp(_dot_exact_rhs(log_decay, exp_ref[:, 0:512]))
    vals = _dot_exact_rhs(vals, exp_ref[:, 512:1024])

    u = proj[:, B_QKV:B_QKV + CONV_DIM]
    ug_ref[...] = u
    cw = _prow(prm_ref, R_GCONV, CONV_DIM, CONV_W)
    conv = cw[3:4] * u + cw[2:3] * gctx_ref[0, 2] + cw[1:2] * gctx_ref[0, 1] + cw[0:1] * gctx_ref[0, 0]
    act = _silu(conv)
    put(VT_GQ, _l2n(act[:, 0:256], ones64) * QK_SCALE)
    put(VT_GK, _l2n(act[:, 256:512], ones64))
    put(VT_GV, act[:, 512:768])
    put(VT_GDEC, decay[:, 0:256])
    put(VT_GBETA, vals[:, 0:256])

    u = proj[:, C_XBC:C_XBC + CONV_DIM]
    us_ref[...] = u
    cw = _prow(prm_ref, R_SCONV, CONV_DIM, CONV_W)
    conv = (cw[3:4] * u + cw[2:3] * sctx_ref[0, 2] + cw[1:2] * sctx_ref[0, 1] + cw[0:1] * sctx_ref[0, 0]
            + _prow(prm_ref, R_SCONVB, CONV_DIM))
    act = _silu(conv)
    post_ref[:, POST_SX:POST_SX + 256] = act[:, 0:256]
    put(VT_SV, act[:, 0:256] * vals[:, 256:512])
    put(VT_SB, act[:, 256:512])
    put(VT_SC, act[:, 512:768])
    put(VT_SDEC, decay[:, 256:512])

    cos = cos_ref[...]
    sin = sin_ref[...]
    put(VT_RQ, _rotary(proj[:, D_Q:D_Q + GROUP_W], cos, sin))
    put(VT_RK, _rotary(proj[:, D_K:D_K + GROUP_W], cos, sin) * QK_SCALE)
    put(VT_RV, proj[:, D_V:D_V + GROUP_W])
    put(VT_RDEC, jnp.broadcast_to(jnp.exp(_prow(prm_ref, R_RLOGG)), (nb, GROUP_W)))

    post_ref[:, POST_AZ:POST_AZ + 256] = proj[:, A_Z:A_Z + GROUP_W]
    post_ref[:, POST_BZ:POST_BZ + 256] = proj[:, B_Z:B_Z + GROUP_W]
    post_ref[:, POST_CZ:POST_CZ + 256] = proj[:, C_Z:C_Z + GROUP_W]
    post_ref[:, POST_DZ:POST_DZ + 256] = proj[:, D_Z:D_Z + GROUP_W]


def _dec_pre(layer, hs, cos, sin, win, prm, expand, gctx, sctx):
    nb = hs.shape[0]
    full = lambda a: pl.BlockSpec(a.shape, lambda i: (0,) * a.ndim)
    shapes = [(N_VT, nb), (nb, N_POST), (nb, CONV_DIM), (nb, CONV_DIM)]
    return pl.pallas_call(
        functools.partial(_dec_pre_kernel, layer),
        grid=(1,),
        in_specs=[
            full(hs), full(cos), full(sin),
            pl.BlockSpec((1, D_MODEL, NP), lambda i: (layer, 0, 0)),
            pl.BlockSpec((1, N_PRM, D_MODEL), lambda i: (layer, 0, 0)),
            full(expand),
            pl.BlockSpec((1, 3, nb, CONV_DIM), lambda i: (layer, 0, 0, 0)),
            pl.BlockSpec((1, 3, nb, CONV_DIM), lambda i: (layer, 0, 0, 0)),
        ],
        out_specs=[pl.BlockSpec(s, lambda i: (0, 0)) for s in shapes],
        out_shape=[jax.ShapeDtypeStruct(s, F32) for s in shapes],
        compiler_params=pltpu.CompilerParams(dimension_semantics=("arbitrary",), vmem_limit_bytes=VMEM_LIMIT),
        name="decode_pre",
    )(hs, cos, sin, win, prm, expand, gctx, sctx)


def _dec_state_kernel(vt_ref, hg_ref, gd_ref, sd_ref, rt_ref, hg_acc, gd_acc, rt_acc,
                      o_ref, hg_out, gd_out, sd_out, rt_out):
    del hg_acc, gd_acc, rt_acc
    h = pl.program_id(0)
    j = pl.program_id(1)
    vb = DEC_VBLOCK
    nb = vt_ref.shape[1]
    head = h * HEAD_DIM

    def row(off, i):
        return vt_ref[pl.ds(off + i, 1), :]

    def slab(off):
        return vt_ref[pl.ds(pl.multiple_of(off + head + j * vb, 8), vb), :]

    zero = jnp.zeros((vb, nb), F32)

    v = slab(VT_HV)

    def hgrn_step(k, o):
        s = row(VT_HF + head, k) * hg_ref[0, 0, k] + row(VT_HK + head, k) * v
        hg_out[0, 0, k] = s
        return o + row(VT_HQ + head, k) * s

    o_ref[0, 0] = lax.fori_loop(0, HEAD_DIM, hgrn_step, zero, unroll=4)

    v = slab(VT_GV)
    dec = row(VT_GDEC + head, 0)
    ks = lax.fori_loop(0, HEAD_DIM, lambda k, a: a + row(VT_GK + head, k) * gd_ref[0, 0, k], zero, unroll=4)
    u = row(VT_GBETA + head, 0) * (v - ks * dec)

    def gdn_step(k, o):
        s = gd_ref[0, 0, k] * dec + row(VT_GK + head, k) * u
        gd_out[0, 0, k] = s
        return o + row(VT_GQ + head, k) * s

    o_ref[1, 0] = lax.fori_loop(0, HEAD_DIM, gdn_step, zero, unroll=4)

    v = slab(VT_SV)
    dec = row(VT_SDEC + head, 0)
    grp = (h >> 1) * 128

    def ssd_step(n, o):
        s = sd_ref[0, 0, n] * dec + row(VT_SB + grp, n) * v
        sd_out[0, 0, n] = s
        return o + row(VT_SC + grp, n) * s

    o_ref[2, 0] = lax.fori_loop(0, 128, ssd_step, zero, unroll=4)

    v = slab(VT_RV)
    dec = row(VT_RDEC + head, 0)

    def ret_step(k, o):
        s = rt_ref[0, 0, k] * dec + row(VT_RK + head, k) * v
        rt_out[0, 0, k] = s
        return o + row(VT_RQ + head, k) * s

    o_ref[3, 0] = lax.fori_loop(0, HEAD_DIM, ret_step, zero, unroll=4)


def _dec_state(layer, vt, states, accs):
    nb = vt.shape[1]
    vb = DEC_VBLOCK
    blk = lambda a: pl.BlockSpec((1, 1, a.shape[2], vb, nb), lambda h, j: (layer, h, 0, j, 0))
    return pl.pallas_call(
        _dec_state_kernel,
        grid=(N_HEADS, HEAD_DIM // vb),
        in_specs=[pl.BlockSpec(vt.shape, lambda h, j: (0, 0))] + [blk(a) for a in states]
        + [pl.BlockSpec(memory_space=pl.ANY)] * 3,
        out_specs=[pl.BlockSpec((4, 1, vb, nb), lambda h, j: (0, h, j, 0))] + [blk(a) for a in states],
        out_shape=[jax.ShapeDtypeStruct((4, N_HEADS, HEAD_DIM, nb), F32)]
        + [jax.ShapeDtypeStruct(a.shape, F32) for a in states],
        input_output_aliases={3: 3, 5: 1, 6: 2, 7: 4},
        compiler_params=pltpu.CompilerParams(dimension_semantics=("arbitrary", "arbitrary"),
                                             vmem_limit_bytes=VMEM_LIMIT),
        name="decode_state",
    )(vt, *states, *accs)


def _dec_post_kernel(final, hs_ref, ot_ref, post_ref, wout_ref, prm_ref, out_ref):
    ri = _iota((GROUP_W, GROUP_W), 0)
    ci = _iota((GROUP_W, GROUP_W), 1)
    ones64 = jnp.where((ri >> 6) == (ci >> 6), 1.0, 0.0).astype(BF16)
    ones128 = jnp.where((ri >> 7) == (ci >> 7), 1.0, 0.0).astype(BF16)
    gate = lambda off: post_ref[:, off:off + GROUP_W]
    o = [jnp.transpose(ot_ref[m * GROUP_W:(m + 1) * GROUP_W, :]) for m in range(4)]
    y = _post_mix(o[0], o[1], o[2], o[3],
                  gate(POST_AZ), gate(POST_BZ), gate(POST_CZ), gate(POST_DZ), gate(POST_SX),
                  prm_ref, ones64, ones128)
    out = hs_ref[...] + jnp.dot(y.astype(BF16), wout_ref[0], preferred_element_type=F32)
    if final:
        out = _rms(out) * _prow(prm_ref, R_FINAL, D_MODEL)
    out_ref[...] = out


def _dec_post(layer, hs, ot, post, wout, prm, final):
    nb = hs.shape[0]
    full = lambda a: pl.BlockSpec(a.shape, lambda i: (0,) * a.ndim)
    return pl.pallas_call(
        functools.partial(_dec_post_kernel, final),
        grid=(1,),
        in_specs=[full(hs), full(ot), full(post),
                  pl.BlockSpec((1, D_MODEL, D_MODEL), lambda i: (layer, 0, 0)),
                  pl.BlockSpec((1, N_PRM, D_MODEL), lambda i: (layer, 0, 0))],
        out_specs=pl.BlockSpec((nb, D_MODEL), lambda i: (0, 0)),
        out_shape=jax.ShapeDtypeStruct((nb, D_MODEL), F32),
        compiler_params=pltpu.CompilerParams(dimension_semantics=("arbitrary",), vmem_limit_bytes=VMEM_LIMIT),
        name="decode_post",
    )(hs, ot, post, wout, prm)


POST_AZ, POST_BZ, POST_CZ, POST_DZ, POST_SX = [GROUP_W * i for i in range(5)]
N_POST = 5 * GROUP_W


def _expand_matrix():
    e = np.zeros((128, 4 * GROUP_W), np.float32)
    for block, first_lane in enumerate((0, 2 * N_HEADS, N_HEADS, 2 * N_HEADS)):
        for h in range(N_HEADS):
            lo = block * GROUP_W + h * HEAD_DIM
            e[first_lane + h, lo:lo + HEAD_DIM] = 1.0
    return jnp.asarray(e, BF16)


def _relayout_w_in(w_in):
    sizes = (256, 256, 256, 256, 768, 256, 4, 4, 768, 256, 4, 256, 256, 256, 256)
    offs = np.concatenate([[0], np.cumsum(sizes)])
    w_t = jnp.transpose(w_in, (2, 0, 1))
    seg = lambda i: w_t[offs[i]:offs[i + 1]]
    pad = jnp.zeros((128 - 12,) + w_t.shape[1:], w_in.dtype)
    rows = [seg(0), seg(1), seg(2), seg(3), seg(4), seg(5), seg(8), seg(9), seg(11), seg(12), seg(13), seg(14),
            seg(6), seg(7), seg(10), pad]
    return jnp.transpose(jnp.concatenate(rows, axis=0), (1, 2, 0)).astype(BF16)


def _pack_params(norm_w, hgrn_lb_logits, hgrn_norm_w, gdn_conv_w, gdn_a_log, gdn_dt_bias, gdn_norm_w,
                 ssd_conv_w, ssd_conv_b, ssd_a_log, ssd_dt_bias, ssd_d, ssd_norm_w, ret_norm_w, ret_norm_b,
                 final_norm_w):
    depth = norm_w.shape[0]
    prm = jnp.zeros((depth, N_PRM, D_MODEL), F32)

    def put(p, r, a):
        a = a.astype(F32)
        if a.ndim == 2:
            a = a[:, None, :]
        return p.at[:, r:r + a.shape[1], 0:a.shape[2]].set(a)

    rep = lambda a: jnp.repeat(a, HEAD_DIM, axis=-1)
    prm = put(prm, R_NORM, norm_w)
    prm = put(prm, R_LB, jnp.broadcast_to(hgrn_lb_logits[None], (depth,) + hgrn_lb_logits.shape))
    prm = put(prm, R_HNORM, hgrn_norm_w)
    prm = put(prm, R_GCONV, gdn_conv_w)
    prm = put(prm, R_GNORM, gdn_norm_w)
    prm = put(prm, R_SCONV, ssd_conv_w)
    prm = put(prm, R_SCONVB, ssd_conv_b)
    prm = put(prm, R_SD, rep(ssd_d))
    gap = jnp.zeros((depth, N_HEADS), F32)
    prm = put(prm, R_CBIAS, jnp.concatenate([gdn_dt_bias.astype(F32), gap, ssd_dt_bias.astype(F32)], axis=-1))
    prm = put(prm, R_CALOG, jnp.concatenate([gdn_a_log.astype(F32), gap, ssd_a_log.astype(F32)], axis=-1))
    prm = put(prm, R_SNORM, ssd_norm_w)
    prm = put(prm, R_RNORM, ret_norm_w)
    prm = put(prm, R_RNORMB, ret_norm_b)
    ret_logg = jnp.log1p(-jnp.exp2(-5.0 - jnp.arange(N_HEADS, dtype=F32)))
    prm = put(prm, R_RLOGG, jnp.broadcast_to(rep(ret_logg)[None], (depth, GROUP_W)))
    prm = put(prm, R_FINAL, jnp.broadcast_to(final_norm_w[None], (depth, D_MODEL)))
    return prm


def _rope_tables(pos):
    half = HEAD_DIM // 2
    inv_freq = 1.0 / (ROPE_BASE ** jnp.linspace(0.0, 1.0, half, dtype=F32))
    ang = pos[:, None] * inv_freq[None, :]
    cos, sin = jnp.cos(ang), jnp.sin(ang)
    cos_h = jnp.concatenate([cos, cos], axis=-1)
    sin_h = jnp.concatenate([-sin, sin], axis=-1)
    return jnp.tile(cos_h, (1, N_HEADS)), jnp.tile(sin_h, (1, N_HEADS))


def _diag_blocks(s, rows, cols):
    return jnp.stack([s[:, (h * rows):(h + 1) * rows, h * cols:(h + 1) * cols] for h in range(N_HEADS)], axis=1)


def _unpack_states(sout):
    hgrn = jnp.swapaxes(_diag_blocks(sout[:, 0], HEAD_DIM, HEAD_DIM), -1, -2)
    gdn = _diag_blocks(sout[:, 1], HEAD_DIM, HEAD_DIM)
    ssd = jnp.stack([sout[:, 2, (h // 2) * 128:(h // 2 + 1) * 128, h * HEAD_DIM:(h + 1) * HEAD_DIM]
                     for h in range(N_HEADS)], axis=1)
    ret = _diag_blocks(sout[:, 3], HEAD_DIM, HEAD_DIM)
    return hgrn, gdn, ssd, ret


def kernel(x_prompt, x_sample, state_hgrn, state_gdn, state_gdn_conv, state_ssd, state_ssd_conv, state_ret,
           meta_tokens, norm_w, w_in, hgrn_lb_logits, hgrn_norm_w, gdn_conv_w, gdn_a_log, gdn_dt_bias, gdn_norm_w,
           ssd_conv_w, ssd_conv_b, ssd_a_log, ssd_dt_bias, ssd_d, ssd_norm_w, ret_norm_w, ret_norm_b,
           w_out, final_norm_w):
    depth = w_in.shape[0]
    bsz, seq, _ = x_prompt.shape
    nb = x_sample.shape[0]
    tile = PROMPT_TILE if seq % PROMPT_TILE == 0 else CHUNK

    win = _relayout_w_in(w_in)
    wout = w_out.astype(BF16)
    prm = _pack_params(norm_w, hgrn_lb_logits, hgrn_norm_w, gdn_conv_w, gdn_a_log, gdn_dt_bias, gdn_norm_w,
                       ssd_conv_w, ssd_conv_b, ssd_a_log, ssd_dt_bias, ssd_d, ssd_norm_w, ret_norm_w, ret_norm_b,
                       final_norm_w)
    expand = _expand_matrix()

    cos_m, sin_m = _rope_tables(jnp.arange(CHUNK, dtype=F32))
    cos_p, sin_p = _rope_tables(N_META + jnp.arange(seq, dtype=F32))
    cos_s, sin_s = _rope_tables(PAST_LEN + jnp.arange(1, dtype=F32))

    hm = jnp.zeros((1, CHUNK, D_MODEL), F32).at[0, :N_META].set(meta_tokens.astype(F32))
    hp = x_prompt.astype(F32)
    hs = x_sample.astype(F32)[:, 0, :]
    zero_s = jnp.zeros((4, GROUP_W, GROUP_W), F32)
    zero_ctx = jnp.zeros((2, 8, CONV_DIM), F32)

    gctx = jnp.swapaxes(state_gdn_conv.astype(F32), 1, 2)
    sctx = jnp.swapaxes(state_ssd_conv.astype(F32), 1, 2)

    seq_minor = lambda a: jnp.transpose(a.astype(F32), (0, 2, 3, 4, 1))
    dec_states = [seq_minor(a) for a in (state_hgrn, state_gdn, state_ssd, state_ret)]
    dec_new = [jnp.zeros_like(dec_states[0]), jnp.zeros_like(dec_states[1]), dec_states[2],
               jnp.zeros_like(dec_states[3])]

    p_states, p_ctx = [], []
    s_gctx, s_sctx = [], []
    for l in range(depth):
        final = l == depth - 1
        hm, sm, cm = _prompt_layer(l, hm, cos_m, sin_m, win, wout, prm, expand, zero_s, zero_ctx,
                                   tile=CHUNK, valid=N_META, final=False)
        hp, sp, cp = _prompt_layer(l, hp, cos_p, sin_p, win, wout, prm, expand, sm[0], cm[0],
                                   tile=tile, valid=tile, final=final)
        p_states.append(sp)
        p_ctx.append(cp)

        vt, post, ug, us = _dec_pre(l, hs, cos_s, sin_s, win, prm, expand, gctx, sctx)
        ot, *dec_new = _dec_state(l, vt, [dec_states[0], dec_states[1], dec_new[2], dec_states[3]],
                                  [dec_new[0], dec_new[1], dec_new[3]])
        hs = _dec_post(l, hs, ot.reshape(4 * GROUP_W, nb), post, wout, prm, final)
        s_gctx.append(jnp.concatenate([state_gdn_conv[l, :, 1:].astype(F32), ug[:, None, :]], axis=1))
        s_sctx.append(jnp.concatenate([state_ssd_conv[l, :, 1:].astype(F32), us[:, None, :]], axis=1))

    sp = jnp.stack(p_states)
    cp = jnp.stack(p_ctx)
    ph, pg, ps, pr = _unpack_states(sp.reshape((depth * bsz,) + sp.shape[2:]))
    unb = lambda a: a.reshape((depth, bsz) + a.shape[1:])
    seq_major = lambda a: jnp.transpose(a, (0, 4, 1, 2, 3))
    return (hp.astype(x_prompt.dtype), hs[:, None, :].astype(x_sample.dtype),
            unb(ph).astype(state_hgrn.dtype), unb(pg).astype(state_gdn.dtype),
            cp[:, :, 0, 8 - (CONV_W - 1):].astype(state_gdn_conv.dtype),
            unb(ps).astype(state_ssd.dtype),
            cp[:, :, 1, 8 - (CONV_W - 1):].astype(state_ssd_conv.dtype),
            unb(pr).astype(state_ret.dtype),
            seq_major(dec_new[0]).astype(state_hgrn.dtype), seq_major(dec_new[1]).astype(state_gdn.dtype),
            jnp.stack(s_gctx).astype(state_gdn_conv.dtype), seq_major(dec_new[2]).astype(state_ssd.dtype),
            jnp.stack(s_sctx).astype(state_ssd_conv.dtype), seq_major(dec_new[3]).astype(state_ret.dtype))
```

```python
import functools
import itertools
import math

import numpy as np
import jax
import jax.numpy as jnp
from jax import lax
from jax.experimental import pallas as pl
from jax.experimental.pallas import tpu as pltpu

F32 = jnp.float32
BF16 = jnp.bfloat16

D_MODEL = 1024
GROUP_W = 256
HEAD_DIM = 64
N_HEADS = 4
CHUNK = 64
CONV_W = 4
CONV_DIM = 768
N_META = 16
PAST_LEN = 16384
ROPE_BASE = 10000.0
EPS = 1e-6
TINY = 1e-30
QK_SCALE = HEAD_DIM ** -0.5
DEPTH = 4

A_Q, A_F, A_I, A_Z = 0, 256, 512, 768
B_QKV, B_Z = 1024, 1792
C_XBC, C_Z = 2048, 2816
D_Q, D_K, D_V, D_Z = 3072, 3328, 3584, 3840
SMALL = 4096
NP = 4224

R_NORM, R_LB, R_HNORM = 0, 1, 5
R_GCONV, R_GNORM = 6, 12
R_SCONV, R_SCONVB, R_SD, R_SNORM = 13, 17, 20, 21
R_RNORM, R_RNORMB, R_RLOGG, R_FINAL = 22, 23, 24, 25
R_CBIAS, R_CALOG = 26, 27
N_PRM = 32

VMEM_LIMIT = 56 * 1024 * 1024
PROMPT_TILE = 256
DEC_BLOCK = 8
HGRN_SAFE_SPAN = 80.0
FILL_PER_ROUND = 4
_DONE = object()


def _interleave(leads, fillers, fill_per_round):
    leads = list(leads)
    fillers = iter(fillers)
    fill_live = fill_per_round > 0
    while leads or fill_live:
        leads = [g for g in leads if next(g, _DONE) is not _DONE]
        for _ in range(fill_per_round):
            if fill_live and next(fillers, _DONE) is _DONE:
                fill_live = False


def _dot(a, b):
    return jnp.dot(a.astype(BF16), b.astype(BF16), preferred_element_type=F32)


def _dot_nt(a, b):
    return lax.dot_general(a.astype(BF16), b.astype(BF16), (((1,), (1,)), ((), ())),
                           preferred_element_type=F32)


def _dot_tn(a, b):
    return lax.dot_general(a.astype(BF16), b.astype(BF16), (((0,), (0,)), ((), ())),
                           preferred_element_type=F32)


def _split3(x):
    hi = x.astype(BF16)
    r1 = x - hi.astype(F32)
    mid = r1.astype(BF16)
    lo = (r1 - mid.astype(F32)).astype(BF16)
    return hi, mid, lo


def _dot_exact_rhs(x, m):
    hi, mid, lo = _split3(x)
    return (jnp.dot(hi, m, preferred_element_type=F32) + jnp.dot(mid, m, preferred_element_type=F32)
            + jnp.dot(lo, m, preferred_element_type=F32))


def _dot_exact_lhs(m, x):
    hi, mid, lo = _split3(x)
    return (jnp.dot(m, hi, preferred_element_type=F32) + jnp.dot(m, mid, preferred_element_type=F32)
            + jnp.dot(m, lo, preferred_element_type=F32))


def _sigmoid(x):
    return 1.0 / (1.0 + jnp.exp(-x))


def _silu(x):
    return x * _sigmoid(x)


def _softplus(x):
    return jnp.maximum(x, 0.0) + jnp.log(1.0 + jnp.exp(-jnp.abs(x)))


def _iota(shape, dim):
    return lax.broadcasted_iota(jnp.int32, shape, dim)


def _block_diag(x, mask01):
    xb = x.astype(BF16)
    return jnp.concatenate([xb] * N_HEADS, axis=0) * mask01


def _chunk_cumsum(x, row_in_chunk):
    sh = 1
    while sh < CHUNK:
        r = pltpu.roll(x, sh, axis=0)
        x = x + jnp.where(row_in_chunk >= sh, r, 0.0)
        sh *= 2
    return x


def _swap_halves(x):
    lane = _iota(x.shape, 1)
    fwd = pltpu.roll(x, GROUP_W - HEAD_DIM // 2, axis=1)
    bwd = pltpu.roll(x, HEAD_DIM // 2, axis=1)
    return jnp.where((lane & (HEAD_DIM - 1)) < HEAD_DIM // 2, fwd, bwd)


def _group_sums(xs, ones_bd, squares=()):
    parts, spans = [], []
    for i, x in enumerate(xs):
        hi = x.astype(BF16)
        terms = [hi] if i in squares else [hi, (x - hi.astype(F32)).astype(BF16)]
        spans.append((len(parts), len(terms)))
        parts += terms
    sums = jnp.dot(jnp.concatenate(parts, axis=0) if len(parts) > 1 else parts[0], ones_bd,
                   preferred_element_type=F32)
    n = xs[0].shape[0]
    out = []
    for first, count in spans:
        total = sums[first * n:(first + 1) * n]
        for j in range(1, count):
            total = total + sums[(first + j) * n:(first + j + 1) * n]
        out.append(total)
    return out


def _group_sum(x, ones_bd, square=False):
    return _group_sums([x], ones_bd, squares=(0,) if square else ())[0]


def _rms(x):
    return x * lax.rsqrt(jnp.mean(x * x, axis=-1, keepdims=True) + EPS)


def _hgrn_lower_bound(logits, layer):
    m = jnp.max(logits, axis=0, keepdims=True)
    e = jnp.exp(logits - m)
    w = e / jnp.sum(e, axis=0, keepdims=True)
    cum = jnp.sum(w[0:layer + 1], axis=0, keepdims=True) - w[0:1]
    return jnp.maximum(cum, 0.0)


def _prow(prm_ref, r, width=GROUP_W, n=1):
    return prm_ref[0, r:r + n, 0:width]


def _hgrn_inputs(proj, lb):
    aq = proj[:, A_Q:A_Q + GROUP_W]
    af = proj[:, A_F:A_F + GROUP_W]
    q = _silu(aq) * QK_SCALE
    k = (1.0 - lb) * _sigmoid(-af)
    f = jnp.maximum(lb + (1.0 - lb) * _sigmoid(af), TINY)
    return q, k, f, proj[:, A_I:A_I + GROUP_W]


def _compact_gates(c, prm_ref):
    sp = _softplus(c + _prow(prm_ref, R_CBIAS, 128))
    log_decay = -jnp.exp(_prow(prm_ref, R_CALOG, 128)) * sp
    lane = _iota(c.shape, 1)
    vals = jnp.where((lane >= N_HEADS) & (lane < 2 * N_HEADS), _sigmoid(c), sp)
    return log_decay, vals


def _l2n_pair(q, k, ones64):
    sq_q, sq_k = _group_sums([q * q, k * k], ones64, squares=(0, 1))
    return q * lax.rsqrt(sq_q + EPS), k * lax.rsqrt(sq_k + EPS)


def _rotary(x, cos, sin):
    return x * cos + _swap_halves(x) * sin


def _post_mix(o_a, o_b, o_c, o_d, z_a, z_b, z_c, z_d, x_ssd, prm_ref, ones64, ones128):
    inv64 = 1.0 / HEAD_DIM
    sq_a, sq_b, sum_d = _group_sums([o_a * o_a, o_b * o_b, o_d], ones64, squares=(0, 1))
    y_a = o_a * lax.rsqrt(sq_a * inv64 + EPS) * _prow(prm_ref, R_HNORM) * _silu(z_a)
    y_b = o_b * lax.rsqrt(sq_b * inv64 + EPS) * _prow(prm_ref, R_GNORM) * _silu(z_b)
    t_c = (o_c + _prow(prm_ref, R_SD) * x_ssd) * _silu(z_c)
    y_c = (t_c * lax.rsqrt(_group_sum(t_c * t_c, ones128, square=True) * (1.0 / 128.0) + EPS)
           * _prow(prm_ref, R_SNORM))
    xc = o_d - sum_d * inv64
    y_d = ((xc * lax.rsqrt(_group_sum(xc * xc, ones64, square=True) * inv64 + EPS) * _prow(prm_ref, R_RNORM)
            + _prow(prm_ref, R_RNORMB)) * _silu(z_d))
    return jnp.concatenate([y_a, y_b, y_c, y_d], axis=-1)


def _prompt_kernel(layer, tile, valid, final,
                   h_ref, cos_ref, sin_ref, win_ref, wout_ref, prm_ref, exp_ref, s0_ref, ctx0_ref,
                   hout_ref, sout_ref, ctxout_ref,
                   proj_s, cbuf_g, cbuf_s, st_s, he_s,
                   hq_s, hk_s, hv_s, hg_s,
                   gq_s, gk_s, gv_s, gg_s, gb_s, gqk_s, gsv_s, gsk_s,
                   sx_s, sv_s, sb_s, sc_s, sg_s, sdt_s,
                   rq_s, rk_s, rv_s, rg_s, o_s, y_s, hn_s, hspan_s):
    t = pl.program_id(1)
    nt = pl.num_programs(1)
    n_chunks = tile // CHUNK
    C = CHUNK

    @pl.when(t == 0)
    def _():
        st_s[...] = s0_ref[...]
        cbuf_g[0:8, :] = ctx0_ref[0]
        cbuf_s[0:8, :] = ctx0_ref[1]

    hn_s[...] = (_rms(h_ref[0]) * _prow(prm_ref, R_NORM, D_MODEL)).astype(BF16)

    def project(*col_ranges):
        for lo, hi in col_ranges:
            proj_s[:, lo:hi] = jnp.dot(hn_s[...], win_ref[0, :, lo:hi], preferred_element_type=F32)
        yield

    row = _iota((tile, GROUP_W), 0)
    row_in_chunk = row & (C - 1)
    if valid < tile:
        live = row < valid
        keep = lambda a: jnp.where(live[:, 0:a.shape[1]], a, 0.0)
    else:
        keep = lambda a: a

    ri = _iota((GROUP_W, GROUP_W), 0)
    ci = _iota((GROUP_W, GROUP_W), 1)
    ones64 = jnp.where((ri >> 6) == (ci >> 6), 1.0, 0.0).astype(BF16)
    ones128 = jnp.where((ri >> 7) == (ci >> 7), 1.0, 0.0).astype(BF16)

    def conv_taps(cbuf, u, r_w):
        cbuf[8:8 + tile, :] = u
        cw = _prow(prm_ref, r_w, CONV_DIM, CONV_W)
        ext = cbuf[...]
        acc = cw[0:1] * ext
        for tap in range(1, CONV_W):
            acc = pltpu.roll(acc, 1, axis=0) + cw[tap:tap + 1] * ext
        cbuf[0:8, :] = cbuf[tile:tile + 8, :]
        return acc[8:8 + tile, :]

    def gdn_prep():
        log_decay, vals = _compact_gates(proj_s[:, SMALL:SMALL + 128], prm_ref)
        cum = _chunk_cumsum(keep(log_decay), row_in_chunk[:, 0:128])
        cum = _dot_exact_rhs(cum, exp_ref[:, 0:512])
        gg_s[...] = cum[:, 0:256]
        sg_s[...] = cum[:, 256:512]
        vals = _dot_exact_rhs(keep(vals), exp_ref[:, 512:1024])
        gb_s[...] = vals[:, 0:256]
        sdt_s[...] = vals[:, 256:512]
        yield
        act = _silu(conv_taps(cbuf_g, proj_s[:, B_QKV:B_QKV + CONV_DIM], R_GCONV))
        qn, kn = _l2n_pair(act[:, 0:256], act[:, 256:512], ones64)
        gq_s[...] = qn * QK_SCALE
        gk_s[...] = kn
        gv_s[...] = act[:, 512:768]
        yield

    def hgrn_prep():
        lb = _hgrn_lower_bound(_prow(prm_ref, R_LB, n=DEPTH), layer)
        q, k, f, v = _hgrn_inputs(proj_s[...], lb)
        hq_s[...] = q
        hk_s[...] = k
        hv_s[...] = keep(v)
        yield
        G = _chunk_cumsum(keep(jnp.log(f)), row_in_chunk)
        hg_s[...] = G
        span = jnp.zeros((1, GROUP_W), F32)
        for c in range(n_chunks):
            mid = G[c * C + C // 2 - 1:c * C + C // 2, :]
            span = jnp.maximum(span, jnp.maximum(G[c * C:c * C + 1, :] - mid, mid - G[c * C + C - 1:c * C + C, :]))
        hspan_s[...] = jnp.broadcast_to(span, hspan_s.shape)
        yield

    def hgrn_level_exps():
        G = hg_s[...]
        last = G
        s = 1
        lvl = 5
        while s < C:
            upper = (row & s) != 0
            ref_row = jnp.where(upper, pltpu.roll(last, s, axis=0), last)
            he_s[lvl] = jnp.exp(jnp.minimum(jnp.where(upper, G - ref_row, ref_row - G), 0.0))
            last = jnp.where(upper, last, pltpu.roll(last, tile - s, axis=0))
            s *= 2
            lvl -= 1

    def ssd_prep():
        act = _silu(conv_taps(cbuf_s, proj_s[:, C_XBC:C_XBC + CONV_DIM], R_SCONV)
                    + _prow(prm_ref, R_SCONVB, CONV_DIM))
        dt = sdt_s[...]
        sx_s[...] = act[:, 0:256]
        sv_s[...] = act[:, 0:256] * dt
        sb_s[...] = act[:, 256:512]
        sc_s[...] = act[:, 512:768]
        yield

    def ret_prep():
        cos = cos_ref[...]
        sin = sin_ref[...]
        rq_s[...] = _rotary(proj_s[:, D_Q:D_Q + GROUP_W], cos, sin)
        rk_s[...] = _rotary(proj_s[:, D_K:D_K + GROUP_W], cos, sin) * QK_SCALE
        rv_s[...] = keep(proj_s[:, D_V:D_V + GROUP_W])
        live_rows = jnp.clip(jnp.minimum(row_in_chunk + 1, valid - (row - row_in_chunk)), 0, C)
        rg_s[...] = live_rows.astype(F32) * _prow(prm_ref, R_RLOGG)
        yield

    ii = _iota((C, GROUP_W), 0)
    jj = _iota((C, GROUP_W), 1) & (C - 1)
    tri = jj <= ii
    strict = jj < ii
    eye = jj == ii
    eye_f = jnp.where(eye, 1.0, 0.0)
    sq_mask = (ri >> 6) == (ci >> 6)
    grp_mask = (ri >> 7) == (ci >> 7)
    bd_mask = ones64
    grp_row_mask = ones128

    def decay_parts(G):
        g_row = jnp.sum(jnp.where(eye, G, 0.0), axis=0, keepdims=True)
        dm = jnp.where(tri, jnp.exp(jnp.minimum(G - g_row, 0.0)), 0.0)
        g_last = G[C - 1:C, :]
        return dm, g_last

    def hgrn_a(c):
        sl = pl.ds(c * C, C)
        G = hg_s[sl, :]
        mid = hg_s[pl.ds(c * C + C // 2 - 1, 1), :]
        qf = hq_s[sl, :] * jnp.exp(G - mid)
        kf = hk_s[sl, :] * jnp.exp(mid - G)
        scores = jnp.where(tri, _dot_nt(qf, _block_diag(kf, bd_mask)), 0.0)
        yield
        o_s[sl, 0:256] = _dot(scores, _block_diag(hv_s[sl, :], bd_mask))
        yield

    def hgrn_a_any_decay(c, carry):
        sl = pl.ds(pl.multiple_of(c * C, C), C)
        q = hq_s[sl, :]
        k = hk_s[sl, :]
        scores = jnp.where(eye, _dot_nt(q, _block_diag(k, bd_mask)), 0.0)
        s = C // 2
        lvl = 0
        while s >= 1:
            e = he_s[lvl, sl, :]
            upper = (ii & s) != 0
            qs = jnp.where(upper, q * e, 0.0)
            ks = jnp.where(upper, 0.0, k * e)
            sc = _dot_nt(qs, _block_diag(ks, bd_mask))
            same_block = (ii & -(2 * s)) == (jj & -(2 * s))
            scores = scores + jnp.where(same_block, sc, 0.0)
            s //= 2
            lvl += 1
        o_s[sl, 0:256] = _dot(scores, _block_diag(hv_s[sl, :], bd_mask))
        return carry

    def gdn_a(c):
        sl = pl.ds(c * C, C)
        q = gq_s[sl, :]
        k = gk_s[sl, :]
        G = gg_s[sl, :]
        beta = gb_s[sl, :]
        dm, _ = decay_parts(G)
        kq = _dot_nt(jnp.concatenate([k, q], axis=0), _block_diag(k, bd_mask))
        yield
        m = jnp.where(strict, kq[0:C] * dm * beta, 0.0)
        gqk_s[sl, :] = kq[C:2 * C] * dm
        p = _dot(m, _block_diag(m, bd_mask))
        yield
        tinv = eye_f - m
        n = 2
        while 2 * n < C:
            both = _dot(jnp.concatenate([p, tinv], axis=0), _block_diag(p, bd_mask))
            yield
            p = both[0:C]
            tinv = tinv + both[C:2 * C]
            n *= 2
        tinv = tinv + _dot(tinv, _block_diag(p, bd_mask))
        yield
        kb = k * beta * jnp.exp(G)
        rhs = jnp.concatenate([_block_diag(gv_s[sl, :] * beta, bd_mask), _block_diag(kb, bd_mask)], axis=1)
        sol = _dot(tinv, rhs)
        gsv_s[sl, :] = sol[:, 0:256]
        gsk_s[sl, :] = sol[:, 256:512]
        yield

    def ssd_a(c):
        sl = pl.ds(c * C, C)
        bm = sb_s[sl, :]
        dm, _ = decay_parts(sg_s[sl, :])
        cb = _dot_nt(sc_s[sl, :], _block_diag(bm, grp_row_mask))
        yield
        o_s[sl, 512:768] = _dot(cb * dm, _block_diag(sv_s[sl, :], bd_mask))
        yield

    def ret_a(c):
        sl = pl.ds(c * C, C)
        dm, _ = decay_parts(rg_s[sl, :])
        sc = _dot_nt(rq_s[sl, :], _block_diag(rk_s[sl, :], bd_mask)) * dm
        yield
        o_s[sl, 768:1024] = _dot(sc, _block_diag(rv_s[sl, :], bd_mask))
        yield

    def hgrn_b(c):
        sl = pl.ds(c * C, C)
        G = hg_s[sl, :]
        g_last = G[C - 1:C, :]
        st = st_s[0]
        o_s[sl, 0:256] += _dot_nt(hq_s[sl, :] * jnp.exp(G), st)
        yield
        kt = hk_s[sl, :] * jnp.exp(g_last - G)
        st_s[0] = st * jnp.exp(g_last) + jnp.where(sq_mask, _dot_tn(hv_s[sl, :], kt), 0.0)
        yield

    def gdn_b(c):
        sl = pl.ds(c * C, C)
        G = gg_s[sl, :]
        g_last = G[C - 1:C, :]
        st = st_s[1]
        tmp = _dot(jnp.concatenate([gsk_s[sl, :], gq_s[sl, :] * jnp.exp(G)], axis=0), st)
        yield
        u = gsv_s[sl, :] - tmp[0:C]
        kt = gk_s[sl, :] * jnp.exp(g_last - G)
        st_s[1] = st * jnp.exp(g_last) + jnp.where(sq_mask, _dot_tn(kt, u), 0.0)
        yield
        o_s[sl, 256:512] = tmp[C:2 * C] + _dot(gqk_s[sl, :], _block_diag(u, bd_mask))
        yield

    def ssd_b(c):
        sl = pl.ds(c * C, C)
        G = sg_s[sl, :]
        g_last = G[C - 1:C, :]
        st = st_s[2]
        o_s[sl, 512:768] += jnp.exp(G) * _dot(sc_s[sl, :], st)
        yield
        vt = sv_s[sl, :] * jnp.exp(g_last - G)
        st_s[2] = st * jnp.exp(g_last) + jnp.where(grp_mask, _dot_tn(sb_s[sl, :], vt), 0.0)
        yield

    def ret_b(c):
        sl = pl.ds(c * C, C)
        G = rg_s[sl, :]
        g_last = G[C - 1:C, :]
        st = st_s[3]
        o_s[sl, 768:1024] += jnp.exp(G) * _dot(rq_s[sl, :], st)
        yield
        vt = rv_s[sl, :] * jnp.exp(g_last - G)
        st_s[3] = st * jnp.exp(g_last) + jnp.where(sq_mask, _dot_tn(rk_s[sl, :], vt), 0.0)
        yield

    def post(c):
        sl = pl.ds(c * C, C)
        y_s[sl, :] = _post_mix(o_s[sl, 0:256], o_s[sl, 256:512], o_s[sl, 512:768], o_s[sl, 768:1024],
                               proj_s[sl, A_Z:A_Z + GROUP_W], proj_s[sl, B_Z:B_Z + GROUP_W],
                               proj_s[sl, C_Z:C_Z + GROUP_W], proj_s[sl, D_Z:D_Z + GROUP_W],
                               sx_s[sl, :], prm_ref, ones64, ones128).astype(BF16)
        yield

    chunks = range(n_chunks)
    every = lambda *gens: itertools.chain.from_iterable(gens)
    z_cols = [(z, z + GROUP_W) for z in (A_Z, B_Z, C_Z, D_Z)]
    _interleave([every(project((B_QKV, B_Z), (SMALL, NP)), gdn_prep())], project((A_Q, A_Z)), 1)
    _interleave([gdn_a(c) for c in chunks],
                every(hgrn_prep(), project((C_XBC, C_Z)), *[hgrn_a(c) for c in chunks],
                      ssd_prep(), project((D_Q, D_Z)), *[ssd_a(c) for c in chunks],
                      ret_prep(), project(*z_cols), *[ret_a(c) for c in chunks]),
                FILL_PER_ROUND)

    @pl.when(jnp.max(hspan_s[0:1, :]) > HGRN_SAFE_SPAN)
    def _():
        hgrn_level_exps()
        lax.fori_loop(0, n_chunks, hgrn_a_any_decay, 0)

    for c in chunks:
        done = [post(c - 1)] if c else []
        _interleave([gdn_b(c)], every(hgrn_b(c), ssd_b(c), ret_b(c), *done), 2)
    _interleave([post(n_chunks - 1)], (), 0)

    out = h_ref[0] + jnp.dot(y_s[...], wout_ref[0], preferred_element_type=F32)
    if final:
        out = _rms(out) * _prow(prm_ref, R_FINAL, D_MODEL)
    hout_ref[0] = out

    @pl.when(t == nt - 1)
    def _():
        sout_ref[0] = st_s[...]
        ctxout_ref[0, 0] = cbuf_g[valid:valid + 8, :]
        ctxout_ref[0, 1] = cbuf_s[valid:valid + 8, :]


def _prompt_layer(layer, h, cos, sin, win, wout, prm, expand, s0, ctx0, *, tile, valid, final):
    bsz, seq, _ = h.shape
    nt = seq // tile
    kern = functools.partial(_prompt_kernel, layer, tile, valid, final)
    const2 = lambda b, t: (0, 0)
    slab = lambda: pltpu.VMEM((tile, GROUP_W), F32)
    return pl.pallas_call(
        kern,
        grid=(bsz, nt),
        in_specs=[
            pl.BlockSpec((1, tile, D_MODEL), lambda b, t: (b, t, 0)),
            pl.BlockSpec((tile, GROUP_W), lambda b, t: (t, 0)),
            pl.BlockSpec((tile, GROUP_W), lambda b, t: (t, 0)),
            pl.BlockSpec((1, D_MODEL, NP), lambda b, t: (layer, 0, 0)),
            pl.BlockSpec((1, D_MODEL, D_MODEL), lambda b, t: (layer, 0, 0)),
            pl.BlockSpec((1, N_PRM, D_MODEL), lambda b, t: (layer, 0, 0)),
            pl.BlockSpec(expand.shape, const2),
            pl.BlockSpec((4, GROUP_W, GROUP_W), lambda b, t: (0, 0, 0)),
            pl.BlockSpec((2, 8, CONV_DIM), lambda b, t: (0, 0, 0)),
        ],
        out_specs=[
            pl.BlockSpec((1, tile, D_MODEL), lambda b, t: (b, t, 0)),
            pl.BlockSpec((1, 4, GROUP_W, GROUP_W), lambda b, t: (b, 0, 0, 0)),
            pl.BlockSpec((1, 2, 8, CONV_DIM), lambda b, t: (b, 0, 0, 0)),
        ],
        out_shape=[
            jax.ShapeDtypeStruct((bsz, seq, D_MODEL), F32),
            jax.ShapeDtypeStruct((bsz, 4, GROUP_W, GROUP_W), F32),
            jax.ShapeDtypeStruct((bsz, 2, 8, CONV_DIM), F32),
        ],
        scratch_shapes=[
            pltpu.VMEM((tile, NP), F32),
            pltpu.VMEM((tile + 8, CONV_DIM), F32),
            pltpu.VMEM((tile + 8, CONV_DIM), F32),
            pltpu.VMEM((4, GROUP_W, GROUP_W), F32),
            pltpu.VMEM((6, tile, GROUP_W), F32),
        ] + [slab() for _ in range(22)] + [pltpu.VMEM((tile, D_MODEL), F32), pltpu.VMEM((tile, D_MODEL), BF16),
                                             pltpu.VMEM((tile, D_MODEL), BF16), pltpu.VMEM((8, GROUP_W), F32)],
        compiler_params=pltpu.CompilerParams(
            dimension_semantics=("arbitrary", "arbitrary"), vmem_limit_bytes=VMEM_LIMIT),
        name=f"prompt_layer_t{tile}",
    )(h, cos, sin, win, wout, prm, expand, s0, ctx0)


(VT_HQ, VT_HK, VT_HF, VT_HV, VT_GQ, VT_GK, VT_GV, VT_GDEC, VT_GBETA,
 VT_SV, VT_SB, VT_SC, VT_SDEC, VT_RQ, VT_RK, VT_RV, VT_RDEC) = [GROUP_W * i for i in range(17)]
N_VT = 17 * GROUP_W
DEC_VBLOCK = 32


def _dec_pre_kernel(layer, hs_ref, cos_ref, sin_ref, win_ref, prm_ref, exp_ref, gctx_ref, sctx_ref,
                    vt_ref, post_ref, ug_ref, us_ref):
    x = hs_ref[...]
    nb = x.shape[0]
    hn = _rms(x) * _prow(prm_ref, R_NORM, D_MODEL)
    proj = jnp.dot(hn.astype(BF16), win_ref[0], preferred_element_type=F32)

    ri = _iota((GROUP_W, GROUP_W), 0)
    ci = _iota((GROUP_W, GROUP_W), 1)
    ones64 = jnp.where((ri >> 6) == (ci >> 6), 1.0, 0.0).astype(BF16)

    def put(off, a):
        vt_ref[off:off + GROUP_W, :] = jnp.transpose(a)

    lb = _hgrn_lower_bound(_prow(prm_ref, R_LB, n=DEPTH), layer)
    q, k, f, v = _hgrn_inputs(proj, lb)
    put(VT_HQ, q)
    put(VT_HK, k)
    put(VT_HF, f)
    put(VT_HV, v)

    log_decay, vals = _compact_gates(proj[:, SMALL:SMALL + 128], prm_ref)
    decay = jnp.exp(_dot_exact_rhs(log_decay, exp_ref[:, 0:512]))
    vals = _dot_exact_rhs(vals, exp_ref[:, 512:1024])

    u = proj[:, B_QKV:B_QKV + CONV_DIM]
    ug_ref[...] = u
    cw = _prow(prm_ref, R_GCONV, CONV_DIM, CONV_W)
    conv = cw[3:4] * u + cw[2:3] * gctx_ref[0, 2] + cw[1:2] * gctx_ref[0, 1] + cw[0:1] * gctx_ref[0, 0]
    act = _silu(conv)
    qn, kn = _l2n_pair(act[:, 0:256], act[:, 256:512], ones64)
    put(VT_GQ, qn * QK_SCALE)
    put(VT_GK, kn)
    put(VT_GV, act[:, 512:768])
    put(VT_GDEC, decay[:, 0:256])
    put(VT_GBETA, vals[:, 0:256])

    u = proj[:, C_XBC:C_XBC + CONV_DIM]
    us_ref[...] = u
    cw = _prow(prm_ref, R_SCONV, CONV_DIM, CONV_W)
    conv = (cw[3:4] * u + cw[2:3] * sctx_ref[0, 2] + cw[1:2] * sctx_ref[0, 1] + cw[0:1] * sctx_ref[0, 0]
            + _prow(prm_ref, R_SCONVB, CONV_DIM))
    act = _silu(conv)
    post_ref[:, POST_SX:POST_SX + 256] = act[:, 0:256]
    put(VT_SV, act[:, 0:256] * vals[:, 256:512])
    put(VT_SB, act[:, 256:512])
    put(VT_SC, act[:, 512:768])
    put(VT_SDEC, decay[:, 256:512])

    cos = cos_ref[...]
    sin = sin_ref[...]
    put(VT_RQ, _rotary(proj[:, D_Q:D_Q + GROUP_W], cos, sin))
    put(VT_RK, _rotary(proj[:, D_K:D_K + GROUP_W], cos, sin) * QK_SCALE)
    put(VT_RV, proj[:, D_V:D_V + GROUP_W])
    put(VT_RDEC, jnp.broadcast_to(jnp.exp(_prow(prm_ref, R_RLOGG)), (nb, GROUP_W)))

    post_ref[:, POST_AZ:POST_AZ + 256] = proj[:, A_Z:A_Z + GROUP_W]
    post_ref[:, POST_BZ:POST_BZ + 256] = proj[:, B_Z:B_Z + GROUP_W]
    post_ref[:, POST_CZ:POST_CZ + 256] = proj[:, C_Z:C_Z + GROUP_W]
    post_ref[:, POST_DZ:POST_DZ + 256] = proj[:, D_Z:D_Z + GROUP_W]


def _dec_pre(layer, hs, cos, sin, win, prm, expand, gctx, sctx):
    nb = hs.shape[0]
    full = lambda a: pl.BlockSpec(a.shape, lambda i: (0,) * a.ndim)
    shapes = [(N_VT, nb), (nb, N_POST), (nb, CONV_DIM), (nb, CONV_DIM)]
    return pl.pallas_call(
        functools.partial(_dec_pre_kernel, layer),
        grid=(1,),
        in_specs=[
            full(hs), full(cos), full(sin),
            pl.BlockSpec((1, D_MODEL, NP), lambda i: (layer, 0, 0)),
            pl.BlockSpec((1, N_PRM, D_MODEL), lambda i: (layer, 0, 0)),
            full(expand),
            pl.BlockSpec((1, 3, nb, CONV_DIM), lambda i: (layer, 0, 0, 0)),
            pl.BlockSpec((1, 3, nb, CONV_DIM), lambda i: (layer, 0, 0, 0)),
        ],
        out_specs=[pl.BlockSpec(s, lambda i: (0, 0)) for s in shapes],
        out_shape=[jax.ShapeDtypeStruct(s, F32) for s in shapes],
        compiler_params=pltpu.CompilerParams(dimension_semantics=("arbitrary",), vmem_limit_bytes=VMEM_LIMIT),
        name="decode_pre",
    )(hs, cos, sin, win, prm, expand, gctx, sctx)


def _dec_state_kernel(vt_ref, hg_ref, gd_ref, sd_ref, rt_ref, hg_acc, gd_acc, rt_acc,
                      o_ref, hg_out, gd_out, sd_out, rt_out):
    del hg_acc, gd_acc, rt_acc
    h = pl.program_id(0)
    j = pl.program_id(1)
    vb = DEC_VBLOCK
    nb = vt_ref.shape[1]
    head = h * HEAD_DIM

    def row(off, i):
        return vt_ref[pl.ds(off + i, 1), :]

    def slab(off):
        return vt_ref[pl.ds(pl.multiple_of(off + head + j * vb, 8), vb), :]

    zero = jnp.zeros((vb, nb), F32)

    v = slab(VT_HV)

    def hgrn_step(k, o):
        s = row(VT_HF + head, k) * hg_ref[0, 0, k] + row(VT_HK + head, k) * v
        hg_out[0, 0, k] = s
        return o + row(VT_HQ + head, k) * s

    o_ref[0, 0] = lax.fori_loop(0, HEAD_DIM, hgrn_step, zero, unroll=4)

    v = slab(VT_GV)
    dec = row(VT_GDEC + head, 0)
    ks = lax.fori_loop(0, HEAD_DIM, lambda k, a: a + row(VT_GK + head, k) * gd_ref[0, 0, k], zero, unroll=4)
    u = row(VT_GBETA + head, 0) * (v - ks * dec)

    def gdn_step(k, o):
        s = gd_ref[0, 0, k] * dec + row(VT_GK + head, k) * u
        gd_out[0, 0, k] = s
        return o + row(VT_GQ + head, k) * s

    o_ref[1, 0] = lax.fori_loop(0, HEAD_DIM, gdn_step, zero, unroll=4)

    v = slab(VT_SV)
    dec = row(VT_SDEC + head, 0)
    grp = (h >> 1) * 128

    def ssd_step(n, o):
        s = sd_ref[0, 0, n] * dec + row(VT_SB + grp, n) * v
        sd_out[0, 0, n] = s
        return o + row(VT_SC + grp, n) * s

    o_ref[2, 0] = lax.fori_loop(0, 128, ssd_step, zero, unroll=4)

    v = slab(VT_RV)
    dec = row(VT_RDEC + head, 0)

    def ret_step(k, o):
        s = rt_ref[0, 0, k] * dec + row(VT_RK + head, k) * v
        rt_out[0, 0, k] = s
        return o + row(VT_RQ + head, k) * s

    o_ref[3, 0] = lax.fori_loop(0, HEAD_DIM, ret_step, zero, unroll=4)


def _dec_state(layer, vt, states, accs):
    nb = vt.shape[1]
    vb = DEC_VBLOCK
    blk = lambda a: pl.BlockSpec((1, 1, a.shape[2], vb, nb), lambda h, j: (layer, h, 0, j, 0))
    return pl.pallas_call(
        _dec_state_kernel,
        grid=(N_HEADS, HEAD_DIM // vb),
        in_specs=[pl.BlockSpec(vt.shape, lambda h, j: (0, 0))] + [blk(a) for a in states]
        + [pl.BlockSpec(memory_space=pl.ANY)] * 3,
        out_specs=[pl.BlockSpec((4, 1, vb, nb), lambda h, j: (0, h, j, 0))] + [blk(a) for a in states],
        out_shape=[jax.ShapeDtypeStruct((4, N_HEADS, HEAD_DIM, nb), F32)]
        + [jax.ShapeDtypeStruct(a.shape, F32) for a in states],
        input_output_aliases={3: 3, 5: 1, 6: 2, 7: 4},
        compiler_params=pltpu.CompilerParams(dimension_semantics=("arbitrary", "arbitrary"),
                                             vmem_limit_bytes=VMEM_LIMIT),
        name="decode_state",
    )(vt, *states, *accs)


def _dec_post_kernel(final, hs_ref, ot_ref, post_ref, wout_ref, prm_ref, out_ref):
    ri = _iota((GROUP_W, GROUP_W), 0)
    ci = _iota((GROUP_W, GROUP_W), 1)
    ones64 = jnp.where((ri >> 6) == (ci >> 6), 1.0, 0.0).astype(BF16)
    ones128 = jnp.where((ri >> 7) == (ci >> 7), 1.0, 0.0).astype(BF16)
    gate = lambda off: post_ref[:, off:off + GROUP_W]
    o = [jnp.transpose(ot_ref[m * GROUP_W:(m + 1) * GROUP_W, :]) for m in range(4)]
    y = _post_mix(o[0], o[1], o[2], o[3],
                  gate(POST_AZ), gate(POST_BZ), gate(POST_CZ), gate(POST_DZ), gate(POST_SX),
                  prm_ref, ones64, ones128)
    out = hs_ref[...] + jnp.dot(y.astype(BF16), wout_ref[0], preferred_element_type=F32)
    if final:
        out = _rms(out) * _prow(prm_ref, R_FINAL, D_MODEL)
    out_ref[...] = out


def _dec_post(layer, hs, ot, post, wout, prm, final):
    nb = hs.shape[0]
    full = lambda a: pl.BlockSpec(a.shape, lambda i: (0,) * a.ndim)
    return pl.pallas_call(
        functools.partial(_dec_post_kernel, final),
        grid=(1,),
        in_specs=[full(hs), full(ot), full(post),
                  pl.BlockSpec((1, D_MODEL, D_MODEL), lambda i: (layer, 0, 0)),
                  pl.BlockSpec((1, N_PRM, D_MODEL), lambda i: (layer, 0, 0))],
        out_specs=pl.BlockSpec((nb, D_MODEL), lambda i: (0, 0)),
        out_shape=jax.ShapeDtypeStruct((nb, D_MODEL), F32),
        compiler_params=pltpu.CompilerParams(dimension_semantics=("arbitrary",), vmem_limit_bytes=VMEM_LIMIT),
        name="decode_post",
    )(hs, ot, post, wout, prm)


POST_AZ, POST_BZ, POST_CZ, POST_DZ, POST_SX = [GROUP_W * i for i in range(5)]
N_POST = 5 * GROUP_W


def _expand_matrix():
    e = np.zeros((128, 4 * GROUP_W), np.float32)
    for block, first_lane in enumerate((0, 2 * N_HEADS, N_HEADS, 2 * N_HEADS)):
        for h in range(N_HEADS):
            lo = block * GROUP_W + h * HEAD_DIM
            e[first_lane + h, lo:lo + HEAD_DIM] = 1.0
    return jnp.asarray(e, BF16)


def _relayout_w_in(w_in):
    sizes = (256, 256, 256, 256, 768, 256, 4, 4, 768, 256, 4, 256, 256, 256, 256)
    offs = np.concatenate([[0], np.cumsum(sizes)])
    w_t = jnp.transpose(w_in, (2, 0, 1))
    seg = lambda i: w_t[offs[i]:offs[i + 1]]
    pad = jnp.zeros((128 - 12,) + w_t.shape[1:], w_in.dtype)
    rows = [seg(0), seg(1), seg(2), seg(3), seg(4), seg(5), seg(8), seg(9), seg(11), seg(12), seg(13), seg(14),
            seg(6), seg(7), seg(10), pad]
    return jnp.transpose(jnp.concatenate(rows, axis=0), (1, 2, 0)).astype(BF16)


def _pack_params(norm_w, hgrn_lb_logits, hgrn_norm_w, gdn_conv_w, gdn_a_log, gdn_dt_bias, gdn_norm_w,
                 ssd_conv_w, ssd_conv_b, ssd_a_log, ssd_dt_bias, ssd_d, ssd_norm_w, ret_norm_w, ret_norm_b,
                 final_norm_w):
    depth = norm_w.shape[0]
    prm = jnp.zeros((depth, N_PRM, D_MODEL), F32)

    def put(p, r, a):
        a = a.astype(F32)
        if a.ndim == 2:
            a = a[:, None, :]
        return p.at[:, r:r + a.shape[1], 0:a.shape[2]].set(a)

    rep = lambda a: jnp.repeat(a, HEAD_DIM, axis=-1)
    prm = put(prm, R_NORM, norm_w)
    prm = put(prm, R_LB, jnp.broadcast_to(hgrn_lb_logits[None], (depth,) + hgrn_lb_logits.shape))
    prm = put(prm, R_HNORM, hgrn_norm_w)
    prm = put(prm, R_GCONV, gdn_conv_w)
    prm = put(prm, R_GNORM, gdn_norm_w)
    prm = put(prm, R_SCONV, ssd_conv_w)
    prm = put(prm, R_SCONVB, ssd_conv_b)
    prm = put(prm, R_SD, rep(ssd_d))
    gap = jnp.zeros((depth, N_HEADS), F32)
    prm = put(prm, R_CBIAS, jnp.concatenate([gdn_dt_bias.astype(F32), gap, ssd_dt_bias.astype(F32)], axis=-1))
    prm = put(prm, R_CALOG, jnp.concatenate([gdn_a_log.astype(F32), gap, ssd_a_log.astype(F32)], axis=-1))
    prm = put(prm, R_SNORM, ssd_norm_w)
    prm = put(prm, R_RNORM, ret_norm_w)
    prm = put(prm, R_RNORMB, ret_norm_b)
    ret_logg = jnp.log1p(-jnp.exp2(-5.0 - jnp.arange(N_HEADS, dtype=F32)))
    prm = put(prm, R_RLOGG, jnp.broadcast_to(rep(ret_logg)[None], (depth, GROUP_W)))
    prm = put(prm, R_FINAL, jnp.broadcast_to(final_norm_w[None], (depth, D_MODEL)))
    return prm


def _rope_tables(pos):
    half = HEAD_DIM // 2
    inv_freq = 1.0 / (ROPE_BASE ** jnp.linspace(0.0, 1.0, half, dtype=F32))
    ang = pos[:, None] * inv_freq[None, :]
    cos, sin = jnp.cos(ang), jnp.sin(ang)
    cos_h = jnp.concatenate([cos, cos], axis=-1)
    sin_h = jnp.concatenate([-sin, sin], axis=-1)
    return jnp.tile(cos_h, (1, N_HEADS)), jnp.tile(sin_h, (1, N_HEADS))


def _diag_blocks(s, rows, cols):
    return jnp.stack([s[:, (h * rows):(h + 1) * rows, h * cols:(h + 1) * cols] for h in range(N_HEADS)], axis=1)


def _unpack_states(sout):
    hgrn = jnp.swapaxes(_diag_blocks(sout[:, 0], HEAD_DIM, HEAD_DIM), -1, -2)
    gdn = _diag_blocks(sout[:, 1], HEAD_DIM, HEAD_DIM)
    ssd = jnp.stack([sout[:, 2, (h // 2) * 128:(h // 2 + 1) * 128, h * HEAD_DIM:(h + 1) * HEAD_DIM]
                     for h in range(N_HEADS)], axis=1)
    ret = _diag_blocks(sout[:, 3], HEAD_DIM, HEAD_DIM)
    return hgrn, gdn, ssd, ret


def kernel(x_prompt, x_sample, state_hgrn, state_gdn, state_gdn_conv, state_ssd, state_ssd_conv, state_ret,
           meta_tokens, norm_w, w_in, hgrn_lb_logits, hgrn_norm_w, gdn_conv_w, gdn_a_log, gdn_dt_bias, gdn_norm_w,
           ssd_conv_w, ssd_conv_b, ssd_a_log, ssd_dt_bias, ssd_d, ssd_norm_w, ret_norm_w, ret_norm_b,
           w_out, final_norm_w):
    depth = w_in.shape[0]
    bsz, seq, _ = x_prompt.shape
    nb = x_sample.shape[0]
    tile = PROMPT_TILE if seq % PROMPT_TILE == 0 else CHUNK

    win = _relayout_w_in(w_in)
    wout = w_out.astype(BF16)
    prm = _pack_params(norm_w, hgrn_lb_logits, hgrn_norm_w, gdn_conv_w, gdn_a_log, gdn_dt_bias, gdn_norm_w,
                       ssd_conv_w, ssd_conv_b, ssd_a_log, ssd_dt_bias, ssd_d, ssd_norm_w, ret_norm_w, ret_norm_b,
                       final_norm_w)
    expand = _expand_matrix()

    cos_m, sin_m = _rope_tables(jnp.arange(CHUNK, dtype=F32))
    cos_p, sin_p = _rope_tables(N_META + jnp.arange(seq, dtype=F32))
    cos_s, sin_s = _rope_tables(PAST_LEN + jnp.arange(1, dtype=F32))

    hm = jnp.zeros((1, CHUNK, D_MODEL), F32).at[0, :N_META].set(meta_tokens.astype(F32))
    hp = x_prompt.astype(F32)
    hs = x_sample.astype(F32)[:, 0, :]
    zero_s = jnp.zeros((4, GROUP_W, GROUP_W), F32)
    zero_ctx = jnp.zeros((2, 8, CONV_DIM), F32)

    gctx = jnp.swapaxes(state_gdn_conv.astype(F32), 1, 2)
    sctx = jnp.swapaxes(state_ssd_conv.astype(F32), 1, 2)

    seq_minor = lambda a: jnp.transpose(a.astype(F32), (0, 2, 3, 4, 1))
    dec_states = [seq_minor(a) for a in (state_hgrn, state_gdn, state_ssd, state_ret)]
    dec_new = [jnp.zeros_like(dec_states[0]), jnp.zeros_like(dec_states[1]), dec_states[2],
               jnp.zeros_like(dec_states[3])]

    p_states, p_ctx = [], []
    s_gctx, s_sctx = [], []
    for l in range(depth):
        final = l == depth - 1
        hm, sm, cm = _prompt_layer(l, hm, cos_m, sin_m, win, wout, prm, expand, zero_s, zero_ctx,
                                   tile=CHUNK, valid=N_META, final=False)
        hp, sp, cp = _prompt_layer(l, hp, cos_p, sin_p, win, wout, prm, expand, sm[0], cm[0],
                                   tile=tile, valid=tile, final=final)
        p_states.append(sp)
        p_ctx.append(cp)

        vt, post, ug, us = _dec_pre(l, hs, cos_s, sin_s, win, prm, expand, gctx, sctx)
        ot, *dec_new = _dec_state(l, vt, [dec_states[0], dec_states[1], dec_new[2], dec_states[3]],
                                  [dec_new[0], dec_new[1], dec_new[3]])
        hs = _dec_post(l, hs, ot.reshape(4 * GROUP_W, nb), post, wout, prm, final)
        s_gctx.append(jnp.concatenate([state_gdn_conv[l, :, 1:].astype(F32), ug[:, None, :]], axis=1))
        s_sctx.append(jnp.concatenate([state_ssd_conv[l, :, 1:].astype(F32), us[:, None, :]], axis=1))

    sp = jnp.stack(p_states)
    cp = jnp.stack(p_ctx)
    ph, pg, ps, pr = _unpack_states(sp.reshape((depth * bsz,) + sp.shape[2:]))
    unb = lambda a: a.reshape((depth, bsz) + a.shape[1:])
    seq_major = lambda a: jnp.transpose(a, (0, 4, 1, 2, 3))
    return (hp.astype(x_prompt.dtype), hs[:, None, :].astype(x_sample.dtype),
            unb(ph).astype(state_hgrn.dtype), unb(pg).astype(state_gdn.dtype),
            cp[:, :, 0, 8 - (CONV_W - 1):].astype(state_gdn_conv.dtype),
            unb(ps).astype(state_ssd.dtype),
            cp[:, :, 1, 8 - (CONV_W - 1):].astype(state_ssd_conv.dtype),
            unb(pr).astype(state_ret.dtype),
            seq_major(dec_new[0]).astype(state_hgrn.dtype), seq_major(dec_new[1]).astype(state_gdn.dtype),
            jnp.stack(s_gctx).astype(state_gdn_conv.dtype), seq_major(dec_new[2]).astype(state_ssd.dtype),
            jnp.stack(s_sctx).astype(state_ssd_conv.dtype), seq_major(dec_new[3]).astype(state_ret.dtype))
```

```python
import functools
import itertools
import math

import numpy as np
import jax
import jax.numpy as jnp
from jax import lax
from jax.experimental import pallas as pl
from jax.experimental.pallas import tpu as pltpu

F32 = jnp.float32
BF16 = jnp.bfloat16

D_MODEL = 1024
GROUP_W = 256
HEAD_DIM = 64
N_HEADS = 4
CHUNK = 64
CONV_W = 4
CONV_DIM = 768
N_META = 16
PAST_LEN = 16384
ROPE_BASE = 10000.0
EPS = 1e-6
TINY = 1e-30
QK_SCALE = HEAD_DIM ** -0.5
DEPTH = 4

A_Q, A_F, A_I, A_Z = 0, 256, 512, 768
B_QKV, B_Z = 1024, 1792
C_XBC, C_Z = 2048, 2816
D_Q, D_K, D_V, D_Z = 3072, 3328, 3584, 3840
SMALL = 4096
NP = 4224

R_NORM, R_LB, R_HNORM = 0, 1, 5
R_GCONV, R_GNORM = 6, 12
R_SCONV, R_SCONVB, R_SD, R_SNORM = 13, 17, 20, 21
R_RNORM, R_RNORMB, R_RLOGG, R_FINAL = 22, 23, 24, 25
R_CBIAS, R_CALOG = 26, 27
N_PRM = 32

VMEM_LIMIT = 56 * 1024 * 1024
PROMPT_TILE = 256
DEC_BLOCK = 8
HGRN_SAFE_SPAN = 80.0
FILL_PER_ROUND = 4
_DONE = object()


def _interleave(leads, fillers, fill_per_round):
    leads = list(leads)
    fillers = iter(fillers)
    fill_live = fill_per_round > 0
    while leads or fill_live:
        leads = [g for g in leads if next(g, _DONE) is not _DONE]
        for _ in range(fill_per_round):
            if fill_live and next(fillers, _DONE) is _DONE:
                fill_live = False


def _dot(a, b):
    return jnp.dot(a.astype(BF16), b.astype(BF16), preferred_element_type=F32)


def _dot_nt(a, b):
    return lax.dot_general(a.astype(BF16), b.astype(BF16), (((1,), (1,)), ((), ())),
                           preferred_element_type=F32)


def _dot_tn(a, b):
    return lax.dot_general(a.astype(BF16), b.astype(BF16), (((0,), (0,)), ((), ())),
                           preferred_element_type=F32)


def _split3(x):
    hi = x.astype(BF16)
    r1 = x - hi.astype(F32)
    mid = r1.astype(BF16)
    lo = (r1 - mid.astype(F32)).astype(BF16)
    return hi, mid, lo


def _dot_exact_rhs(x, m):
    hi, mid, lo = _split3(x)
    return (jnp.dot(hi, m, preferred_element_type=F32) + jnp.dot(mid, m, preferred_element_type=F32)
            + jnp.dot(lo, m, preferred_element_type=F32))


def _dot_exact_lhs(m, x):
    hi, mid, lo = _split3(x)
    return (jnp.dot(m, hi, preferred_element_type=F32) + jnp.dot(m, mid, preferred_element_type=F32)
            + jnp.dot(m, lo, preferred_element_type=F32))


def _sigmoid(x):
    return 1.0 / (1.0 + jnp.exp(-x))


def _silu(x):
    return x * _sigmoid(x)


def _softplus(x):
    return jnp.maximum(x, 0.0) + jnp.log(1.0 + jnp.exp(-jnp.abs(x)))


def _iota(shape, dim):
    return lax.broadcasted_iota(jnp.int32, shape, dim)


def _block_diag(x, mask01):
    xb = x.astype(BF16)
    return jnp.concatenate([xb] * N_HEADS, axis=0) * mask01


def _chunk_cumsum(x, row_in_chunk):
    sh = 1
    while sh < CHUNK:
        r = pltpu.roll(x, sh, axis=0)
        x = x + jnp.where(row_in_chunk >= sh, r, 0.0)
        sh *= 2
    return x


def _swap_halves(x):
    lane = _iota(x.shape, 1)
    fwd = pltpu.roll(x, GROUP_W - HEAD_DIM // 2, axis=1)
    bwd = pltpu.roll(x, HEAD_DIM // 2, axis=1)
    return jnp.where((lane & (HEAD_DIM - 1)) < HEAD_DIM // 2, fwd, bwd)


def _group_sums(xs, ones_bd, squares=()):
    parts, spans = [], []
    for i, x in enumerate(xs):
        hi = x.astype(BF16)
        terms = [hi] if i in squares else [hi, (x - hi.astype(F32)).astype(BF16)]
        spans.append((len(parts), len(terms)))
        parts += terms
    sums = jnp.dot(jnp.concatenate(parts, axis=0) if len(parts) > 1 else parts[0], ones_bd,
                   preferred_element_type=F32)
    n = xs[0].shape[0]
    out = []
    for first, count in spans:
        total = sums[first * n:(first + 1) * n]
        for j in range(1, count):
            total = total + sums[(first + j) * n:(first + j + 1) * n]
        out.append(total)
    return out


def _group_sum(x, ones_bd, square=False):
    return _group_sums([x], ones_bd, squares=(0,) if square else ())[0]


def _rms(x):
    return x * lax.rsqrt(jnp.mean(x * x, axis=-1, keepdims=True) + EPS)


def _hgrn_lower_bound(logits, layer):
    m = jnp.max(logits, axis=0, keepdims=True)
    e = jnp.exp(logits - m)
    w = e / jnp.sum(e, axis=0, keepdims=True)
    cum = jnp.sum(w[0:layer + 1], axis=0, keepdims=True) - w[0:1]
    return jnp.maximum(cum, 0.0)


def _prow(prm_ref, r, width=GROUP_W, n=1):
    return prm_ref[0, r:r + n, 0:width]


def _hgrn_inputs(proj, lb):
    aq = proj[:, A_Q:A_Q + GROUP_W]
    af = proj[:, A_F:A_F + GROUP_W]
    q = _silu(aq) * QK_SCALE
    k = (1.0 - lb) * _sigmoid(-af)
    f = jnp.maximum(lb + (1.0 - lb) * _sigmoid(af), TINY)
    return q, k, f, proj[:, A_I:A_I + GROUP_W]


def _compact_gates(c, prm_ref):
    sp = _softplus(c + _prow(prm_ref, R_CBIAS, 128))
    log_decay = -jnp.exp(_prow(prm_ref, R_CALOG, 128)) * sp
    lane = _iota(c.shape, 1)
    vals = jnp.where((lane >= N_HEADS) & (lane < 2 * N_HEADS), _sigmoid(c), sp)
    return log_decay, vals


def _l2n_pair(q, k, ones64):
    sq_q, sq_k = _group_sums([q * q, k * k], ones64, squares=(0, 1))
    return q * lax.rsqrt(sq_q + EPS), k * lax.rsqrt(sq_k + EPS)


def _rotary(x, cos, sin):
    return x * cos + _swap_halves(x) * sin


def _post_mix(o_a, o_b, o_c, o_d, z_a, z_b, z_c, z_d, x_ssd, prm_ref, ones64, ones128):
    inv64 = 1.0 / HEAD_DIM
    sq_a, sq_b, sum_d = _group_sums([o_a * o_a, o_b * o_b, o_d], ones64, squares=(0, 1))
    y_a = o_a * lax.rsqrt(sq_a * inv64 + EPS) * _prow(prm_ref, R_HNORM) * _silu(z_a)
    y_b = o_b * lax.rsqrt(sq_b * inv64 + EPS) * _prow(prm_ref, R_GNORM) * _silu(z_b)
    t_c = (o_c + _prow(prm_ref, R_SD) * x_ssd) * _silu(z_c)
    y_c = (t_c * lax.rsqrt(_group_sum(t_c * t_c, ones128, square=True) * (1.0 / 128.0) + EPS)
           * _prow(prm_ref, R_SNORM))
    xc = o_d - sum_d * inv64
    y_d = ((xc * lax.rsqrt(_group_sum(xc * xc, ones64, square=True) * inv64 + EPS) * _prow(prm_ref, R_RNORM)
            + _prow(prm_ref, R_RNORMB)) * _silu(z_d))
    return jnp.concatenate([y_a, y_b, y_c, y_d], axis=-1)


def _prompt_kernel(layer, tile, valid, final,
                   h_ref, cos_ref, sin_ref, win_ref, wout_ref, prm_ref, exp_ref, s0_ref, ctx0_ref,
                   hout_ref, sout_ref, ctxout_ref, hgo_ref, gdo_ref, sdo_ref, rto_ref,
                   proj_s, cbuf_g, cbuf_s, st_s, he_s,
                   hq_s, hk_s, hv_s, hg_s,
                   gq_s, gk_s, gv_s, gg_s, gb_s, gqk_s, gsv_s, gsk_s,
                   sx_s, sv_s, sb_s, sc_s, sg_s, sdt_s,
                   rq_s, rk_s, rv_s, rg_s, o_s, y_s, hn_s, hspan_s):
    t = pl.program_id(1)
    nt = pl.num_programs(1)
    n_chunks = tile // CHUNK
    C = CHUNK

    @pl.when(t == 0)
    def _():
        st_s[...] = s0_ref[...]
        cbuf_g[0:8, :] = ctx0_ref[0]
        cbuf_s[0:8, :] = ctx0_ref[1]

    hn_s[...] = (_rms(h_ref[0]) * _prow(prm_ref, R_NORM, D_MODEL)).astype(BF16)

    def project(*col_ranges):
        for lo, hi in col_ranges:
            proj_s[:, lo:hi] = jnp.dot(hn_s[...], win_ref[0, :, lo:hi], preferred_element_type=F32)
        yield

    row = _iota((tile, GROUP_W), 0)
    row_in_chunk = row & (C - 1)
    if valid < tile:
        live = row < valid
        keep = lambda a: jnp.where(live[:, 0:a.shape[1]], a, 0.0)
    else:
        keep = lambda a: a

    ri = _iota((GROUP_W, GROUP_W), 0)
    ci = _iota((GROUP_W, GROUP_W), 1)
    ones64 = jnp.where((ri >> 6) == (ci >> 6), 1.0, 0.0).astype(BF16)
    ones128 = jnp.where((ri >> 7) == (ci >> 7), 1.0, 0.0).astype(BF16)

    def conv_taps(cbuf, u, r_w):
        cbuf[8:8 + tile, :] = u
        cw = _prow(prm_ref, r_w, CONV_DIM, CONV_W)
        ext = cbuf[...]
        acc = cw[0:1] * ext
        for tap in range(1, CONV_W):
            acc = pltpu.roll(acc, 1, axis=0) + cw[tap:tap + 1] * ext
        cbuf[0:8, :] = cbuf[tile:tile + 8, :]
        return acc[8:8 + tile, :]

    def gdn_prep():
        log_decay, vals = _compact_gates(proj_s[:, SMALL:SMALL + 128], prm_ref)
        cum = _chunk_cumsum(keep(log_decay), row_in_chunk[:, 0:128])
        cum = _dot_exact_rhs(cum, exp_ref[:, 0:512])
        gg_s[...] = cum[:, 0:256]
        sg_s[...] = cum[:, 256:512]
        vals = _dot_exact_rhs(keep(vals), exp_ref[:, 512:1024])
        gb_s[...] = vals[:, 0:256]
        sdt_s[...] = vals[:, 256:512]
        yield
        act = _silu(conv_taps(cbuf_g, proj_s[:, B_QKV:B_QKV + CONV_DIM], R_GCONV))
        qn, kn = _l2n_pair(act[:, 0:256], act[:, 256:512], ones64)
        gq_s[...] = qn * QK_SCALE
        gk_s[...] = kn
        gv_s[...] = act[:, 512:768]
        yield

    def hgrn_prep():
        lb = _hgrn_lower_bound(_prow(prm_ref, R_LB, n=DEPTH), layer)
        q, k, f, v = _hgrn_inputs(proj_s[...], lb)
        hq_s[...] = q
        hk_s[...] = k
        hv_s[...] = keep(v)
        yield
        G = _chunk_cumsum(keep(jnp.log(f)), row_in_chunk)
        hg_s[...] = G
        span = jnp.zeros((1, GROUP_W), F32)
        for c in range(n_chunks):
            mid = G[c * C + C // 2 - 1:c * C + C // 2, :]
            span = jnp.maximum(span, jnp.maximum(G[c * C:c * C + 1, :] - mid, mid - G[c * C + C - 1:c * C + C, :]))
        hspan_s[...] = jnp.broadcast_to(span, hspan_s.shape)
        yield

    def hgrn_level_exps():
        G = hg_s[...]
        last = G
        s = 1
        lvl = 5
        while s < C:
            upper = (row & s) != 0
            ref_row = jnp.where(upper, pltpu.roll(last, s, axis=0), last)
            he_s[lvl] = jnp.exp(jnp.minimum(jnp.where(upper, G - ref_row, ref_row - G), 0.0))
            last = jnp.where(upper, last, pltpu.roll(last, tile - s, axis=0))
            s *= 2
            lvl -= 1

    def ssd_prep():
        act = _silu(conv_taps(cbuf_s, proj_s[:, C_XBC:C_XBC + CONV_DIM], R_SCONV)
                    + _prow(prm_ref, R_SCONVB, CONV_DIM))
        dt = sdt_s[...]
        sx_s[...] = act[:, 0:256]
        sv_s[...] = act[:, 0:256] * dt
        sb_s[...] = act[:, 256:512]
        sc_s[...] = act[:, 512:768]
        yield

    def ret_prep():
        cos = cos_ref[...]
        sin = sin_ref[...]
        rq_s[...] = _rotary(proj_s[:, D_Q:D_Q + GROUP_W], cos, sin)
        rk_s[...] = _rotary(proj_s[:, D_K:D_K + GROUP_W], cos, sin) * QK_SCALE
        rv_s[...] = keep(proj_s[:, D_V:D_V + GROUP_W])
        live_rows = jnp.clip(jnp.minimum(row_in_chunk + 1, valid - (row - row_in_chunk)), 0, C)
        rg_s[...] = live_rows.astype(F32) * _prow(prm_ref, R_RLOGG)
        yield

    ii = _iota((C, GROUP_W), 0)
    jj = _iota((C, GROUP_W), 1) & (C - 1)
    tri = jj <= ii
    strict = jj < ii
    eye = jj == ii
    eye_f = jnp.where(eye, 1.0, 0.0)
    sq_mask = (ri >> 6) == (ci >> 6)
    grp_mask = (ri >> 7) == (ci >> 7)
    bd_mask = ones64
    grp_row_mask = ones128

    def decay_parts(G):
        g_row = jnp.sum(jnp.where(eye, G, 0.0), axis=0, keepdims=True)
        dm = jnp.where(tri, jnp.exp(jnp.minimum(G - g_row, 0.0)), 0.0)
        g_last = G[C - 1:C, :]
        return dm, g_last

    def hgrn_a(c):
        sl = pl.ds(c * C, C)
        G = hg_s[sl, :]
        mid = hg_s[pl.ds(c * C + C // 2 - 1, 1), :]
        qf = hq_s[sl, :] * jnp.exp(G - mid)
        kf = hk_s[sl, :] * jnp.exp(mid - G)
        scores = jnp.where(tri, _dot_nt(qf, _block_diag(kf, bd_mask)), 0.0)
        yield
        o_s[sl, 0:256] = _dot(scores, _block_diag(hv_s[sl, :], bd_mask))
        yield

    def hgrn_a_any_decay(c, carry):
        sl = pl.ds(pl.multiple_of(c * C, C), C)
        q = hq_s[sl, :]
        k = hk_s[sl, :]
        scores = jnp.where(eye, _dot_nt(q, _block_diag(k, bd_mask)), 0.0)
        s = C // 2
        lvl = 0
        while s >= 1:
            e = he_s[lvl, sl, :]
            upper = (ii & s) != 0
            qs = jnp.where(upper, q * e, 0.0)
            ks = jnp.where(upper, 0.0, k * e)
            sc = _dot_nt(qs, _block_diag(ks, bd_mask))
            same_block = (ii & -(2 * s)) == (jj & -(2 * s))
            scores = scores + jnp.where(same_block, sc, 0.0)
            s //= 2
            lvl += 1
        o_s[sl, 0:256] = _dot(scores, _block_diag(hv_s[sl, :], bd_mask))
        return carry

    def gdn_a(c):
        sl = pl.ds(c * C, C)
        q = gq_s[sl, :]
        k = gk_s[sl, :]
        G = gg_s[sl, :]
        beta = gb_s[sl, :]
        dm, _ = decay_parts(G)
        kq = _dot_nt(jnp.concatenate([k, q], axis=0), _block_diag(k, bd_mask))
        yield
        m = jnp.where(strict, kq[0:C] * dm * beta, 0.0)
        gqk_s[sl, :] = kq[C:2 * C] * dm
        p = _dot(m, _block_diag(m, bd_mask))
        yield
        tinv = eye_f - m
        n = 2
        while 2 * n < C:
            both = _dot(jnp.concatenate([p, tinv], axis=0), _block_diag(p, bd_mask))
            yield
            p = both[0:C]
            tinv = tinv + both[C:2 * C]
            n *= 2
        tinv = tinv + _dot(tinv, _block_diag(p, bd_mask))
        yield
        kb = k * beta * jnp.exp(G)
        rhs = jnp.concatenate([_block_diag(gv_s[sl, :] * beta, bd_mask), _block_diag(kb, bd_mask)], axis=1)
        sol = _dot(tinv, rhs)
        gsv_s[sl, :] = sol[:, 0:256]
        gsk_s[sl, :] = sol[:, 256:512]
        yield

    def ssd_a(c):
        sl = pl.ds(c * C, C)
        bm = sb_s[sl, :]
        dm, _ = decay_parts(sg_s[sl, :])
        cb = _dot_nt(sc_s[sl, :], _block_diag(bm, grp_row_mask))
        yield
        o_s[sl, 512:768] = _dot(cb * dm, _block_diag(sv_s[sl, :], bd_mask))
        yield

    def ret_a(c):
        sl = pl.ds(c * C, C)
        dm, _ = decay_parts(rg_s[sl, :])
        sc = _dot_nt(rq_s[sl, :], _block_diag(rk_s[sl, :], bd_mask)) * dm
        yield
        o_s[sl, 768:1024] = _dot(sc, _block_diag(rv_s[sl, :], bd_mask))
        yield

    def hgrn_b(c):
        sl = pl.ds(c * C, C)
        G = hg_s[sl, :]
        g_last = G[C - 1:C, :]
        st = st_s[0]
        o_s[sl, 0:256] += _dot_nt(hq_s[sl, :] * jnp.exp(G), st)
        yield
        kt = hk_s[sl, :] * jnp.exp(g_last - G)
        st_s[0] = st * jnp.exp(g_last) + jnp.where(sq_mask, _dot_tn(hv_s[sl, :], kt), 0.0)
        yield

    def gdn_b(c):
        sl = pl.ds(c * C, C)
        G = gg_s[sl, :]
        g_last = G[C - 1:C, :]
        st = st_s[1]
        tmp = _dot(jnp.concatenate([gsk_s[sl, :], gq_s[sl, :] * jnp.exp(G)], axis=0), st)
        yield
        u = gsv_s[sl, :] - tmp[0:C]
        kt = gk_s[sl, :] * jnp.exp(g_last - G)
        st_s[1] = st * jnp.exp(g_last) + jnp.where(sq_mask, _dot_tn(kt, u), 0.0)
        yield
        o_s[sl, 256:512] = tmp[C:2 * C] + _dot(gqk_s[sl, :], _block_diag(u, bd_mask))
        yield

    def ssd_b(c):
        sl = pl.ds(c * C, C)
        G = sg_s[sl, :]
        g_last = G[C - 1:C, :]
        st = st_s[2]
        o_s[sl, 512:768] += jnp.exp(G) * _dot(sc_s[sl, :], st)
        yield
        vt = sv_s[sl, :] * jnp.exp(g_last - G)
        st_s[2] = st * jnp.exp(g_last) + jnp.where(grp_mask, _dot_tn(sb_s[sl, :], vt), 0.0)
        yield

    def ret_b(c):
        sl = pl.ds(c * C, C)
        G = rg_s[sl, :]
        g_last = G[C - 1:C, :]
        st = st_s[3]
        o_s[sl, 768:1024] += jnp.exp(G) * _dot(rq_s[sl, :], st)
        yield
        vt = rv_s[sl, :] * jnp.exp(g_last - G)
        st_s[3] = st * jnp.exp(g_last) + jnp.where(sq_mask, _dot_tn(rk_s[sl, :], vt), 0.0)
        yield

    def post(c):
        sl = pl.ds(c * C, C)
        y_s[sl, :] = _post_mix(o_s[sl, 0:256], o_s[sl, 256:512], o_s[sl, 512:768], o_s[sl, 768:1024],
                               proj_s[sl, A_Z:A_Z + GROUP_W], proj_s[sl, B_Z:B_Z + GROUP_W],
                               proj_s[sl, C_Z:C_Z + GROUP_W], proj_s[sl, D_Z:D_Z + GROUP_W],
                               sx_s[sl, :], prm_ref, ones64, ones128).astype(BF16)
        yield

    chunks = range(n_chunks)
    every = lambda *gens: itertools.chain.from_iterable(gens)
    z_cols = [(z, z + GROUP_W) for z in (A_Z, B_Z, C_Z, D_Z)]
    _interleave([every(project((B_QKV, B_Z), (SMALL, NP)), gdn_prep())], project((A_Q, A_Z)), 1)
    _interleave([gdn_a(c) for c in chunks],
                every(hgrn_prep(), project((C_XBC, C_Z)), *[hgrn_a(c) for c in chunks],
                      ssd_prep(), project((D_Q, D_Z)), *[ssd_a(c) for c in chunks],
                      ret_prep(), project(*z_cols), *[ret_a(c) for c in chunks]),
                FILL_PER_ROUND)

    @pl.when(jnp.max(hspan_s[0:1, :]) > HGRN_SAFE_SPAN)
    def _():
        hgrn_level_exps()
        lax.fori_loop(0, n_chunks, hgrn_a_any_decay, 0)

    for c in chunks:
        done = [post(c - 1)] if c else []
        _interleave([gdn_b(c)], every(hgrn_b(c), ssd_b(c), ret_b(c), *done), 2)
    _interleave([post(n_chunks - 1)], (), 0)

    out = h_ref[0] + jnp.dot(y_s[...], wout_ref[0], preferred_element_type=F32)
    if final:
        out = _rms(out) * _prow(prm_ref, R_FINAL, D_MODEL)
    hout_ref[0] = out

    @pl.when(t == nt - 1)
    def _():
        sout_ref[0] = st_s[...]
        ctxout_ref[0, 0] = cbuf_g[valid:valid + 8, :]
        ctxout_ref[0, 1] = cbuf_s[valid:valid + 8, :]
        hgrn = jnp.transpose(st_s[0])
        gdn = st_s[1]
        ssd = st_s[2]
        ret = st_s[3]
        for hd in range(N_HEADS):
            lo, hi = hd * HEAD_DIM, (hd + 1) * HEAD_DIM
            hgo_ref[0, hd] = hgrn[lo:hi, lo:hi]
            gdo_ref[0, hd] = gdn[lo:hi, lo:hi]
            sdo_ref[0, hd] = ssd[(hd // 2) * 128:(hd // 2 + 1) * 128, lo:hi]
            rto_ref[0, hd] = ret[lo:hi, lo:hi]


def _prompt_layer(layer, h, cos, sin, win, wout, prm, expand, s0, ctx0, *, tile, valid, final):
    bsz, seq, _ = h.shape
    nt = seq // tile
    kern = functools.partial(_prompt_kernel, layer, tile, valid, final)
    const2 = lambda b, t: (0, 0)
    slab = lambda: pltpu.VMEM((tile, GROUP_W), F32)
    sq_state = (N_HEADS, HEAD_DIM, HEAD_DIM)
    head_states = [sq_state, sq_state, (N_HEADS, 128, HEAD_DIM), sq_state]
    return pl.pallas_call(
        kern,
        grid=(bsz, nt),
        in_specs=[
            pl.BlockSpec((1, tile, D_MODEL), lambda b, t: (b, t, 0)),
            pl.BlockSpec((tile, GROUP_W), lambda b, t: (t, 0)),
            pl.BlockSpec((tile, GROUP_W), lambda b, t: (t, 0)),
            pl.BlockSpec((1, D_MODEL, NP), lambda b, t: (layer, 0, 0)),
            pl.BlockSpec((1, D_MODEL, D_MODEL), lambda b, t: (layer, 0, 0)),
            pl.BlockSpec((1, N_PRM, D_MODEL), lambda b, t: (layer, 0, 0)),
            pl.BlockSpec(expand.shape, const2),
            pl.BlockSpec((4, GROUP_W, GROUP_W), lambda b, t: (0, 0, 0)),
            pl.BlockSpec((2, 8, CONV_DIM), lambda b, t: (0, 0, 0)),
        ],
        out_specs=[
            pl.BlockSpec((1, tile, D_MODEL), lambda b, t: (b, t, 0)),
            pl.BlockSpec((1, 4, GROUP_W, GROUP_W), lambda b, t: (b, 0, 0, 0)),
            pl.BlockSpec((1, 2, 8, CONV_DIM), lambda b, t: (b, 0, 0, 0)),
        ] + [pl.BlockSpec((1,) + s, lambda b, t: (b, 0, 0, 0)) for s in head_states],
        out_shape=[
            jax.ShapeDtypeStruct((bsz, seq, D_MODEL), F32),
            jax.ShapeDtypeStruct((bsz, 4, GROUP_W, GROUP_W), F32),
            jax.ShapeDtypeStruct((bsz, 2, 8, CONV_DIM), F32),
        ] + [jax.ShapeDtypeStruct((bsz,) + s, F32) for s in head_states],
        scratch_shapes=[
            pltpu.VMEM((tile, NP), F32),
            pltpu.VMEM((tile + 8, CONV_DIM), F32),
            pltpu.VMEM((tile + 8, CONV_DIM), F32),
            pltpu.VMEM((4, GROUP_W, GROUP_W), F32),
            pltpu.VMEM((6, tile, GROUP_W), F32),
        ] + [slab() for _ in range(22)] + [pltpu.VMEM((tile, D_MODEL), F32), pltpu.VMEM((tile, D_MODEL), BF16),
                                             pltpu.VMEM((tile, D_MODEL), BF16), pltpu.VMEM((8, GROUP_W), F32)],
        compiler_params=pltpu.CompilerParams(
            dimension_semantics=("arbitrary", "arbitrary"), vmem_limit_bytes=VMEM_LIMIT),
        name=f"prompt_layer_t{tile}",
    )(h, cos, sin, win, wout, prm, expand, s0, ctx0)


(VT_HQ, VT_HK, VT_HF, VT_HV, VT_GQ, VT_GK, VT_GV, VT_GDEC, VT_GBETA,
 VT_SV, VT_SB, VT_SC, VT_SDEC, VT_RQ, VT_RK, VT_RV, VT_RDEC) = [GROUP_W * i for i in range(17)]
N_VT = 17 * GROUP_W
DEC_VBLOCK = 32


def _dec_pre_kernel(layer, hs_ref, cos_ref, sin_ref, win_ref, prm_ref, exp_ref, gctx_ref, sctx_ref,
                    vt_ref, post_ref, ug_ref, us_ref):
    x = hs_ref[...]
    nb = x.shape[0]
    hn = _rms(x) * _prow(prm_ref, R_NORM, D_MODEL)
    proj = jnp.dot(hn.astype(BF16), win_ref[0], preferred_element_type=F32)

    ri = _iota((GROUP_W, GROUP_W), 0)
    ci = _iota((GROUP_W, GROUP_W), 1)
    ones64 = jnp.where((ri >> 6) == (ci >> 6), 1.0, 0.0).astype(BF16)

    def put(off, a):
        vt_ref[off:off + GROUP_W, :] = jnp.transpose(a)

    lb = _hgrn_lower_bound(_prow(prm_ref, R_LB, n=DEPTH), layer)
    q, k, f, v = _hgrn_inputs(proj, lb)
    put(VT_HQ, q)
    put(VT_HK, k)
    put(VT_HF, f)
    put(VT_HV, v)

    log_decay, vals = _compact_gates(proj[:, SMALL:SMALL + 128], prm_ref)
    decay = jnp.exp(_dot_exact_rhs(log_decay, exp_ref[:, 0:512]))
    vals = _dot_exact_rhs(vals, exp_ref[:, 512:1024])

    u = proj[:, B_QKV:B_QKV + CONV_DIM]
    ug_ref[...] = u
    cw = _prow(prm_ref, R_GCONV, CONV_DIM, CONV_W)
    conv = cw[3:4] * u + cw[2:3] * gctx_ref[0, 2] + cw[1:2] * gctx_ref[0, 1] + cw[0:1] * gctx_ref[0, 0]
    act = _silu(conv)
    qn, kn = _l2n_pair(act[:, 0:256], act[:, 256:512], ones64)
    put(VT_GQ, qn * QK_SCALE)
    put(VT_GK, kn)
    put(VT_GV, act[:, 512:768])
    put(VT_GDEC, decay[:, 0:256])
    put(VT_GBETA, vals[:, 0:256])

    u = proj[:, C_XBC:C_XBC + CONV_DIM]
    us_ref[...] = u
    cw = _prow(prm_ref, R_SCONV, CONV_DIM, CONV_W)
    conv = (cw[3:4] * u + cw[2:3] * sctx_ref[0, 2] + cw[1:2] * sctx_ref[0, 1] + cw[0:1] * sctx_ref[0, 0]
            + _prow(prm_ref, R_SCONVB, CONV_DIM))
    act = _silu(conv)
    post_ref[:, POST_SX:POST_SX + 256] = act[:, 0:256]
    put(VT_SV, act[:, 0:256] * vals[:, 256:512])
    put(VT_SB, act[:, 256:512])
    put(VT_SC, act[:, 512:768])
    put(VT_SDEC, decay[:, 256:512])

    cos = cos_ref[...]
    sin = sin_ref[...]
    put(VT_RQ, _rotary(proj[:, D_Q:D_Q + GROUP_W], cos, sin))
    put(VT_RK, _rotary(proj[:, D_K:D_K + GROUP_W], cos, sin) * QK_SCALE)
    put(VT_RV, proj[:, D_V:D_V + GROUP_W])
    put(VT_RDEC, jnp.broadcast_to(jnp.exp(_prow(prm_ref, R_RLOGG)), (nb, GROUP_W)))

    post_ref[:, POST_AZ:POST_AZ + 256] = proj[:, A_Z:A_Z + GROUP_W]
    post_ref[:, POST_BZ:POST_BZ + 256] = proj[:, B_Z:B_Z + GROUP_W]
    post_ref[:, POST_CZ:POST_CZ + 256] = proj[:, C_Z:C_Z + GROUP_W]
    post_ref[:, POST_DZ:POST_DZ + 256] = proj[:, D_Z:D_Z + GROUP_W]


def _dec_pre(layer, hs, cos, sin, win, prm, expand, gctx, sctx):
    nb = hs.shape[0]
    full = lambda a: pl.BlockSpec(a.shape, lambda i: (0,) * a.ndim)
    shapes = [(N_VT, nb), (nb, N_POST), (nb, CONV_DIM), (nb, CONV_DIM)]
    return pl.pallas_call(
        functools.partial(_dec_pre_kernel, layer),
        grid=(1,),
        in_specs=[
            full(hs), full(cos), full(sin),
            pl.BlockSpec((1, D_MODEL, NP), lambda i: (layer, 0, 0)),
            pl.BlockSpec((1, N_PRM, D_MODEL), lambda i: (layer, 0, 0)),
            full(expand),
            pl.BlockSpec((1, 3, nb, CONV_DIM), lambda i: (layer, 0, 0, 0)),
            pl.BlockSpec((1, 3, nb, CONV_DIM), lambda i: (layer, 0, 0, 0)),
        ],
        out_specs=[pl.BlockSpec(s, lambda i: (0, 0)) for s in shapes],
        out_shape=[jax.ShapeDtypeStruct(s, F32) for s in shapes],
        compiler_params=pltpu.CompilerParams(dimension_semantics=("arbitrary",), vmem_limit_bytes=VMEM_LIMIT),
        name="decode_pre",
    )(hs, cos, sin, win, prm, expand, gctx, sctx)


def _dec_state_kernel(vt_ref, hg_ref, gd_ref, sd_ref, rt_ref, hg_acc, gd_acc, rt_acc,
                      o_ref, hg_out, gd_out, sd_out, rt_out):
    del hg_acc, gd_acc, rt_acc
    h = pl.program_id(0)
    j = pl.program_id(1)
    vb = DEC_VBLOCK
    nb = vt_ref.shape[1]
    head = h * HEAD_DIM

    def row(off, i):
        return vt_ref[pl.ds(off + i, 1), :]

    def slab(off):
        return vt_ref[pl.ds(pl.multiple_of(off + head + j * vb, 8), vb), :]

    zero = jnp.zeros((vb, nb), F32)

    v = slab(VT_HV)

    def hgrn_step(k, o):
        s = row(VT_HF + head, k) * hg_ref[0, 0, k] + row(VT_HK + head, k) * v
        hg_out[0, 0, k] = s
        return o + row(VT_HQ + head, k) * s

    o_ref[0, 0] = lax.fori_loop(0, HEAD_DIM, hgrn_step, zero, unroll=4)

    v = slab(VT_GV)
    dec = row(VT_GDEC + head, 0)
    ks = lax.fori_loop(0, HEAD_DIM, lambda k, a: a + row(VT_GK + head, k) * gd_ref[0, 0, k], zero, unroll=4)
    u = row(VT_GBETA + head, 0) * (v - ks * dec)

    def gdn_step(k, o):
        s = gd_ref[0, 0, k] * dec + row(VT_GK + head, k) * u
        gd_out[0, 0, k] = s
        return o + row(VT_GQ + head, k) * s

    o_ref[1, 0] = lax.fori_loop(0, HEAD_DIM, gdn_step, zero, unroll=4)

    v = slab(VT_SV)
    dec = row(VT_SDEC + head, 0)
    grp = (h >> 1) * 128

    def ssd_step(n, o):
        s = sd_ref[0, 0, n] * dec + row(VT_SB + grp, n) * v
        sd_out[0, 0, n] = s
        return o + row(VT_SC + grp, n) * s

    o_ref[2, 0] = lax.fori_loop(0, 128, ssd_step, zero, unroll=4)

    v = slab(VT_RV)
    dec = row(VT_RDEC + head, 0)

    def ret_step(k, o):
        s = rt_ref[0, 0, k] * dec + row(VT_RK + head, k) * v
        rt_out[0, 0, k] = s
        return o + row(VT_RQ + head, k) * s

    o_ref[3, 0] = lax.fori_loop(0, HEAD_DIM, ret_step, zero, unroll=4)


def _dec_state(layer, vt, states, accs):
    nb = vt.shape[1]
    vb = DEC_VBLOCK
    blk = lambda a: pl.BlockSpec((1, 1, a.shape[2], vb, nb), lambda h, j: (layer, h, 0, j, 0))
    return pl.pallas_call(
        _dec_state_kernel,
        grid=(N_HEADS, HEAD_DIM // vb),
        in_specs=[pl.BlockSpec(vt.shape, lambda h, j: (0, 0))] + [blk(a) for a in states]
        + [pl.BlockSpec(memory_space=pl.ANY)] * 3,
        out_specs=[pl.BlockSpec((4, 1, vb, nb), lambda h, j: (0, h, j, 0))] + [blk(a) for a in states],
        out_shape=[jax.ShapeDtypeStruct((4, N_HEADS, HEAD_DIM, nb), F32)]
        + [jax.ShapeDtypeStruct(a.shape, F32) for a in states],
        input_output_aliases={3: 3, 5: 1, 6: 2, 7: 4},
        compiler_params=pltpu.CompilerParams(dimension_semantics=("arbitrary", "arbitrary"),
                                             vmem_limit_bytes=VMEM_LIMIT),
        name="decode_state",
    )(vt, *states, *accs)


def _dec_post_kernel(final, hs_ref, ot_ref, post_ref, wout_ref, prm_ref, out_ref):
    ri = _iota((GROUP_W, GROUP_W), 0)
    ci = _iota((GROUP_W, GROUP_W), 1)
    ones64 = jnp.where((ri >> 6) == (ci >> 6), 1.0, 0.0).astype(BF16)
    ones128 = jnp.where((ri >> 7) == (ci >> 7), 1.0, 0.0).astype(BF16)
    gate = lambda off: post_ref[:, off:off + GROUP_W]
    o = [jnp.transpose(ot_ref[m * GROUP_W:(m + 1) * GROUP_W, :]) for m in range(4)]
    y = _post_mix(o[0], o[1], o[2], o[3],
                  gate(POST_AZ), gate(POST_BZ), gate(POST_CZ), gate(POST_DZ), gate(POST_SX),
                  prm_ref, ones64, ones128)
    out = hs_ref[...] + jnp.dot(y.astype(BF16), wout_ref[0], preferred_element_type=F32)
    if final:
        out = _rms(out) * _prow(prm_ref, R_FINAL, D_MODEL)
    out_ref[...] = out


def _dec_post(layer, hs, ot, post, wout, prm, final):
    nb = hs.shape[0]
    full = lambda a: pl.BlockSpec(a.shape, lambda i: (0,) * a.ndim)
    return pl.pallas_call(
        functools.partial(_dec_post_kernel, final),
        grid=(1,),
        in_specs=[full(hs), full(ot), full(post),
                  pl.BlockSpec((1, D_MODEL, D_MODEL), lambda i: (layer, 0, 0)),
                  pl.BlockSpec((1, N_PRM, D_MODEL), lambda i: (layer, 0, 0))],
        out_specs=pl.BlockSpec((nb, D_MODEL), lambda i: (0, 0)),
        out_shape=jax.ShapeDtypeStruct((nb, D_MODEL), F32),
        compiler_params=pltpu.CompilerParams(dimension_semantics=("arbitrary",), vmem_limit_bytes=VMEM_LIMIT),
        name="decode_post",
    )(hs, ot, post, wout, prm)


POST_AZ, POST_BZ, POST_CZ, POST_DZ, POST_SX = [GROUP_W * i for i in range(5)]
N_POST = 5 * GROUP_W


def _expand_matrix():
    e = np.zeros((128, 4 * GROUP_W), np.float32)
    for block, first_lane in enumerate((0, 2 * N_HEADS, N_HEADS, 2 * N_HEADS)):
        for h in range(N_HEADS):
            lo = block * GROUP_W + h * HEAD_DIM
            e[first_lane + h, lo:lo + HEAD_DIM] = 1.0
    return jnp.asarray(e, BF16)


def _relayout_w_in(w_in):
    sizes = (256, 256, 256, 256, 768, 256, 4, 4, 768, 256, 4, 256, 256, 256, 256)
    offs = np.concatenate([[0], np.cumsum(sizes)])
    w_t = jnp.transpose(w_in, (2, 0, 1))
    seg = lambda i: w_t[offs[i]:offs[i + 1]]
    pad = jnp.zeros((128 - 12,) + w_t.shape[1:], w_in.dtype)
    rows = [seg(0), seg(1), seg(2), seg(3), seg(4), seg(5), seg(8), seg(9), seg(11), seg(12), seg(13), seg(14),
            seg(6), seg(7), seg(10), pad]
    return jnp.transpose(jnp.concatenate(rows, axis=0), (1, 2, 0)).astype(BF16)


def _pack_params(norm_w, hgrn_lb_logits, hgrn_norm_w, gdn_conv_w, gdn_a_log, gdn_dt_bias, gdn_norm_w,
                 ssd_conv_w, ssd_conv_b, ssd_a_log, ssd_dt_bias, ssd_d, ssd_norm_w, ret_norm_w, ret_norm_b,
                 final_norm_w):
    depth = norm_w.shape[0]
    prm = jnp.zeros((depth, N_PRM, D_MODEL), F32)

    def put(p, r, a):
        a = a.astype(F32)
        if a.ndim == 2:
            a = a[:, None, :]
        return p.at[:, r:r + a.shape[1], 0:a.shape[2]].set(a)

    rep = lambda a: jnp.repeat(a, HEAD_DIM, axis=-1)
    prm = put(prm, R_NORM, norm_w)
    prm = put(prm, R_LB, jnp.broadcast_to(hgrn_lb_logits[None], (depth,) + hgrn_lb_logits.shape))
    prm = put(prm, R_HNORM, hgrn_norm_w)
    prm = put(prm, R_GCONV, gdn_conv_w)
    prm = put(prm, R_GNORM, gdn_norm_w)
    prm = put(prm, R_SCONV, ssd_conv_w)
    prm = put(prm, R_SCONVB, ssd_conv_b)
    prm = put(prm, R_SD, rep(ssd_d))
    gap = jnp.zeros((depth, N_HEADS), F32)
    prm = put(prm, R_CBIAS, jnp.concatenate([gdn_dt_bias.astype(F32), gap, ssd_dt_bias.astype(F32)], axis=-1))
    prm = put(prm, R_CALOG, jnp.concatenate([gdn_a_log.astype(F32), gap, ssd_a_log.astype(F32)], axis=-1))
    prm = put(prm, R_SNORM, ssd_norm_w)
    prm = put(prm, R_RNORM, ret_norm_w)
    prm = put(prm, R_RNORMB, ret_norm_b)
    ret_logg = jnp.log1p(-jnp.exp2(-5.0 - jnp.arange(N_HEADS, dtype=F32)))
    prm = put(prm, R_RLOGG, jnp.broadcast_to(rep(ret_logg)[None], (depth, GROUP_W)))
    prm = put(prm, R_FINAL, jnp.broadcast_to(final_norm_w[None], (depth, D_MODEL)))
    return prm


def _rope_tables(pos):
    half = HEAD_DIM // 2
    inv_freq = 1.0 / (ROPE_BASE ** jnp.linspace(0.0, 1.0, half, dtype=F32))
    ang = pos[:, None] * inv_freq[None, :]
    cos, sin = jnp.cos(ang), jnp.sin(ang)
    cos_h = jnp.concatenate([cos, cos], axis=-1)
    sin_h = jnp.concatenate([-sin, sin], axis=-1)
    return jnp.tile(cos_h, (1, N_HEADS)), jnp.tile(sin_h, (1, N_HEADS))


def kernel(x_prompt, x_sample, state_hgrn, state_gdn, state_gdn_conv, state_ssd, state_ssd_conv, state_ret,
           meta_tokens, norm_w, w_in, hgrn_lb_logits, hgrn_norm_w, gdn_conv_w, gdn_a_log, gdn_dt_bias, gdn_norm_w,
           ssd_conv_w, ssd_conv_b, ssd_a_log, ssd_dt_bias, ssd_d, ssd_norm_w, ret_norm_w, ret_norm_b,
           w_out, final_norm_w):
    depth = w_in.shape[0]
    bsz, seq, _ = x_prompt.shape
    nb = x_sample.shape[0]
    tile = PROMPT_TILE if seq % PROMPT_TILE == 0 else CHUNK

    win = _relayout_w_in(w_in)
    wout = w_out.astype(BF16)
    prm = _pack_params(norm_w, hgrn_lb_logits, hgrn_norm_w, gdn_conv_w, gdn_a_log, gdn_dt_bias, gdn_norm_w,
                       ssd_conv_w, ssd_conv_b, ssd_a_log, ssd_dt_bias, ssd_d, ssd_norm_w, ret_norm_w, ret_norm_b,
                       final_norm_w)
    expand = _expand_matrix()

    cos_m, sin_m = _rope_tables(jnp.arange(CHUNK, dtype=F32))
    cos_p, sin_p = _rope_tables(N_META + jnp.arange(seq, dtype=F32))
    cos_s, sin_s = _rope_tables(PAST_LEN + jnp.arange(1, dtype=F32))

    hm = jnp.zeros((1, CHUNK, D_MODEL), F32).at[0, :N_META].set(meta_tokens.astype(F32))
    hp = x_prompt.astype(F32)
    hs = x_sample.astype(F32)[:, 0, :]
    zero_s = jnp.zeros((4, GROUP_W, GROUP_W), F32)
    zero_ctx = jnp.zeros((2, 8, CONV_DIM), F32)

    gctx = jnp.swapaxes(state_gdn_conv.astype(F32), 1, 2)
    sctx = jnp.swapaxes(state_ssd_conv.astype(F32), 1, 2)

    seq_minor = lambda a: jnp.transpose(a.astype(F32), (0, 2, 3, 4, 1))
    dec_states = [seq_minor(a) for a in (state_hgrn, state_gdn, state_ssd, state_ret)]
    dec_new = [jnp.zeros_like(dec_states[0]), jnp.zeros_like(dec_states[1]), dec_states[2],
               jnp.zeros_like(dec_states[3])]

    p_states, p_ctx = [], []
    s_gctx, s_sctx = [], []
    for l in range(depth):
        final = l == depth - 1
        hm, sm, cm, *_ = _prompt_layer(l, hm, cos_m, sin_m, win, wout, prm, expand, zero_s, zero_ctx,
                                       tile=CHUNK, valid=N_META, final=False)
        hp, _, cp, *head_states = _prompt_layer(l, hp, cos_p, sin_p, win, wout, prm, expand, sm[0], cm[0],
                                                tile=tile, valid=tile, final=final)
        p_states.append(head_states)
        p_ctx.append(cp)

        vt, post, ug, us = _dec_pre(l, hs, cos_s, sin_s, win, prm, expand, gctx, sctx)
        ot, *dec_new = _dec_state(l, vt, [dec_states[0], dec_states[1], dec_new[2], dec_states[3]],
                                  [dec_new[0], dec_new[1], dec_new[3]])
        hs = _dec_post(l, hs, ot.reshape(4 * GROUP_W, nb), post, wout, prm, final)
        s_gctx.append(jnp.concatenate([state_gdn_conv[l, :, 1:].astype(F32), ug[:, None, :]], axis=1))
        s_sctx.append(jnp.concatenate([state_ssd_conv[l, :, 1:].astype(F32), us[:, None, :]], axis=1))

    ph, pg, ps, pr = [jnp.stack([layer_states[i] for layer_states in p_states]) for i in range(4)]
    cp = jnp.stack(p_ctx)
    seq_major = lambda a: jnp.transpose(a, (0, 4, 1, 2, 3))
    return (hp.astype(x_prompt.dtype), hs[:, None, :].astype(x_sample.dtype),
            ph.astype(state_hgrn.dtype), pg.astype(state_gdn.dtype),
            cp[:, :, 0, 8 - (CONV_W - 1):].astype(state_gdn_conv.dtype),
            ps.astype(state_ssd.dtype),
            cp[:, :, 1, 8 - (CONV_W - 1):].astype(state_ssd_conv.dtype),
            pr.astype(state_ret.dtype),
            seq_major(dec_new[0]).astype(state_hgrn.dtype), seq_major(dec_new[1]).astype(state_gdn.dtype),
            jnp.stack(s_gctx).astype(state_gdn_conv.dtype), seq_major(dec_new[2]).astype(state_ssd.dtype),
            jnp.stack(s_sctx).astype(state_ssd_conv.dtype), seq_major(dec_new[3]).astype(state_ret.dtype))
```

```python
import functools
import itertools

import numpy as np
import jax
import jax.numpy as jnp
from jax import lax
from jax.experimental import pallas as pl
from jax.experimental.pallas import tpu as pltpu

F32 = jnp.float32
BF16 = jnp.bfloat16

D_MODEL = 1024
GROUP_W = 256
HEAD_DIM = 64
N_HEADS = 4
CHUNK = 64
CONV_W = 4
CONV_DIM = 768
N_META = 16
PAST_LEN = 16384
ROPE_BASE = 10000.0
EPS = 1e-6
TINY = 1e-30
QK_SCALE = HEAD_DIM ** -0.5
DEPTH = 4

A_Q, A_F, A_I, A_Z = 0, 256, 512, 768
B_QKV, B_Z = 1024, 1792
C_XBC, C_Z = 2048, 2816
D_Q, D_K, D_V, D_Z = 3072, 3328, 3584, 3840
SMALL = 4096
NP = 4224

R_NORM, R_LB, R_HNORM = 0, 1, 5
R_GCONV, R_GNORM = 6, 12
R_SCONV, R_SCONVB, R_SD, R_SNORM = 13, 17, 20, 21
R_RNORM, R_RNORMB, R_RLOGG, R_FINAL = 22, 23, 24, 25
R_CBIAS, R_CALOG = 26, 27
N_PRM = 32

VMEM_LIMIT = 56 * 1024 * 1024
PROMPT_TILE = 256
HGRN_SAFE_SPAN = 80.0
FILL_PER_ROUND = 4
_DONE = object()


def _interleave(leads, fillers, fill_per_round):
    leads = list(leads)
    fillers = iter(fillers)
    fill_live = fill_per_round > 0
    while leads or fill_live:
        leads = [g for g in leads if next(g, _DONE) is not _DONE]
        for _ in range(fill_per_round):
            if fill_live and next(fillers, _DONE) is _DONE:
                fill_live = False


def _dot(a, b):
    return jnp.dot(a.astype(BF16), b.astype(BF16), preferred_element_type=F32)


def _dot_nt(a, b):
    return lax.dot_general(a.astype(BF16), b.astype(BF16), (((1,), (1,)), ((), ())),
                           preferred_element_type=F32)


def _dot_tn(a, b):
    return lax.dot_general(a.astype(BF16), b.astype(BF16), (((0,), (0,)), ((), ())),
                           preferred_element_type=F32)


def _split3(x):
    hi = x.astype(BF16)
    r1 = x - hi.astype(F32)
    mid = r1.astype(BF16)
    lo = (r1 - mid.astype(F32)).astype(BF16)
    return hi, mid, lo


def _dot_exact_rhs(x, m):
    hi, mid, lo = _split3(x)
    return (jnp.dot(hi, m, preferred_element_type=F32) + jnp.dot(mid, m, preferred_element_type=F32)
            + jnp.dot(lo, m, preferred_element_type=F32))


def _dot_exact_lhs(m, x):
    hi, mid, lo = _split3(x)
    return (jnp.dot(m, hi, preferred_element_type=F32) + jnp.dot(m, mid, preferred_element_type=F32)
            + jnp.dot(m, lo, preferred_element_type=F32))


def _sigmoid(x):
    return jax.nn.sigmoid(x)


def _silu(x):
    return x * _sigmoid(x)


def _softplus(x):
    return jnp.maximum(x, 0.0) + jnp.log(1.0 + jnp.exp(-jnp.abs(x)))


def _iota(shape, dim):
    return lax.broadcasted_iota(jnp.int32, shape, dim)


def _block_diag(x, mask01):
    xb = x.astype(BF16)
    return jnp.concatenate([xb] * N_HEADS, axis=0) * mask01


def _chunk_cumsum(x, row_in_chunk):
    sh = 1
    while sh < CHUNK:
        r = pltpu.roll(x, sh, axis=0)
        x = x + jnp.where(row_in_chunk >= sh, r, 0.0)
        sh *= 2
    return x


def _swap_halves(x):
    lane = _iota(x.shape, 1)
    fwd = pltpu.roll(x, GROUP_W - HEAD_DIM // 2, axis=1)
    bwd = pltpu.roll(x, HEAD_DIM // 2, axis=1)
    return jnp.where((lane & (HEAD_DIM - 1)) < HEAD_DIM // 2, fwd, bwd)


def _group_sums(xs, ones_bd, squares=()):
    parts, spans = [], []
    for i, x in enumerate(xs):
        hi = x.astype(BF16)
        terms = [hi] if i in squares else [hi, (x - hi.astype(F32)).astype(BF16)]
        spans.append((len(parts), len(terms)))
        parts += terms
    sums = jnp.dot(jnp.concatenate(parts, axis=0) if len(parts) > 1 else parts[0], ones_bd,
                   preferred_element_type=F32)
    n = xs[0].shape[0]
    out = []
    for first, count in spans:
        total = sums[first * n:(first + 1) * n]
        for j in range(1, count):
            total = total + sums[(first + j) * n:(first + j + 1) * n]
        out.append(total)
    return out


def _group_sum(x, ones_bd, square=False):
    return _group_sums([x], ones_bd, squares=(0,) if square else ())[0]


def _rms(x):
    return x * lax.rsqrt(jnp.mean(x * x, axis=-1, keepdims=True) + EPS)


def _hgrn_lower_bound(logits, layer):
    m = jnp.max(logits, axis=0, keepdims=True)
    e = jnp.exp(logits - m)
    w = e / jnp.sum(e, axis=0, keepdims=True)
    cum = jnp.sum(w[0:layer + 1], axis=0, keepdims=True) - w[0:1]
    return jnp.maximum(cum, 0.0)


def _prow(prm_ref, r, width=GROUP_W, n=1):
    return prm_ref[0, r:r + n, 0:width]


def _hgrn_inputs(proj, lb):
    aq = proj[:, A_Q:A_Q + GROUP_W]
    af = proj[:, A_F:A_F + GROUP_W]
    q = _silu(aq) * QK_SCALE
    k = (1.0 - lb) * _sigmoid(-af)
    f = jnp.maximum(lb + (1.0 - lb) * _sigmoid(af), TINY)
    return q, k, f, proj[:, A_I:A_I + GROUP_W]


def _compact_gates(c, prm_ref):
    sp = _softplus(c + _prow(prm_ref, R_CBIAS, 128))
    log_decay = -jnp.exp(_prow(prm_ref, R_CALOG, 128)) * sp
    lane = _iota(c.shape, 1)
    vals = jnp.where((lane >= N_HEADS) & (lane < 2 * N_HEADS), _sigmoid(c), sp)
    return log_decay, vals


def _l2n_pair(q, k, ones64):
    sq_q, sq_k = _group_sums([q * q, k * k], ones64, squares=(0, 1))
    return q * lax.rsqrt(sq_q + EPS), k * lax.rsqrt(sq_k + EPS)


def _rotary(x, cos, sin):
    return x * cos + _swap_halves(x) * sin


def _post_mix(o_a, o_b, o_c, o_d, z_a, z_b, z_c, z_d, x_ssd, prm_ref, ones64, ones128):
    inv64 = 1.0 / HEAD_DIM
    sq_a, sq_b, sum_d = _group_sums([o_a * o_a, o_b * o_b, o_d], ones64, squares=(0, 1))
    y_a = o_a * lax.rsqrt(sq_a * inv64 + EPS) * _prow(prm_ref, R_HNORM) * _silu(z_a)
    y_b = o_b * lax.rsqrt(sq_b * inv64 + EPS) * _prow(prm_ref, R_GNORM) * _silu(z_b)
    t_c = (o_c + _prow(prm_ref, R_SD) * x_ssd) * _silu(z_c)
    y_c = (t_c * lax.rsqrt(_group_sum(t_c * t_c, ones128, square=True) * (1.0 / 128.0) + EPS)
           * _prow(prm_ref, R_SNORM))
    xc = o_d - sum_d * inv64
    y_d = ((xc * lax.rsqrt(_group_sum(xc * xc, ones64, square=True) * inv64 + EPS) * _prow(prm_ref, R_RNORM)
            + _prow(prm_ref, R_RNORMB)) * _silu(z_d))
    return jnp.concatenate([y_a, y_b, y_c, y_d], axis=-1)


def _prompt_kernel(layer, tile, valid, final,
                   h_ref, cos_ref, sin_ref, win_ref, wout_ref, prm_ref, exp_ref, s0_ref, ctx0_ref,
                   hout_ref, sout_ref, ctxout_ref, hgo_ref, gdo_ref, sdo_ref, rto_ref,
                   proj_s, cbuf_g, cbuf_s, st_s, he_s,
                   hq_s, hk_s, hv_s, hg_s,
                   gq_s, gk_s, gv_s, gg_s, gb_s, gqk_s, gsv_s, gsk_s,
                   sx_s, sv_s, sb_s, sc_s, sg_s, sdt_s,
                   rq_s, rk_s, rv_s, rg_s, o_s, y_s, hn_s, hspan_s):
    t = pl.program_id(1)
    nt = pl.num_programs(1)
    n_chunks = tile // CHUNK
    C = CHUNK

    @pl.when(t == 0)
    def _():
        st_s[...] = s0_ref[...]
        cbuf_g[0:8, :] = ctx0_ref[0]
        cbuf_s[0:8, :] = ctx0_ref[1]

    hn_s[...] = (_rms(h_ref[0]) * _prow(prm_ref, R_NORM, D_MODEL)).astype(BF16)

    def project(*col_ranges):
        for lo, hi in col_ranges:
            proj_s[:, lo:hi] = jnp.dot(hn_s[...], win_ref[0, :, lo:hi], preferred_element_type=F32)
        yield

    row = _iota((tile, GROUP_W), 0)
    row_in_chunk = row & (C - 1)
    if valid < tile:
        live = row < valid
        keep = lambda a: jnp.where(live[:, 0:a.shape[1]], a, 0.0)
    else:
        keep = lambda a: a

    ri = _iota((GROUP_W, GROUP_W), 0)
    ci = _iota((GROUP_W, GROUP_W), 1)
    ones64 = jnp.where((ri >> 6) == (ci >> 6), 1.0, 0.0).astype(BF16)
    ones128 = jnp.where((ri >> 7) == (ci >> 7), 1.0, 0.0).astype(BF16)

    def conv_taps(cbuf, u, r_w):
        cbuf[8:8 + tile, :] = u
        cw = _prow(prm_ref, r_w, CONV_DIM, CONV_W)
        ext = cbuf[...]
        acc = cw[0:1] * ext
        for tap in range(1, CONV_W):
            acc = pltpu.roll(acc, 1, axis=0) + cw[tap:tap + 1] * ext
        cbuf[0:8, :] = cbuf[tile:tile + 8, :]
        return acc[8:8 + tile, :]

    def gdn_prep():
        log_decay, vals = _compact_gates(proj_s[:, SMALL:SMALL + 128], prm_ref)
        cum = _chunk_cumsum(keep(log_decay), row_in_chunk[:, 0:128])
        cum = _dot_exact_rhs(cum, exp_ref[:, 0:512])
        gg_s[...] = cum[:, 0:256]
        sg_s[...] = cum[:, 256:512]
        vals = _dot_exact_rhs(keep(vals), exp_ref[:, 512:1024])
        gb_s[...] = vals[:, 0:256]
        sdt_s[...] = vals[:, 256:512]
        yield
        act = _silu(conv_taps(cbuf_g, proj_s[:, B_QKV:B_QKV + CONV_DIM], R_GCONV))
        qn, kn = _l2n_pair(act[:, 0:256], act[:, 256:512], ones64)
        gq_s[...] = qn * QK_SCALE
        gk_s[...] = kn
        gv_s[...] = act[:, 512:768]
        yield

    def hgrn_prep():
        lb = _hgrn_lower_bound(_prow(prm_ref, R_LB, n=DEPTH), layer)
        q, k, f, v = _hgrn_inputs(proj_s[...], lb)
        hq_s[...] = q
        hk_s[...] = k
        hv_s[...] = keep(v)
        yield
        G = _chunk_cumsum(keep(jnp.log(f)), row_in_chunk)
        hg_s[...] = G
        span = jnp.zeros((1, GROUP_W), F32)
        for c in range(n_chunks):
            mid = G[c * C + C // 2 - 1:c * C + C // 2, :]
            span = jnp.maximum(span, jnp.maximum(G[c * C:c * C + 1, :] - mid, mid - G[c * C + C - 1:c * C + C, :]))
        hspan_s[...] = jnp.broadcast_to(span, hspan_s.shape)
        yield

    def hgrn_level_exps():
        G = hg_s[...]
        last = G
        s = 1
        lvl = 5
        while s < C:
            upper = (row & s) != 0
            ref_row = jnp.where(upper, pltpu.roll(last, s, axis=0), last)
            he_s[lvl] = jnp.exp(jnp.minimum(jnp.where(upper, G - ref_row, ref_row - G), 0.0))
            last = jnp.where(upper, last, pltpu.roll(last, tile - s, axis=0))
            s *= 2
            lvl -= 1

    def ssd_prep():
        act = _silu(conv_taps(cbuf_s, proj_s[:, C_XBC:C_XBC + CONV_DIM], R_SCONV)
                    + _prow(prm_ref, R_SCONVB, CONV_DIM))
        dt = sdt_s[...]
        sx_s[...] = act[:, 0:256]
        sv_s[...] = act[:, 0:256] * dt
        sb_s[...] = act[:, 256:512]
        sc_s[...] = act[:, 512:768]
        yield

    def ret_prep():
        cos = cos_ref[...]
        sin = sin_ref[...]
        rq_s[...] = _rotary(proj_s[:, D_Q:D_Q + GROUP_W], cos, sin)
        rk_s[...] = _rotary(proj_s[:, D_K:D_K + GROUP_W], cos, sin) * QK_SCALE
        rv_s[...] = keep(proj_s[:, D_V:D_V + GROUP_W])
        live_rows = jnp.clip(jnp.minimum(row_in_chunk + 1, valid - (row - row_in_chunk)), 0, C)
        rg_s[...] = live_rows.astype(F32) * _prow(prm_ref, R_RLOGG)
        yield

    ii = _iota((C, GROUP_W), 0)
    jj = _iota((C, GROUP_W), 1) & (C - 1)
    tri = jj <= ii
    strict = jj < ii
    eye = jj == ii
    eye_f = jnp.where(eye, 1.0, 0.0)
    sq_mask = (ri >> 6) == (ci >> 6)
    grp_mask = (ri >> 7) == (ci >> 7)
    bd_mask = ones64
    grp_row_mask = ones128

    def decay_parts(G):
        g_row = jnp.sum(jnp.where(eye, G, 0.0), axis=0, keepdims=True)
        dm = jnp.where(tri, jnp.exp(jnp.minimum(G - g_row, 0.0)), 0.0)
        g_last = G[C - 1:C, :]
        return dm, g_last

    def hgrn_a(c):
        sl = pl.ds(c * C, C)
        G = hg_s[sl, :]
        mid = hg_s[pl.ds(c * C + C // 2 - 1, 1), :]
        qf = hq_s[sl, :] * jnp.exp(G - mid)
        kf = hk_s[sl, :] * jnp.exp(mid - G)
        scores = jnp.where(tri, _dot_nt(qf, _block_diag(kf, bd_mask)), 0.0)
        yield
        o_s[sl, 0:256] = _dot(scores, _block_diag(hv_s[sl, :], bd_mask))
        yield

    def hgrn_a_any_decay(c, carry):
        sl = pl.ds(pl.multiple_of(c * C, C), C)
        q = hq_s[sl, :]
        k = hk_s[sl, :]
        scores = jnp.where(eye, _dot_nt(q, _block_diag(k, bd_mask)), 0.0)
        s = C // 2
        lvl = 0
        while s >= 1:
            e = he_s[lvl, sl, :]
            upper = (ii & s) != 0
            qs = jnp.where(upper, q * e, 0.0)
            ks = jnp.where(upper, 0.0, k * e)
            sc = _dot_nt(qs, _block_diag(ks, bd_mask))
            same_block = (ii & -(2 * s)) == (jj & -(2 * s))
            scores = scores + jnp.where(same_block, sc, 0.0)
            s //= 2
            lvl += 1
        o_s[sl, 0:256] = _dot(scores, _block_diag(hv_s[sl, :], bd_mask))
        return carry

    def gdn_a(c):
        sl = pl.ds(c * C, C)
        q = gq_s[sl, :]
        k = gk_s[sl, :]
        G = gg_s[sl, :]
        beta = gb_s[sl, :]
        dm, _ = decay_parts(G)
        kq = _dot_nt(jnp.concatenate([k, q], axis=0), _block_diag(k, bd_mask))
        yield
        m = jnp.where(strict, kq[0:C] * dm * beta, 0.0)
        gqk_s[sl, :] = kq[C:2 * C] * dm
        p = _dot(m, _block_diag(m, bd_mask))
        yield
        tinv = eye_f - m
        n = 2
        while 2 * n < C:
            both = _dot(jnp.concatenate([p, tinv], axis=0), _block_diag(p, bd_mask))
            yield
            p = both[0:C]
            tinv = tinv + both[C:2 * C]
            n *= 2
        tinv = tinv + _dot(tinv, _block_diag(p, bd_mask))
        yield
        kb = k * beta * jnp.exp(G)
        rhs = jnp.concatenate([_block_diag(gv_s[sl, :] * beta, bd_mask), _block_diag(kb, bd_mask)], axis=1)
        sol = _dot(tinv, rhs)
        gsv_s[sl, :] = sol[:, 0:256]
        gsk_s[sl, :] = sol[:, 256:512]
        yield

    def ssd_a(c):
        sl = pl.ds(c * C, C)
        bm = sb_s[sl, :]
        dm, _ = decay_parts(sg_s[sl, :])
        cb = _dot_nt(sc_s[sl, :], _block_diag(bm, grp_row_mask))
        yield
        o_s[sl, 512:768] = _dot(cb * dm, _block_diag(sv_s[sl, :], bd_mask))
        yield

    def ret_a(c):
        sl = pl.ds(c * C, C)
        dm, _ = decay_parts(rg_s[sl, :])
        sc = _dot_nt(rq_s[sl, :], _block_diag(rk_s[sl, :], bd_mask)) * dm
        yield
        o_s[sl, 768:1024] = _dot(sc, _block_diag(rv_s[sl, :], bd_mask))
        yield

    def hgrn_b(c):
        sl = pl.ds(c * C, C)
        G = hg_s[sl, :]
        g_last = G[C - 1:C, :]
        st = st_s[0]
        o_s[sl, 0:256] += _dot_nt(hq_s[sl, :] * jnp.exp(G), st)
        yield
        kt = hk_s[sl, :] * jnp.exp(g_last - G)
        st_s[0] = st * jnp.exp(g_last) + jnp.where(sq_mask, _dot_tn(hv_s[sl, :], kt), 0.0)
        yield

    def gdn_b(c):
        sl = pl.ds(c * C, C)
        G = gg_s[sl, :]
        g_last = G[C - 1:C, :]
        st = st_s[1]
        tmp = _dot(jnp.concatenate([gsk_s[sl, :], gq_s[sl, :] * jnp.exp(G)], axis=0), st)
        yield
        u = gsv_s[sl, :] - tmp[0:C]
        kt = gk_s[sl, :] * jnp.exp(g_last - G)
        st_s[1] = st * jnp.exp(g_last) + jnp.where(sq_mask, _dot_tn(kt, u), 0.0)
        yield
        o_s[sl, 256:512] = tmp[C:2 * C] + _dot(gqk_s[sl, :], _block_diag(u, bd_mask))
        yield

    def ssd_b(c):
        sl = pl.ds(c * C, C)
        G = sg_s[sl, :]
        g_last = G[C - 1:C, :]
        st = st_s[2]
        o_s[sl, 512:768] += jnp.exp(G) * _dot(sc_s[sl, :], st)
        yield
        vt = sv_s[sl, :] * jnp.exp(g_last - G)
        st_s[2] = st * jnp.exp(g_last) + jnp.where(grp_mask, _dot_tn(sb_s[sl, :], vt), 0.0)
        yield

    def ret_b(c):
        sl = pl.ds(c * C, C)
        G = rg_s[sl, :]
        g_last = G[C - 1:C, :]
        st = st_s[3]
        o_s[sl, 768:1024] += jnp.exp(G) * _dot(rq_s[sl, :], st)
        yield
        vt = rv_s[sl, :] * jnp.exp(g_last - G)
        st_s[3] = st * jnp.exp(g_last) + jnp.where(sq_mask, _dot_tn(rk_s[sl, :], vt), 0.0)
        yield

    def post(c):
        sl = pl.ds(c * C, C)
        y_s[sl, :] = _post_mix(o_s[sl, 0:256], o_s[sl, 256:512], o_s[sl, 512:768], o_s[sl, 768:1024],
                               proj_s[sl, A_Z:A_Z + GROUP_W], proj_s[sl, B_Z:B_Z + GROUP_W],
                               proj_s[sl, C_Z:C_Z + GROUP_W], proj_s[sl, D_Z:D_Z + GROUP_W],
                               sx_s[sl, :], prm_ref, ones64, ones128).astype(BF16)
        yield

    chunks = range(n_chunks)
    every = lambda *gens: itertools.chain.from_iterable(gens)
    z_cols = [(z, z + GROUP_W) for z in (A_Z, B_Z, C_Z, D_Z)]
    _interleave([every(project((B_QKV, B_Z), (SMALL, NP)), gdn_prep())], project((A_Q, A_Z)), 1)
    _interleave([gdn_a(c) for c in chunks],
                every(hgrn_prep(), project((D_Q, D_Z)), *[hgrn_a(c) for c in chunks],
                      ret_prep(), project((C_XBC, C_Z)), *[ret_a(c) for c in chunks],
                      ssd_prep(), project(*z_cols), *[ssd_a(c) for c in chunks]),
                FILL_PER_ROUND)

    @pl.when(jnp.max(hspan_s[0:1, :]) > HGRN_SAFE_SPAN)
    def _():
        hgrn_level_exps()
        lax.fori_loop(0, n_chunks, hgrn_a_any_decay, 0)

    for c in chunks:
        done = [post(c - 1)] if c else []
        _interleave([gdn_b(c)], every(hgrn_b(c), ssd_b(c), ret_b(c), *done), 2)
    _interleave([post(n_chunks - 1)], (), 0)

    out = h_ref[0] + jnp.dot(y_s[...], wout_ref[0], preferred_element_type=F32)
    if final:
        out = _rms(out) * _prow(prm_ref, R_FINAL, D_MODEL)
    hout_ref[0] = out

    @pl.when(t == nt - 1)
    def _():
        sout_ref[0] = st_s[...]
        ctxout_ref[0, 0] = cbuf_g[valid:valid + 8, :]
        ctxout_ref[0, 1] = cbuf_s[valid:valid + 8, :]
        hgrn = jnp.transpose(st_s[0])
        gdn = st_s[1]
        ssd = st_s[2]
        ret = st_s[3]
        for hd in range(N_HEADS):
            lo, hi = hd * HEAD_DIM, (hd + 1) * HEAD_DIM
            hgo_ref[0, hd] = hgrn[lo:hi, lo:hi]
            gdo_ref[0, hd] = gdn[lo:hi, lo:hi]
            sdo_ref[0, hd] = ssd[(hd // 2) * 128:(hd // 2 + 1) * 128, lo:hi]
            rto_ref[0, hd] = ret[lo:hi, lo:hi]


def _prompt_layer(layer, h, cos, sin, win, wout, prm, expand, s0, ctx0, *, tile, valid, final):
    bsz, seq, _ = h.shape
    nt = seq // tile
    kern = functools.partial(_prompt_kernel, layer, tile, valid, final)
    const2 = lambda b, t: (0, 0)
    slab = lambda: pltpu.VMEM((tile, GROUP_W), F32)
    sq_state = (N_HEADS, HEAD_DIM, HEAD_DIM)
    head_states = [sq_state, sq_state, (N_HEADS, 128, HEAD_DIM), sq_state]
    return pl.pallas_call(
        kern,
        grid=(bsz, nt),
        in_specs=[
            pl.BlockSpec((1, tile, D_MODEL), lambda b, t: (b, t, 0)),
            pl.BlockSpec((tile, GROUP_W), lambda b, t: (t, 0)),
            pl.BlockSpec((tile, GROUP_W), lambda b, t: (t, 0)),
            pl.BlockSpec((1, D_MODEL, NP), lambda b, t: (layer, 0, 0)),
            pl.BlockSpec((1, D_MODEL, D_MODEL), lambda b, t: (layer, 0, 0)),
            pl.BlockSpec((1, N_PRM, D_MODEL), lambda b, t: (layer, 0, 0)),
            pl.BlockSpec(expand.shape, const2),
            pl.BlockSpec((4, GROUP_W, GROUP_W), lambda b, t: (0, 0, 0)),
            pl.BlockSpec((2, 8, CONV_DIM), lambda b, t: (0, 0, 0)),
        ],
        out_specs=[
            pl.BlockSpec((1, tile, D_MODEL), lambda b, t: (b, t, 0)),
            pl.BlockSpec((1, 4, GROUP_W, GROUP_W), lambda b, t: (b, 0, 0, 0)),
            pl.BlockSpec((1, 2, 8, CONV_DIM), lambda b, t: (b, 0, 0, 0)),
        ] + [pl.BlockSpec((1,) + s, lambda b, t: (b, 0, 0, 0)) for s in head_states],
        out_shape=[
            jax.ShapeDtypeStruct((bsz, seq, D_MODEL), F32),
            jax.ShapeDtypeStruct((bsz, 4, GROUP_W, GROUP_W), F32),
            jax.ShapeDtypeStruct((bsz, 2, 8, CONV_DIM), F32),
        ] + [jax.ShapeDtypeStruct((bsz,) + s, F32) for s in head_states],
        scratch_shapes=[
            pltpu.VMEM((tile, NP), F32),
            pltpu.VMEM((tile + 8, CONV_DIM), F32),
            pltpu.VMEM((tile + 8, CONV_DIM), F32),
            pltpu.VMEM((4, GROUP_W, GROUP_W), F32),
            pltpu.VMEM((6, tile, GROUP_W), F32),
        ] + [slab() for _ in range(22)] + [pltpu.VMEM((tile, D_MODEL), F32), pltpu.VMEM((tile, D_MODEL), BF16),
                                             pltpu.VMEM((tile, D_MODEL), BF16), pltpu.VMEM((8, GROUP_W), F32)],
        compiler_params=pltpu.CompilerParams(
            dimension_semantics=("arbitrary", "arbitrary"), vmem_limit_bytes=VMEM_LIMIT),
        name=f"prompt_layer_t{tile}",
    )(h, cos, sin, win, wout, prm, expand, s0, ctx0)


(VT_HQ, VT_HK, VT_HF, VT_HV, VT_GQ, VT_GK, VT_GV, VT_GDEC, VT_GBETA,
 VT_SV, VT_SB, VT_SC, VT_SDEC, VT_RQ, VT_RK, VT_RV, VT_RDEC) = [GROUP_W * i for i in range(17)]
N_VT = 17 * GROUP_W
DEC_VBLOCK = 32


def _dec_pre_kernel(layer, hs_ref, cos_ref, sin_ref, win_ref, prm_ref, exp_ref, gctx_ref, sctx_ref,
                    vt_ref, post_ref, ug_ref, us_ref):
    x = hs_ref[...]
    nb = x.shape[0]
    hn = _rms(x) * _prow(prm_ref, R_NORM, D_MODEL)
    proj = jnp.dot(hn.astype(BF16), win_ref[0], preferred_element_type=F32)

    ri = _iota((GROUP_W, GROUP_W), 0)
    ci = _iota((GROUP_W, GROUP_W), 1)
    ones64 = jnp.where((ri >> 6) == (ci >> 6), 1.0, 0.0).astype(BF16)

    def put(off, a):
        vt_ref[off:off + GROUP_W, :] = jnp.transpose(a)

    lb = _hgrn_lower_bound(_prow(prm_ref, R_LB, n=DEPTH), layer)
    q, k, f, v = _hgrn_inputs(proj, lb)
    put(VT_HQ, q)
    put(VT_HK, k)
    put(VT_HF, f)
    put(VT_HV, v)

    log_decay, vals = _compact_gates(proj[:, SMALL:SMALL + 128], prm_ref)
    decay = jnp.exp(_dot_exact_rhs(log_decay, exp_ref[:, 0:512]))
    vals = _dot_exact_rhs(vals, exp_ref[:, 512:1024])

    u = proj[:, B_QKV:B_QKV + CONV_DIM]
    ug_ref[...] = u
    cw = _prow(prm_ref, R_GCONV, CONV_DIM, CONV_W)
    conv = cw[3:4] * u + cw[2:3] * gctx_ref[0, 2] + cw[1:2] * gctx_ref[0, 1] + cw[0:1] * gctx_ref[0, 0]
    act = _silu(conv)
    qn, kn = _l2n_pair(act[:, 0:256], act[:, 256:512], ones64)
    put(VT_GQ, qn * QK_SCALE)
    put(VT_GK, kn)
    put(VT_GV, act[:, 512:768])
    put(VT_GDEC, decay[:, 0:256])
    put(VT_GBETA, vals[:, 0:256])

    u = proj[:, C_XBC:C_XBC + CONV_DIM]
    us_ref[...] = u
    cw = _prow(prm_ref, R_SCONV, CONV_DIM, CONV_W)
    conv = (cw[3:4] * u + cw[2:3] * sctx_ref[0, 2] + cw[1:2] * sctx_ref[0, 1] + cw[0:1] * sctx_ref[0, 0]
            + _prow(prm_ref, R_SCONVB, CONV_DIM))
    act = _silu(conv)
    post_ref[:, POST_SX:POST_SX + 256] = act[:, 0:256]
    put(VT_SV, act[:, 0:256] * vals[:, 256:512])
    put(VT_SB, act[:, 256:512])
    put(VT_SC, act[:, 512:768])
    put(VT_SDEC, decay[:, 256:512])

    cos = cos_ref[...]
    sin = sin_ref[...]
    put(VT_RQ, _rotary(proj[:, D_Q:D_Q + GROUP_W], cos, sin))
    put(VT_RK, _rotary(proj[:, D_K:D_K + GROUP_W], cos, sin) * QK_SCALE)
    put(VT_RV, proj[:, D_V:D_V + GROUP_W])
    put(VT_RDEC, jnp.broadcast_to(jnp.exp(_prow(prm_ref, R_RLOGG)), (nb, GROUP_W)))

    post_ref[:, POST_AZ:POST_AZ + 256] = proj[:, A_Z:A_Z + GROUP_W]
    post_ref[:, POST_BZ:POST_BZ + 256] = proj[:, B_Z:B_Z + GROUP_W]
    post_ref[:, POST_CZ:POST_CZ + 256] = proj[:, C_Z:C_Z + GROUP_W]
    post_ref[:, POST_DZ:POST_DZ + 256] = proj[:, D_Z:D_Z + GROUP_W]


def _dec_pre(layer, hs, cos, sin, win, prm, expand, gctx, sctx):
    nb = hs.shape[0]
    full = lambda a: pl.BlockSpec(a.shape, lambda i: (0,) * a.ndim)
    shapes = [(N_VT, nb), (nb, N_POST), (nb, CONV_DIM), (nb, CONV_DIM)]
    return pl.pallas_call(
        functools.partial(_dec_pre_kernel, layer),
        grid=(1,),
        in_specs=[
            full(hs), full(cos), full(sin),
            pl.BlockSpec((1, D_MODEL, NP), lambda i: (layer, 0, 0)),
            pl.BlockSpec((1, N_PRM, D_MODEL), lambda i: (layer, 0, 0)),
            full(expand),
            pl.BlockSpec((1, 3, nb, CONV_DIM), lambda i: (layer, 0, 0, 0)),
            pl.BlockSpec((1, 3, nb, CONV_DIM), lambda i: (layer, 0, 0, 0)),
        ],
        out_specs=[pl.BlockSpec(s, lambda i: (0, 0)) for s in shapes],
        out_shape=[jax.ShapeDtypeStruct(s, F32) for s in shapes],
        compiler_params=pltpu.CompilerParams(dimension_semantics=("arbitrary",), vmem_limit_bytes=VMEM_LIMIT),
        name="decode_pre",
    )(hs, cos, sin, win, prm, expand, gctx, sctx)


def _dec_state_kernel(vt_ref, hg_ref, gd_ref, sd_ref, rt_ref, hg_acc, gd_acc, rt_acc,
                      o_ref, hg_out, gd_out, sd_out, rt_out):
    del hg_acc, gd_acc, rt_acc
    h = pl.program_id(0)
    j = pl.program_id(1)
    vb = DEC_VBLOCK
    nb = vt_ref.shape[1]
    head = h * HEAD_DIM

    def row(off, i):
        return vt_ref[pl.ds(off + i, 1), :]

    def slab(off):
        return vt_ref[pl.ds(pl.multiple_of(off + head + j * vb, 8), vb), :]

    zero = jnp.zeros((vb, nb), F32)

    v = slab(VT_HV)

    def hgrn_step(k, o):
        s = row(VT_HF + head, k) * hg_ref[0, 0, k] + row(VT_HK + head, k) * v
        hg_out[0, 0, k] = s
        return o + row(VT_HQ + head, k) * s

    o_ref[0, 0] = lax.fori_loop(0, HEAD_DIM, hgrn_step, zero, unroll=4)

    v = slab(VT_GV)
    dec = row(VT_GDEC + head, 0)
    ks = lax.fori_loop(0, HEAD_DIM, lambda k, a: a + row(VT_GK + head, k) * gd_ref[0, 0, k], zero, unroll=4)
    u = row(VT_GBETA + head, 0) * (v - ks * dec)

    def gdn_step(k, o):
        s = gd_ref[0, 0, k] * dec + row(VT_GK + head, k) * u
        gd_out[0, 0, k] = s
        return o + row(VT_GQ + head, k) * s

    o_ref[1, 0] = lax.fori_loop(0, HEAD_DIM, gdn_step, zero, unroll=4)

    v = slab(VT_SV)
    dec = row(VT_SDEC + head, 0)
    grp = (h >> 1) * 128

    def ssd_step(n, o):
        s = sd_ref[0, 0, n] * dec + row(VT_SB + grp, n) * v
        sd_out[0, 0, n] = s
        return o + row(VT_SC + grp, n) * s

    o_ref[2, 0] = lax.fori_loop(0, 128, ssd_step, zero, unroll=4)

    v = slab(VT_RV)
    dec = row(VT_RDEC + head, 0)

    def ret_step(k, o):
        s = rt_ref[0, 0, k] * dec + row(VT_RK + head, k) * v
        rt_out[0, 0, k] = s
        return o + row(VT_RQ + head, k) * s

    o_ref[3, 0] = lax.fori_loop(0, HEAD_DIM, ret_step, zero, unroll=4)


def _dec_state(layer, vt, states, accs):
    nb = vt.shape[1]
    vb = DEC_VBLOCK
    blk = lambda a: pl.BlockSpec((1, 1, a.shape[2], vb, nb), lambda h, j: (layer, h, 0, j, 0))
    return pl.pallas_call(
        _dec_state_kernel,
        grid=(N_HEADS, HEAD_DIM // vb),
        in_specs=[pl.BlockSpec(vt.shape, lambda h, j: (0, 0))] + [blk(a) for a in states]
        + [pl.BlockSpec(memory_space=pl.ANY)] * 3,
        out_specs=[pl.BlockSpec((4, 1, vb, nb), lambda h, j: (0, h, j, 0))] + [blk(a) for a in states],
        out_shape=[jax.ShapeDtypeStruct((4, N_HEADS, HEAD_DIM, nb), F32)]
        + [jax.ShapeDtypeStruct(a.shape, F32) for a in states],
        input_output_aliases={3: 3, 5: 1, 6: 2, 7: 4},
        compiler_params=pltpu.CompilerParams(dimension_semantics=("arbitrary", "arbitrary"),
                                             vmem_limit_bytes=VMEM_LIMIT),
        name="decode_state",
    )(vt, *states, *accs)


def _dec_post_kernel(final, hs_ref, ot_ref, post_ref, wout_ref, prm_ref, out_ref):
    ri = _iota((GROUP_W, GROUP_W), 0)
    ci = _iota((GROUP_W, GROUP_W), 1)
    ones64 = jnp.where((ri >> 6) == (ci >> 6), 1.0, 0.0).astype(BF16)
    ones128 = jnp.where((ri >> 7) == (ci >> 7), 1.0, 0.0).astype(BF16)
    gate = lambda off: post_ref[:, off:off + GROUP_W]
    o = [jnp.transpose(ot_ref[m * GROUP_W:(m + 1) * GROUP_W, :]) for m in range(4)]
    y = _post_mix(o[0], o[1], o[2], o[3],
                  gate(POST_AZ), gate(POST_BZ), gate(POST_CZ), gate(POST_DZ), gate(POST_SX),
                  prm_ref, ones64, ones128)
    out = hs_ref[...] + jnp.dot(y.astype(BF16), wout_ref[0], preferred_element_type=F32)
    if final:
        out = _rms(out) * _prow(prm_ref, R_FINAL, D_MODEL)
    out_ref[...] = out


def _dec_post(layer, hs, ot, post, wout, prm, final):
    nb = hs.shape[0]
    full = lambda a: pl.BlockSpec(a.shape, lambda i: (0,) * a.ndim)
    return pl.pallas_call(
        functools.partial(_dec_post_kernel, final),
        grid=(1,),
        in_specs=[full(hs), full(ot), full(post),
                  pl.BlockSpec((1, D_MODEL, D_MODEL), lambda i: (layer, 0, 0)),
                  pl.BlockSpec((1, N_PRM, D_MODEL), lambda i: (layer, 0, 0))],
        out_specs=pl.BlockSpec((nb, D_MODEL), lambda i: (0, 0)),
        out_shape=jax.ShapeDtypeStruct((nb, D_MODEL), F32),
        compiler_params=pltpu.CompilerParams(dimension_semantics=("arbitrary",), vmem_limit_bytes=VMEM_LIMIT),
        name="decode_post",
    )(hs, ot, post, wout, prm)


POST_AZ, POST_BZ, POST_CZ, POST_DZ, POST_SX = [GROUP_W * i for i in range(5)]
N_POST = 5 * GROUP_W


def _expand_matrix():
    e = np.zeros((128, 4 * GROUP_W), np.float32)
    for block, first_lane in enumerate((0, 2 * N_HEADS, N_HEADS, 2 * N_HEADS)):
        for h in range(N_HEADS):
            lo = block * GROUP_W + h * HEAD_DIM
            e[first_lane + h, lo:lo + HEAD_DIM] = 1.0
    return jnp.asarray(e, BF16)


def _relayout_w_in(w_in):
    sizes = (256, 256, 256, 256, 768, 256, 4, 4, 768, 256, 4, 256, 256, 256, 256)
    offs = np.concatenate([[0], np.cumsum(sizes)])
    w_t = jnp.transpose(w_in, (2, 0, 1))
    seg = lambda i: w_t[offs[i]:offs[i + 1]]
    pad = jnp.zeros((128 - 12,) + w_t.shape[1:], w_in.dtype)
    rows = [seg(0), seg(1), seg(2), seg(3), seg(4), seg(5), seg(8), seg(9), seg(11), seg(12), seg(13), seg(14),
            seg(6), seg(7), seg(10), pad]
    return jnp.transpose(jnp.concatenate(rows, axis=0), (1, 2, 0)).astype(BF16)


def _pack_params(norm_w, hgrn_lb_logits, hgrn_norm_w, gdn_conv_w, gdn_a_log, gdn_dt_bias, gdn_norm_w,
                 ssd_conv_w, ssd_conv_b, ssd_a_log, ssd_dt_bias, ssd_d, ssd_norm_w, ret_norm_w, ret_norm_b,
                 final_norm_w):
    depth = norm_w.shape[0]
    prm = jnp.zeros((depth, N_PRM, D_MODEL), F32)

    def put(p, r, a):
        a = a.astype(F32)
        if a.ndim == 2:
            a = a[:, None, :]
        return p.at[:, r:r + a.shape[1], 0:a.shape[2]].set(a)

    rep = lambda a: jnp.repeat(a, HEAD_DIM, axis=-1)
    prm = put(prm, R_NORM, norm_w)
    prm = put(prm, R_LB, jnp.broadcast_to(hgrn_lb_logits[None], (depth,) + hgrn_lb_logits.shape))
    prm = put(prm, R_HNORM, hgrn_norm_w)
    prm = put(prm, R_GCONV, gdn_conv_w)
    prm = put(prm, R_GNORM, gdn_norm_w)
    prm = put(prm, R_SCONV, ssd_conv_w)
    prm = put(prm, R_SCONVB, ssd_conv_b)
    prm = put(prm, R_SD, rep(ssd_d))
    gap = jnp.zeros((depth, N_HEADS), F32)
    prm = put(prm, R_CBIAS, jnp.concatenate([gdn_dt_bias.astype(F32), gap, ssd_dt_bias.astype(F32)], axis=-1))
    prm = put(prm, R_CALOG, jnp.concatenate([gdn_a_log.astype(F32), gap, ssd_a_log.astype(F32)], axis=-1))
    prm = put(prm, R_SNORM, ssd_norm_w)
    prm = put(prm, R_RNORM, ret_norm_w)
    prm = put(prm, R_RNORMB, ret_norm_b)
    ret_logg = jnp.log1p(-jnp.exp2(-5.0 - jnp.arange(N_HEADS, dtype=F32)))
    prm = put(prm, R_RLOGG, jnp.broadcast_to(rep(ret_logg)[None], (depth, GROUP_W)))
    prm = put(prm, R_FINAL, jnp.broadcast_to(final_norm_w[None], (depth, D_MODEL)))
    return prm


def _rope_tables(pos):
    half = HEAD_DIM // 2
    inv_freq = 1.0 / (ROPE_BASE ** jnp.linspace(0.0, 1.0, half, dtype=F32))
    ang = pos[:, None] * inv_freq[None, :]
    cos, sin = jnp.cos(ang), jnp.sin(ang)
    cos_h = jnp.concatenate([cos, cos], axis=-1)
    sin_h = jnp.concatenate([-sin, sin], axis=-1)
    return jnp.tile(cos_h, (1, N_HEADS)), jnp.tile(sin_h, (1, N_HEADS))


def kernel(x_prompt, x_sample, state_hgrn, state_gdn, state_gdn_conv, state_ssd, state_ssd_conv, state_ret,
           meta_tokens, norm_w, w_in, hgrn_lb_logits, hgrn_norm_w, gdn_conv_w, gdn_a_log, gdn_dt_bias, gdn_norm_w,
           ssd_conv_w, ssd_conv_b, ssd_a_log, ssd_dt_bias, ssd_d, ssd_norm_w, ret_norm_w, ret_norm_b,
           w_out, final_norm_w):
    depth = w_in.shape[0]
    bsz, seq, _ = x_prompt.shape
    nb = x_sample.shape[0]
    tile = PROMPT_TILE if seq % PROMPT_TILE == 0 else CHUNK

    win = _relayout_w_in(w_in)
    wout = w_out.astype(BF16)
    prm = _pack_params(norm_w, hgrn_lb_logits, hgrn_norm_w, gdn_conv_w, gdn_a_log, gdn_dt_bias, gdn_norm_w,
                       ssd_conv_w, ssd_conv_b, ssd_a_log, ssd_dt_bias, ssd_d, ssd_norm_w, ret_norm_w, ret_norm_b,
                       final_norm_w)
    expand = _expand_matrix()

    cos_m, sin_m = _rope_tables(jnp.arange(CHUNK, dtype=F32))
    cos_p, sin_p = _rope_tables(N_META + jnp.arange(seq, dtype=F32))
    cos_s, sin_s = _rope_tables(PAST_LEN + jnp.arange(1, dtype=F32))

    hm = jnp.zeros((1, CHUNK, D_MODEL), F32).at[0, :N_META].set(meta_tokens.astype(F32))
    hp = x_prompt.astype(F32)
    hs = x_sample.astype(F32)[:, 0, :]
    zero_s = jnp.zeros((4, GROUP_W, GROUP_W), F32)
    zero_ctx = jnp.zeros((2, 8, CONV_DIM), F32)

    gctx = jnp.swapaxes(state_gdn_conv.astype(F32), 1, 2)
    sctx = jnp.swapaxes(state_ssd_conv.astype(F32), 1, 2)

    seq_minor = lambda a: jnp.transpose(a.astype(F32), (0, 2, 3, 4, 1))
    dec_states = [seq_minor(a) for a in (state_hgrn, state_gdn, state_ssd, state_ret)]
    dec_new = [jnp.zeros_like(dec_states[0]), jnp.zeros_like(dec_states[1]), dec_states[2],
               jnp.zeros_like(dec_states[3])]

    p_states, p_ctx = [], []
    s_gctx, s_sctx = [], []
    for l in range(depth):
        final = l == depth - 1
        hm, sm, cm, *_ = _prompt_layer(l, hm, cos_m, sin_m, win, wout, prm, expand, zero_s, zero_ctx,
                                       tile=CHUNK, valid=N_META, final=False)
        hp, _, cp, *head_states = _prompt_layer(l, hp, cos_p, sin_p, win, wout, prm, expand, sm[0], cm[0],
                                                tile=tile, valid=tile, final=final)
        p_states.append(head_states)
        p_ctx.append(cp)

        vt, post, ug, us = _dec_pre(l, hs, cos_s, sin_s, win, prm, expand, gctx, sctx)
        ot, *dec_new = _dec_state(l, vt, [dec_states[0], dec_states[1], dec_new[2], dec_states[3]],
                                  [dec_new[0], dec_new[1], dec_new[3]])
        hs = _dec_post(l, hs, ot.reshape(4 * GROUP_W, nb), post, wout, prm, final)
        s_gctx.append(jnp.concatenate([state_gdn_conv[l, :, 1:].astype(F32), ug[:, None, :]], axis=1))
        s_sctx.append(jnp.concatenate([state_ssd_conv[l, :, 1:].astype(F32), us[:, None, :]], axis=1))

    ph, pg, ps, pr = [jnp.stack([layer_states[i] for layer_states in p_states]) for i in range(4)]
    cp = jnp.stack(p_ctx)
    seq_major = lambda a: jnp.transpose(a, (0, 4, 1, 2, 3))
    return (hp.astype(x_prompt.dtype), hs[:, None, :].astype(x_sample.dtype),
            ph.astype(state_hgrn.dtype), pg.astype(state_gdn.dtype),
            cp[:, :, 0, 8 - (CONV_W - 1):].astype(state_gdn_conv.dtype),
            ps.astype(state_ssd.dtype),
            cp[:, :, 1, 8 - (CONV_W - 1):].astype(state_ssd_conv.dtype),
            pr.astype(state_ret.dtype),
            seq_major(dec_new[0]).astype(state_hgrn.dtype), seq_major(dec_new[1]).astype(state_gdn.dtype),
            jnp.stack(s_gctx).astype(state_gdn_conv.dtype), seq_major(dec_new[2]).astype(state_ssd.dtype),
            jnp.stack(s_sctx).astype(state_ssd_conv.dtype), seq_major(dec_new[3]).astype(state_ret.dtype))
```

```python
import functools
import itertools

import numpy as np
import jax
import jax.numpy as jnp
from jax import lax
from jax.experimental import pallas as pl
from jax.experimental.pallas import tpu as pltpu

F32 = jnp.float32
BF16 = jnp.bfloat16

D_MODEL = 1024
GROUP_W = 256
HEAD_DIM = 64
N_HEADS = 4
CHUNK = 64
CONV_W = 4
CONV_DIM = 768
N_META = 16
PAST_LEN = 16384
ROPE_BASE = 10000.0
EPS = 1e-6
TINY = 1e-30
QK_SCALE = HEAD_DIM ** -0.5
DEPTH = 4

A_Q, A_F, A_I, A_Z = 0, 256, 512, 768
B_QKV, B_Z = 1024, 1792
C_XBC, C_Z = 2048, 2816
D_Q, D_K, D_V, D_Z = 3072, 3328, 3584, 3840
SMALL = 4096
NP = 4224

R_NORM, R_LB, R_HNORM = 0, 1, 5
R_GCONV, R_GNORM = 6, 12
R_SCONV, R_SCONVB, R_SD, R_SNORM = 13, 17, 20, 21
R_RNORM, R_RNORMB, R_RLOGG, R_FINAL = 22, 23, 24, 25
R_CBIAS, R_CALOG = 26, 27
N_PRM = 32

VMEM_LIMIT = 56 * 1024 * 1024
PROMPT_TILE = 256
HGRN_SAFE_SPAN = 80.0
FILL_PER_STATE_STEP = 3
FILL_PER_ROUND = 4
_DONE = object()


def _interleave(leads, fillers, fill_per_round):
    leads = list(leads)
    fillers = iter(fillers)
    fill_live = fill_per_round > 0
    while leads or fill_live:
        leads = [g for g in leads if next(g, _DONE) is not _DONE]
        for _ in range(fill_per_round):
            if fill_live and next(fillers, _DONE) is _DONE:
                fill_live = False


def _dot(a, b):
    return jnp.dot(a.astype(BF16), b.astype(BF16), preferred_element_type=F32)


def _dot_nt(a, b):
    return lax.dot_general(a.astype(BF16), b.astype(BF16), (((1,), (1,)), ((), ())),
                           preferred_element_type=F32)


def _dot_tn(a, b):
    return lax.dot_general(a.astype(BF16), b.astype(BF16), (((0,), (0,)), ((), ())),
                           preferred_element_type=F32)


def _split3(x):
    hi = x.astype(BF16)
    r1 = x - hi.astype(F32)
    mid = r1.astype(BF16)
    lo = (r1 - mid.astype(F32)).astype(BF16)
    return hi, mid, lo


def _dot_exact_rhs(x, m):
    hi, mid, lo = _split3(x)
    return (jnp.dot(hi, m, preferred_element_type=F32) + jnp.dot(mid, m, preferred_element_type=F32)
            + jnp.dot(lo, m, preferred_element_type=F32))


def _dot_exact_lhs(m, x):
    hi, mid, lo = _split3(x)
    return (jnp.dot(m, hi, preferred_element_type=F32) + jnp.dot(m, mid, preferred_element_type=F32)
            + jnp.dot(m, lo, preferred_element_type=F32))


def _sigmoid(x):
    return jax.nn.sigmoid(x)


def _silu(x):
    return x * _sigmoid(x)


def _softplus(x):
    return jnp.maximum(x, 0.0) + jnp.log(1.0 + jnp.exp(-jnp.abs(x)))


def _iota(shape, dim):
    return lax.broadcasted_iota(jnp.int32, shape, dim)


def _block_diag(x, mask01):
    xb = x.astype(BF16)
    return jnp.concatenate([xb] * N_HEADS, axis=0) * mask01


def _chunk_cumsum(x, row_in_chunk):
    sh = 1
    while sh < CHUNK:
        r = pltpu.roll(x, sh, axis=0)
        x = x + jnp.where(row_in_chunk >= sh, r, 0.0)
        sh *= 2
    return x


def _swap_halves(x):
    lane = _iota(x.shape, 1)
    fwd = pltpu.roll(x, GROUP_W - HEAD_DIM // 2, axis=1)
    bwd = pltpu.roll(x, HEAD_DIM // 2, axis=1)
    return jnp.where((lane & (HEAD_DIM - 1)) < HEAD_DIM // 2, fwd, bwd)


def _group_sums(xs, ones_bd, squares=()):
    parts, spans = [], []
    for i, x in enumerate(xs):
        hi = x.astype(BF16)
        terms = [hi] if i in squares else [hi, (x - hi.astype(F32)).astype(BF16)]
        spans.append((len(parts), len(terms)))
        parts += terms
    sums = jnp.dot(jnp.concatenate(parts, axis=0) if len(parts) > 1 else parts[0], ones_bd,
                   preferred_element_type=F32)
    n = xs[0].shape[0]
    out = []
    for first, count in spans:
        total = sums[first * n:(first + 1) * n]
        for j in range(1, count):
            total = total + sums[(first + j) * n:(first + j + 1) * n]
        out.append(total)
    return out


def _group_sum(x, ones_bd, square=False):
    return _group_sums([x], ones_bd, squares=(0,) if square else ())[0]


def _rms(x):
    return x * lax.rsqrt(jnp.mean(x * x, axis=-1, keepdims=True) + EPS)


def _hgrn_lower_bound(logits, layer):
    m = jnp.max(logits, axis=0, keepdims=True)
    e = jnp.exp(logits - m)
    w = e / jnp.sum(e, axis=0, keepdims=True)
    cum = jnp.sum(w[0:layer + 1], axis=0, keepdims=True) - w[0:1]
    return jnp.maximum(cum, 0.0)


def _prow(prm_ref, r, width=GROUP_W, n=1):
    return prm_ref[0, r:r + n, 0:width]


def _hgrn_inputs(proj, lb):
    aq = proj[:, A_Q:A_Q + GROUP_W]
    af = proj[:, A_F:A_F + GROUP_W]
    q = _silu(aq) * QK_SCALE
    k = (1.0 - lb) * _sigmoid(-af)
    f = jnp.maximum(lb + (1.0 - lb) * _sigmoid(af), TINY)
    return q, k, f, proj[:, A_I:A_I + GROUP_W]


def _compact_gates(c, prm_ref):
    sp = _softplus(c + _prow(prm_ref, R_CBIAS, 128))
    log_decay = -jnp.exp(_prow(prm_ref, R_CALOG, 128)) * sp
    lane = _iota(c.shape, 1)
    vals = jnp.where((lane >= N_HEADS) & (lane < 2 * N_HEADS), _sigmoid(c), sp)
    return log_decay, vals


def _l2n_pair(q, k, ones64):
    sq_q, sq_k = _group_sums([q * q, k * k], ones64, squares=(0, 1))
    return q * lax.rsqrt(sq_q + EPS), k * lax.rsqrt(sq_k + EPS)


def _rotary(x, cos, sin):
    return x * cos + _swap_halves(x) * sin


def _post_mix(o_a, o_b, o_c, o_d, z_a, z_b, z_c, z_d, x_ssd, prm_ref, ones64, ones128):
    inv64 = 1.0 / HEAD_DIM
    sq_a, sq_b, sum_d = _group_sums([o_a * o_a, o_b * o_b, o_d], ones64, squares=(0, 1))
    y_a = o_a * lax.rsqrt(sq_a * inv64 + EPS) * _prow(prm_ref, R_HNORM) * _silu(z_a)
    y_b = o_b * lax.rsqrt(sq_b * inv64 + EPS) * _prow(prm_ref, R_GNORM) * _silu(z_b)
    t_c = (o_c + _prow(prm_ref, R_SD) * x_ssd) * _silu(z_c)
    y_c = (t_c * lax.rsqrt(_group_sum(t_c * t_c, ones128, square=True) * (1.0 / 128.0) + EPS)
           * _prow(prm_ref, R_SNORM))
    xc = o_d - sum_d * inv64
    y_d = ((xc * lax.rsqrt(_group_sum(xc * xc, ones64, square=True) * inv64 + EPS) * _prow(prm_ref, R_RNORM)
            + _prow(prm_ref, R_RNORMB)) * _silu(z_d))
    return jnp.concatenate([y_a, y_b, y_c, y_d], axis=-1)


def _prompt_kernel(layer, tile, valid, final,
                   h_ref, cos_ref, sin_ref, win_ref, wout_ref, prm_ref, exp_ref, s0_ref, ctx0_ref,
                   hout_ref, sout_ref, ctxout_ref, hgo_ref, gdo_ref, sdo_ref, rto_ref,
                   proj_s, cbuf_g, cbuf_s, st_s, he_s,
                   hq_s, hk_s, hv_s, hg_s,
                   gq_s, gk_s, gv_s, gg_s, gb_s, gqk_s, gsv_s, gsk_s,
                   sx_s, sv_s, sb_s, sc_s, sg_s, sdt_s,
                   rq_s, rk_s, rv_s, rg_s, o_s, y_s, hn_s, hspan_s):
    t = pl.program_id(1)
    nt = pl.num_programs(1)
    n_chunks = tile // CHUNK
    C = CHUNK

    @pl.when(t == 0)
    def _():
        st_s[...] = s0_ref[...]
        cbuf_g[0:8, :] = ctx0_ref[0]
        cbuf_s[0:8, :] = ctx0_ref[1]

    hn_s[...] = (_rms(h_ref[0]) * _prow(prm_ref, R_NORM, D_MODEL)).astype(BF16)

    def project(*col_ranges):
        for lo, hi in col_ranges:
            proj_s[:, lo:hi] = jnp.dot(hn_s[...], win_ref[0, :, lo:hi], preferred_element_type=F32)
        yield

    row = _iota((tile, GROUP_W), 0)
    row_in_chunk = row & (C - 1)
    if valid < tile:
        live = row < valid
        keep = lambda a: jnp.where(live[:, 0:a.shape[1]], a, 0.0)
    else:
        keep = lambda a: a

    ri = _iota((GROUP_W, GROUP_W), 0)
    ci = _iota((GROUP_W, GROUP_W), 1)
    ones64 = jnp.where((ri >> 6) == (ci >> 6), 1.0, 0.0).astype(BF16)
    ones128 = jnp.where((ri >> 7) == (ci >> 7), 1.0, 0.0).astype(BF16)

    def conv_taps(cbuf, u, r_w):
        cbuf[8:8 + tile, :] = u
        cw = _prow(prm_ref, r_w, CONV_DIM, CONV_W)
        ext = cbuf[...]
        acc = cw[0:1] * ext
        for tap in range(1, CONV_W):
            acc = pltpu.roll(acc, 1, axis=0) + cw[tap:tap + 1] * ext
        cbuf[0:8, :] = cbuf[tile:tile + 8, :]
        return acc[8:8 + tile, :]

    def gdn_prep():
        log_decay, vals = _compact_gates(proj_s[:, SMALL:SMALL + 128], prm_ref)
        cum = _chunk_cumsum(keep(log_decay), row_in_chunk[:, 0:128])
        cum = _dot_exact_rhs(cum, exp_ref[:, 0:512])
        gg_s[...] = cum[:, 0:256]
        sg_s[...] = cum[:, 256:512]
        vals = _dot_exact_rhs(keep(vals), exp_ref[:, 512:1024])
        gb_s[...] = vals[:, 0:256]
        sdt_s[...] = vals[:, 256:512]
        yield
        act = _silu(conv_taps(cbuf_g, proj_s[:, B_QKV:B_QKV + CONV_DIM], R_GCONV))
        qn, kn = _l2n_pair(act[:, 0:256], act[:, 256:512], ones64)
        gq_s[...] = qn * QK_SCALE
        gk_s[...] = kn
        gv_s[...] = act[:, 512:768]
        yield

    def hgrn_prep():
        lb = _hgrn_lower_bound(_prow(prm_ref, R_LB, n=DEPTH), layer)
        q, k, f, v = _hgrn_inputs(proj_s[...], lb)
        hq_s[...] = q
        hk_s[...] = k
        hv_s[...] = keep(v)
        yield
        G = _chunk_cumsum(keep(jnp.log(f)), row_in_chunk)
        hg_s[...] = G
        span = jnp.zeros((1, GROUP_W), F32)
        for c in range(n_chunks):
            mid = G[c * C + C // 2 - 1:c * C + C // 2, :]
            span = jnp.maximum(span, jnp.maximum(G[c * C:c * C + 1, :] - mid, mid - G[c * C + C - 1:c * C + C, :]))
        hspan_s[...] = jnp.broadcast_to(span, hspan_s.shape)
        yield

    def hgrn_level_exps():
        G = hg_s[...]
        last = G
        s = 1
        lvl = 5
        while s < C:
            upper = (row & s) != 0
            ref_row = jnp.where(upper, pltpu.roll(last, s, axis=0), last)
            he_s[lvl] = jnp.exp(jnp.minimum(jnp.where(upper, G - ref_row, ref_row - G), 0.0))
            last = jnp.where(upper, last, pltpu.roll(last, tile - s, axis=0))
            s *= 2
            lvl -= 1

    def ssd_prep():
        act = _silu(conv_taps(cbuf_s, proj_s[:, C_XBC:C_XBC + CONV_DIM], R_SCONV)
                    + _prow(prm_ref, R_SCONVB, CONV_DIM))
        dt = sdt_s[...]
        sx_s[...] = act[:, 0:256]
        sv_s[...] = act[:, 0:256] * dt
        sb_s[...] = act[:, 256:512]
        sc_s[...] = act[:, 512:768]
        yield

    def ret_prep():
        cos = cos_ref[...]
        sin = sin_ref[...]
        rq_s[...] = _rotary(proj_s[:, D_Q:D_Q + GROUP_W], cos, sin)
        rk_s[...] = _rotary(proj_s[:, D_K:D_K + GROUP_W], cos, sin) * QK_SCALE
        rv_s[...] = keep(proj_s[:, D_V:D_V + GROUP_W])
        live_rows = jnp.clip(jnp.minimum(row_in_chunk + 1, valid - (row - row_in_chunk)), 0, C)
        rg_s[...] = live_rows.astype(F32) * _prow(prm_ref, R_RLOGG)
        yield

    ii = _iota((C, GROUP_W), 0)
    jj = _iota((C, GROUP_W), 1) & (C - 1)
    tri = jj <= ii
    strict = jj < ii
    eye = jj == ii
    eye_f = jnp.where(eye, 1.0, 0.0)
    sq_mask = (ri >> 6) == (ci >> 6)
    grp_mask = (ri >> 7) == (ci >> 7)
    bd_mask = ones64
    grp_row_mask = ones128

    def decay_parts(G):
        g_row = jnp.sum(jnp.where(eye, G, 0.0), axis=0, keepdims=True)
        dm = jnp.where(tri, jnp.exp(jnp.minimum(G - g_row, 0.0)), 0.0)
        g_last = G[C - 1:C, :]
        return dm, g_last

    def hgrn_a(c):
        sl = pl.ds(c * C, C)
        G = hg_s[sl, :]
        mid = hg_s[pl.ds(c * C + C // 2 - 1, 1), :]
        qf = hq_s[sl, :] * jnp.exp(G - mid)
        kf = hk_s[sl, :] * jnp.exp(mid - G)
        scores = jnp.where(tri, _dot_nt(qf, _block_diag(kf, bd_mask)), 0.0)
        yield
        o_s[sl, 0:256] = _dot(scores, _block_diag(hv_s[sl, :], bd_mask))
        yield

    def hgrn_a_any_decay(c, carry):
        sl = pl.ds(pl.multiple_of(c * C, C), C)
        q = hq_s[sl, :]
        k = hk_s[sl, :]
        scores = jnp.where(eye, _dot_nt(q, _block_diag(k, bd_mask)), 0.0)
        s = C // 2
        lvl = 0
        while s >= 1:
            e = he_s[lvl, sl, :]
            upper = (ii & s) != 0
            qs = jnp.where(upper, q * e, 0.0)
            ks = jnp.where(upper, 0.0, k * e)
            sc = _dot_nt(qs, _block_diag(ks, bd_mask))
            same_block = (ii & -(2 * s)) == (jj & -(2 * s))
            scores = scores + jnp.where(same_block, sc, 0.0)
            s //= 2
            lvl += 1
        o_s[sl, 0:256] = _dot(scores, _block_diag(hv_s[sl, :], bd_mask))
        return carry

    def gdn_a(c):
        sl = pl.ds(c * C, C)
        q = gq_s[sl, :]
        k = gk_s[sl, :]
        G = gg_s[sl, :]
        beta = gb_s[sl, :]
        dm, _ = decay_parts(G)
        kq = _dot_nt(jnp.concatenate([k, q], axis=0), _block_diag(k, bd_mask))
        yield
        m = jnp.where(strict, kq[0:C] * dm * beta, 0.0)
        gqk_s[sl, :] = kq[C:2 * C] * dm
        p = _dot(m, _block_diag(m, bd_mask))
        yield
        tinv = eye_f - m
        n = 2
        while 2 * n < C:
            both = _dot(jnp.concatenate([p, tinv], axis=0), _block_diag(p, bd_mask))
            yield
            p = both[0:C]
            tinv = tinv + both[C:2 * C]
            n *= 2
        tinv = tinv + _dot(tinv, _block_diag(p, bd_mask))
        yield
        kb = k * beta * jnp.exp(G)
        rhs = jnp.concatenate([_block_diag(gv_s[sl, :] * beta, bd_mask), _block_diag(kb, bd_mask)], axis=1)
        sol = _dot(tinv, rhs)
        gsv_s[sl, :] = sol[:, 0:256]
        gsk_s[sl, :] = sol[:, 256:512]
        yield

    def ssd_a(c):
        sl = pl.ds(c * C, C)
        bm = sb_s[sl, :]
        dm, _ = decay_parts(sg_s[sl, :])
        cb = _dot_nt(sc_s[sl, :], _block_diag(bm, grp_row_mask))
        yield
        o_s[sl, 512:768] = _dot(cb * dm, _block_diag(sv_s[sl, :], bd_mask))
        yield

    def ret_a(c):
        sl = pl.ds(c * C, C)
        dm, _ = decay_parts(rg_s[sl, :])
        sc = _dot_nt(rq_s[sl, :], _block_diag(rk_s[sl, :], bd_mask)) * dm
        yield
        o_s[sl, 768:1024] = _dot(sc, _block_diag(rv_s[sl, :], bd_mask))
        yield

    def hgrn_b(c):
        sl = pl.ds(c * C, C)
        G = hg_s[sl, :]
        g_last = G[C - 1:C, :]
        st = st_s[0]
        o_s[sl, 0:256] += _dot_nt(hq_s[sl, :] * jnp.exp(G), st)
        yield
        kt = hk_s[sl, :] * jnp.exp(g_last - G)
        st_s[0] = st * jnp.exp(g_last) + jnp.where(sq_mask, _dot_tn(hv_s[sl, :], kt), 0.0)
        yield

    def gdn_b(c):
        sl = pl.ds(c * C, C)
        G = gg_s[sl, :]
        g_last = G[C - 1:C, :]
        st = st_s[1]
        tmp = _dot(jnp.concatenate([gsk_s[sl, :], gq_s[sl, :] * jnp.exp(G)], axis=0), st)
        yield
        u = gsv_s[sl, :] - tmp[0:C]
        kt = gk_s[sl, :] * jnp.exp(g_last - G)
        st_s[1] = st * jnp.exp(g_last) + jnp.where(sq_mask, _dot_tn(kt, u), 0.0)
        yield
        o_s[sl, 256:512] = tmp[C:2 * C] + _dot(gqk_s[sl, :], _block_diag(u, bd_mask))
        yield

    def ssd_b(c):
        sl = pl.ds(c * C, C)
        G = sg_s[sl, :]
        g_last = G[C - 1:C, :]
        st = st_s[2]
        o_s[sl, 512:768] += jnp.exp(G) * _dot(sc_s[sl, :], st)
        yield
        vt = sv_s[sl, :] * jnp.exp(g_last - G)
        st_s[2] = st * jnp.exp(g_last) + jnp.where(grp_mask, _dot_tn(sb_s[sl, :], vt), 0.0)
        yield

    def ret_b(c):
        sl = pl.ds(c * C, C)
        G = rg_s[sl, :]
        g_last = G[C - 1:C, :]
        st = st_s[3]
        o_s[sl, 768:1024] += jnp.exp(G) * _dot(rq_s[sl, :], st)
        yield
        vt = rv_s[sl, :] * jnp.exp(g_last - G)
        st_s[3] = st * jnp.exp(g_last) + jnp.where(sq_mask, _dot_tn(rk_s[sl, :], vt), 0.0)
        yield

    def post(c):
        sl = pl.ds(c * C, C)
        y_s[sl, :] = _post_mix(o_s[sl, 0:256], o_s[sl, 256:512], o_s[sl, 512:768], o_s[sl, 768:1024],
                               proj_s[sl, A_Z:A_Z + GROUP_W], proj_s[sl, B_Z:B_Z + GROUP_W],
                               proj_s[sl, C_Z:C_Z + GROUP_W], proj_s[sl, D_Z:D_Z + GROUP_W],
                               sx_s[sl, :], prm_ref, ones64, ones128).astype(BF16)
        yield

    chunks = range(n_chunks)
    every = lambda *gens: itertools.chain.from_iterable(gens)
    z_cols = [(z, z + GROUP_W) for z in (A_Z, B_Z, C_Z, D_Z)]
    _interleave([every(project((B_QKV, B_Z), (SMALL, NP)), gdn_prep())], project((A_Q, A_Z)), 1)
    _interleave([gdn_a(c) for c in chunks],
                every(hgrn_prep(), project((D_Q, D_Z)), *[hgrn_a(c) for c in chunks],
                      ret_prep(), project((C_XBC, C_Z)), *[ret_a(c) for c in chunks],
                      ssd_prep(), project(*z_cols), *[ssd_a(c) for c in chunks]),
                FILL_PER_ROUND)

    @pl.when(jnp.max(hspan_s[0:1, :]) > HGRN_SAFE_SPAN)
    def _():
        hgrn_level_exps()
        lax.fori_loop(0, n_chunks, hgrn_a_any_decay, 0)

    for c in chunks:
        done = [post(c - 1)] if c else []
        _interleave([gdn_b(c)], every(hgrn_b(c), ssd_b(c), ret_b(c), *done), FILL_PER_STATE_STEP)
    _interleave([post(n_chunks - 1)], (), 0)

    out = h_ref[0] + jnp.dot(y_s[...], wout_ref[0], preferred_element_type=F32)
    if final:
        out = _rms(out) * _prow(prm_ref, R_FINAL, D_MODEL)
    hout_ref[0] = out

    @pl.when(t == nt - 1)
    def _():
        sout_ref[0] = st_s[...]
        ctxout_ref[0, 0] = cbuf_g[valid:valid + 8, :]
        ctxout_ref[0, 1] = cbuf_s[valid:valid + 8, :]
        hgrn = jnp.transpose(st_s[0])
        gdn = st_s[1]
        ssd = st_s[2]
        ret = st_s[3]
        for hd in range(N_HEADS):
            lo, hi = hd * HEAD_DIM, (hd + 1) * HEAD_DIM
            hgo_ref[0, hd] = hgrn[lo:hi, lo:hi]
            gdo_ref[0, hd] = gdn[lo:hi, lo:hi]
            sdo_ref[0, hd] = ssd[(hd // 2) * 128:(hd // 2 + 1) * 128, lo:hi]
            rto_ref[0, hd] = ret[lo:hi, lo:hi]


def _prompt_layer(layer, h, cos, sin, win, wout, prm, expand, s0, ctx0, *, tile, valid, final):
    bsz, seq, _ = h.shape
    nt = seq // tile
    kern = functools.partial(_prompt_kernel, layer, tile, valid, final)
    const2 = lambda b, t: (0, 0)
    slab = lambda: pltpu.VMEM((tile, GROUP_W), F32)
    sq_state = (N_HEADS, HEAD_DIM, HEAD_DIM)
    head_states = [sq_state, sq_state, (N_HEADS, 128, HEAD_DIM), sq_state]
    return pl.pallas_call(
        kern,
        grid=(bsz, nt),
        in_specs=[
            pl.BlockSpec((1, tile, D_MODEL), lambda b, t: (b, t, 0)),
            pl.BlockSpec((tile, GROUP_W), lambda b, t: (t, 0)),
            pl.BlockSpec((tile, GROUP_W), lambda b, t: (t, 0)),
            pl.BlockSpec((1, D_MODEL, NP), lambda b, t: (layer, 0, 0)),
            pl.BlockSpec((1, D_MODEL, D_MODEL), lambda b, t: (layer, 0, 0)),
            pl.BlockSpec((1, N_PRM, D_MODEL), lambda b, t: (layer, 0, 0)),
            pl.BlockSpec(expand.shape, const2),
            pl.BlockSpec((4, GROUP_W, GROUP_W), lambda b, t: (0, 0, 0)),
            pl.BlockSpec((2, 8, CONV_DIM), lambda b, t: (0, 0, 0)),
        ],
        out_specs=[
            pl.BlockSpec((1, tile, D_MODEL), lambda b, t: (b, t, 0)),
            pl.BlockSpec((1, 4, GROUP_W, GROUP_W), lambda b, t: (b, 0, 0, 0)),
            pl.BlockSpec((1, 2, 8, CONV_DIM), lambda b, t: (b, 0, 0, 0)),
        ] + [pl.BlockSpec((1,) + s, lambda b, t: (b, 0, 0, 0)) for s in head_states],
        out_shape=[
            jax.ShapeDtypeStruct((bsz, seq, D_MODEL), F32),
            jax.ShapeDtypeStruct((bsz, 4, GROUP_W, GROUP_W), F32),
            jax.ShapeDtypeStruct((bsz, 2, 8, CONV_DIM), F32),
        ] + [jax.ShapeDtypeStruct((bsz,) + s, F32) for s in head_states],
        scratch_shapes=[
            pltpu.VMEM((tile, NP), F32),
            pltpu.VMEM((tile + 8, CONV_DIM), F32),
            pltpu.VMEM((tile + 8, CONV_DIM), F32),
            pltpu.VMEM((4, GROUP_W, GROUP_W), F32),
            pltpu.VMEM((6, tile, GROUP_W), F32),
        ] + [slab() for _ in range(22)] + [pltpu.VMEM((tile, D_MODEL), F32), pltpu.VMEM((tile, D_MODEL), BF16),
                                             pltpu.VMEM((tile, D_MODEL), BF16), pltpu.VMEM((8, GROUP_W), F32)],
        compiler_params=pltpu.CompilerParams(
            dimension_semantics=("arbitrary", "arbitrary"), vmem_limit_bytes=VMEM_LIMIT),
        name=f"prompt_layer_t{tile}",
    )(h, cos, sin, win, wout, prm, expand, s0, ctx0)


(VT_HQ, VT_HK, VT_HF, VT_HV, VT_GQ, VT_GK, VT_GV, VT_GDEC, VT_GBETA,
 VT_SV, VT_SB, VT_SC, VT_SDEC, VT_RQ, VT_RK, VT_RV, VT_RDEC) = [GROUP_W * i for i in range(17)]
N_VT = 17 * GROUP_W
DEC_VBLOCK = 32


def _dec_pre_kernel(layer, hs_ref, cos_ref, sin_ref, win_ref, prm_ref, exp_ref, gctx_ref, sctx_ref,
                    vt_ref, post_ref, ug_ref, us_ref):
    x = hs_ref[...]
    nb = x.shape[0]
    hn = _rms(x) * _prow(prm_ref, R_NORM, D_MODEL)
    proj = jnp.dot(hn.astype(BF16), win_ref[0], preferred_element_type=F32)

    ri = _iota((GROUP_W, GROUP_W), 0)
    ci = _iota((GROUP_W, GROUP_W), 1)
    ones64 = jnp.where((ri >> 6) == (ci >> 6), 1.0, 0.0).astype(BF16)

    def put(off, a):
        vt_ref[off:off + GROUP_W, :] = jnp.transpose(a)

    lb = _hgrn_lower_bound(_prow(prm_ref, R_LB, n=DEPTH), layer)
    q, k, f, v = _hgrn_inputs(proj, lb)
    put(VT_HQ, q)
    put(VT_HK, k)
    put(VT_HF, f)
    put(VT_HV, v)

    log_decay, vals = _compact_gates(proj[:, SMALL:SMALL + 128], prm_ref)
    decay = jnp.exp(_dot_exact_rhs(log_decay, exp_ref[:, 0:512]))
    vals = _dot_exact_rhs(vals, exp_ref[:, 512:1024])

    u = proj[:, B_QKV:B_QKV + CONV_DIM]
    ug_ref[...] = u
    cw = _prow(prm_ref, R_GCONV, CONV_DIM, CONV_W)
    conv = cw[3:4] * u + cw[2:3] * gctx_ref[0, 2] + cw[1:2] * gctx_ref[0, 1] + cw[0:1] * gctx_ref[0, 0]
    act = _silu(conv)
    qn, kn = _l2n_pair(act[:, 0:256], act[:, 256:512], ones64)
    put(VT_GQ, qn * QK_SCALE)
    put(VT_GK, kn)
    put(VT_GV, act[:, 512:768])
    put(VT_GDEC, decay[:, 0:256])
    put(VT_GBETA, vals[:, 0:256])

    u = proj[:, C_XBC:C_XBC + CONV_DIM]
    us_ref[...] = u
    cw = _prow(prm_ref, R_SCONV, CONV_DIM, CONV_W)
    conv = (cw[3:4] * u + cw[2:3] * sctx_ref[0, 2] + cw[1:2] * sctx_ref[0, 1] + cw[0:1] * sctx_ref[0, 0]
            + _prow(prm_ref, R_SCONVB, CONV_DIM))
    act = _silu(conv)
    post_ref[:, POST_SX:POST_SX + 256] = act[:, 0:256]
    put(VT_SV, act[:, 0:256] * vals[:, 256:512])
    put(VT_SB, act[:, 256:512])
    put(VT_SC, act[:, 512:768])
    put(VT_SDEC, decay[:, 256:512])

    cos = cos_ref[...]
    sin = sin_ref[...]
    put(VT_RQ, _rotary(proj[:, D_Q:D_Q + GROUP_W], cos, sin))
    put(VT_RK, _rotary(proj[:, D_K:D_K + GROUP_W], cos, sin) * QK_SCALE)
    put(VT_RV, proj[:, D_V:D_V + GROUP_W])
    put(VT_RDEC, jnp.broadcast_to(jnp.exp(_prow(prm_ref, R_RLOGG)), (nb, GROUP_W)))

    post_ref[:, POST_AZ:POST_AZ + 256] = proj[:, A_Z:A_Z + GROUP_W]
    post_ref[:, POST_BZ:POST_BZ + 256] = proj[:, B_Z:B_Z + GROUP_W]
    post_ref[:, POST_CZ:POST_CZ + 256] = proj[:, C_Z:C_Z + GROUP_W]
    post_ref[:, POST_DZ:POST_DZ + 256] = proj[:, D_Z:D_Z + GROUP_W]


def _dec_pre(layer, hs, cos, sin, win, prm, expand, gctx, sctx):
    nb = hs.shape[0]
    full = lambda a: pl.BlockSpec(a.shape, lambda i: (0,) * a.ndim)
    shapes = [(N_VT, nb), (nb, N_POST), (nb, CONV_DIM), (nb, CONV_DIM)]
    return pl.pallas_call(
        functools.partial(_dec_pre_kernel, layer),
        grid=(1,),
        in_specs=[
            full(hs), full(cos), full(sin),
            pl.BlockSpec((1, D_MODEL, NP), lambda i: (layer, 0, 0)),
            pl.BlockSpec((1, N_PRM, D_MODEL), lambda i: (layer, 0, 0)),
            full(expand),
            pl.BlockSpec((1, 3, nb, CONV_DIM), lambda i: (layer, 0, 0, 0)),
            pl.BlockSpec((1, 3, nb, CONV_DIM), lambda i: (layer, 0, 0, 0)),
        ],
        out_specs=[pl.BlockSpec(s, lambda i: (0, 0)) for s in shapes],
        out_shape=[jax.ShapeDtypeStruct(s, F32) for s in shapes],
        compiler_params=pltpu.CompilerParams(dimension_semantics=("arbitrary",), vmem_limit_bytes=VMEM_LIMIT),
        name="decode_pre",
    )(hs, cos, sin, win, prm, expand, gctx, sctx)


def _dec_state_kernel(vt_ref, hg_ref, gd_ref, sd_ref, rt_ref, hg_acc, gd_acc, rt_acc,
                      o_ref, hg_out, gd_out, sd_out, rt_out):
    del hg_acc, gd_acc, rt_acc
    h = pl.program_id(0)
    j = pl.program_id(1)
    vb = DEC_VBLOCK
    nb = vt_ref.shape[1]
    head = h * HEAD_DIM

    def row(off, i):
        return vt_ref[pl.ds(off + i, 1), :]

    def slab(off):
        return vt_ref[pl.ds(pl.multiple_of(off + head + j * vb, 8), vb), :]

    zero = jnp.zeros((vb, nb), F32)

    v = slab(VT_HV)

    def hgrn_step(k, o):
        s = row(VT_HF + head, k) * hg_ref[0, 0, k] + row(VT_HK + head, k) * v
        hg_out[0, 0, k] = s
        return o + row(VT_HQ + head, k) * s

    o_ref[0, 0] = lax.fori_loop(0, HEAD_DIM, hgrn_step, zero, unroll=4)

    v = slab(VT_GV)
    dec = row(VT_GDEC + head, 0)
    ks = lax.fori_loop(0, HEAD_DIM, lambda k, a: a + row(VT_GK + head, k) * gd_ref[0, 0, k], zero, unroll=4)
    u = row(VT_GBETA + head, 0) * (v - ks * dec)

    def gdn_step(k, o):
        s = gd_ref[0, 0, k] * dec + row(VT_GK + head, k) * u
        gd_out[0, 0, k] = s
        return o + row(VT_GQ + head, k) * s

    o_ref[1, 0] = lax.fori_loop(0, HEAD_DIM, gdn_step, zero, unroll=4)

    v = slab(VT_SV)
    dec = row(VT_SDEC + head, 0)
    grp = (h >> 1) * 128

    def ssd_step(n, o):
        s = sd_ref[0, 0, n] * dec + row(VT_SB + grp, n) * v
        sd_out[0, 0, n] = s
        return o + row(VT_SC + grp, n) * s

    o_ref[2, 0] = lax.fori_loop(0, 128, ssd_step, zero, unroll=4)

    v = slab(VT_RV)
    dec = row(VT_RDEC + head, 0)

    def ret_step(k, o):
        s = rt_ref[0, 0, k] * dec + row(VT_RK + head, k) * v
        rt_out[0, 0, k] = s
        return o + row(VT_RQ + head, k) * s

    o_ref[3, 0] = lax.fori_loop(0, HEAD_DIM, ret_step, zero, unroll=4)


def _dec_state(layer, vt, states, accs):
    nb = vt.shape[1]
    vb = DEC_VBLOCK
    blk = lambda a: pl.BlockSpec((1, 1, a.shape[2], vb, nb), lambda h, j: (layer, h, 0, j, 0))
    return pl.pallas_call(
        _dec_state_kernel,
        grid=(N_HEADS, HEAD_DIM // vb),
        in_specs=[pl.BlockSpec(vt.shape, lambda h, j: (0, 0))] + [blk(a) for a in states]
        + [pl.BlockSpec(memory_space=pl.ANY)] * 3,
        out_specs=[pl.BlockSpec((4, 1, vb, nb), lambda h, j: (0, h, j, 0))] + [blk(a) for a in states],
        out_shape=[jax.ShapeDtypeStruct((4, N_HEADS, HEAD_DIM, nb), F32)]
        + [jax.ShapeDtypeStruct(a.shape, F32) for a in states],
        input_output_aliases={3: 3, 5: 1, 6: 2, 7: 4},
        compiler_params=pltpu.CompilerParams(dimension_semantics=("arbitrary", "arbitrary"),
                                             vmem_limit_bytes=VMEM_LIMIT),
        name="decode_state",
    )(vt, *states, *accs)


def _dec_post_kernel(final, hs_ref, ot_ref, post_ref, wout_ref, prm_ref, out_ref):
    ri = _iota((GROUP_W, GROUP_W), 0)
    ci = _iota((GROUP_W, GROUP_W), 1)
    ones64 = jnp.where((ri >> 6) == (ci >> 6), 1.0, 0.0).astype(BF16)
    ones128 = jnp.where((ri >> 7) == (ci >> 7), 1.0, 0.0).astype(BF16)
    gate = lambda off: post_ref[:, off:off + GROUP_W]
    o = [jnp.transpose(ot_ref[m * GROUP_W:(m + 1) * GROUP_W, :]) for m in range(4)]
    y = _post_mix(o[0], o[1], o[2], o[3],
                  gate(POST_AZ), gate(POST_BZ), gate(POST_CZ), gate(POST_DZ), gate(POST_SX),
                  prm_ref, ones64, ones128)
    out = hs_ref[...] + jnp.dot(y.astype(BF16), wout_ref[0], preferred_element_type=F32)
    if final:
        out = _rms(out) * _prow(prm_ref, R_FINAL, D_MODEL)
    out_ref[...] = out


def _dec_post(layer, hs, ot, post, wout, prm, final):
    nb = hs.shape[0]
    full = lambda a: pl.BlockSpec(a.shape, lambda i: (0,) * a.ndim)
    return pl.pallas_call(
        functools.partial(_dec_post_kernel, final),
        grid=(1,),
        in_specs=[full(hs), full(ot), full(post),
                  pl.BlockSpec((1, D_MODEL, D_MODEL), lambda i: (layer, 0, 0)),
                  pl.BlockSpec((1, N_PRM, D_MODEL), lambda i: (layer, 0, 0))],
        out_specs=pl.BlockSpec((nb, D_MODEL), lambda i: (0, 0)),
        out_shape=jax.ShapeDtypeStruct((nb, D_MODEL), F32),
        compiler_params=pltpu.CompilerParams(dimension_semantics=("arbitrary",), vmem_limit_bytes=VMEM_LIMIT),
        name="decode_post",
    )(hs, ot, post, wout, prm)


POST_AZ, POST_BZ, POST_CZ, POST_DZ, POST_SX = [GROUP_W * i for i in range(5)]
N_POST = 5 * GROUP_W


def _expand_matrix():
    e = np.zeros((128, 4 * GROUP_W), np.float32)
    for block, first_lane in enumerate((0, 2 * N_HEADS, N_HEADS, 2 * N_HEADS)):
        for h in range(N_HEADS):
            lo = block * GROUP_W + h * HEAD_DIM
            e[first_lane + h, lo:lo + HEAD_DIM] = 1.0
    return jnp.asarray(e, BF16)


def _relayout_w_in(w_in):
    sizes = (256, 256, 256, 256, 768, 256, 4, 4, 768, 256, 4, 256, 256, 256, 256)
    offs = np.concatenate([[0], np.cumsum(sizes)])
    w_t = jnp.transpose(w_in, (2, 0, 1))
    seg = lambda i: w_t[offs[i]:offs[i + 1]]
    pad = jnp.zeros((128 - 12,) + w_t.shape[1:], w_in.dtype)
    rows = [seg(0), seg(1), seg(2), seg(3), seg(4), seg(5), seg(8), seg(9), seg(11), seg(12), seg(13), seg(14),
            seg(6), seg(7), seg(10), pad]
    return jnp.transpose(jnp.concatenate(rows, axis=0), (1, 2, 0)).astype(BF16)


def _pack_params(norm_w, hgrn_lb_logits, hgrn_norm_w, gdn_conv_w, gdn_a_log, gdn_dt_bias, gdn_norm_w,
                 ssd_conv_w, ssd_conv_b, ssd_a_log, ssd_dt_bias, ssd_d, ssd_norm_w, ret_norm_w, ret_norm_b,
                 final_norm_w):
    depth = norm_w.shape[0]
    prm = jnp.zeros((depth, N_PRM, D_MODEL), F32)

    def put(p, r, a):
        a = a.astype(F32)
        if a.ndim == 2:
            a = a[:, None, :]
        return p.at[:, r:r + a.shape[1], 0:a.shape[2]].set(a)

    rep = lambda a: jnp.repeat(a, HEAD_DIM, axis=-1)
    prm = put(prm, R_NORM, norm_w)
    prm = put(prm, R_LB, jnp.broadcast_to(hgrn_lb_logits[None], (depth,) + hgrn_lb_logits.shape))
    prm = put(prm, R_HNORM, hgrn_norm_w)
    prm = put(prm, R_GCONV, gdn_conv_w)
    prm = put(prm, R_GNORM, gdn_norm_w)
    prm = put(prm, R_SCONV, ssd_conv_w)
    prm = put(prm, R_SCONVB, ssd_conv_b)
    prm = put(prm, R_SD, rep(ssd_d))
    gap = jnp.zeros((depth, N_HEADS), F32)
    prm = put(prm, R_CBIAS, jnp.concatenate([gdn_dt_bias.astype(F32), gap, ssd_dt_bias.astype(F32)], axis=-1))
    prm = put(prm, R_CALOG, jnp.concatenate([gdn_a_log.astype(F32), gap, ssd_a_log.astype(F32)], axis=-1))
    prm = put(prm, R_SNORM, ssd_norm_w)
    prm = put(prm, R_RNORM, ret_norm_w)
    prm = put(prm, R_RNORMB, ret_norm_b)
    ret_logg = jnp.log1p(-jnp.exp2(-5.0 - jnp.arange(N_HEADS, dtype=F32)))
    prm = put(prm, R_RLOGG, jnp.broadcast_to(rep(ret_logg)[None], (depth, GROUP_W)))
    prm = put(prm, R_FINAL, jnp.broadcast_to(final_norm_w[None], (depth, D_MODEL)))
    return prm


def _rope_tables(pos):
    half = HEAD_DIM // 2
    inv_freq = 1.0 / (ROPE_BASE ** jnp.linspace(0.0, 1.0, half, dtype=F32))
    ang = pos[:, None] * inv_freq[None, :]
    cos, sin = jnp.cos(ang), jnp.sin(ang)
    cos_h = jnp.concatenate([cos, cos], axis=-1)
    sin_h = jnp.concatenate([-sin, sin], axis=-1)
    return jnp.tile(cos_h, (1, N_HEADS)), jnp.tile(sin_h, (1, N_HEADS))


def kernel(x_prompt, x_sample, state_hgrn, state_gdn, state_gdn_conv, state_ssd, state_ssd_conv, state_ret,
           meta_tokens, norm_w, w_in, hgrn_lb_logits, hgrn_norm_w, gdn_conv_w, gdn_a_log, gdn_dt_bias, gdn_norm_w,
           ssd_conv_w, ssd_conv_b, ssd_a_log, ssd_dt_bias, ssd_d, ssd_norm_w, ret_norm_w, ret_norm_b,
           w_out, final_norm_w):
    depth = w_in.shape[0]
    bsz, seq, _ = x_prompt.shape
    nb = x_sample.shape[0]
    tile = PROMPT_TILE if seq % PROMPT_TILE == 0 else CHUNK

    win = _relayout_w_in(w_in)
    wout = w_out.astype(BF16)
    prm = _pack_params(norm_w, hgrn_lb_logits, hgrn_norm_w, gdn_conv_w, gdn_a_log, gdn_dt_bias, gdn_norm_w,
                       ssd_conv_w, ssd_conv_b, ssd_a_log, ssd_dt_bias, ssd_d, ssd_norm_w, ret_norm_w, ret_norm_b,
                       final_norm_w)
    expand = _expand_matrix()

    cos_m, sin_m = _rope_tables(jnp.arange(CHUNK, dtype=F32))
    cos_p, sin_p = _rope_tables(N_META + jnp.arange(seq, dtype=F32))
    cos_s, sin_s = _rope_tables(PAST_LEN + jnp.arange(1, dtype=F32))

    hm = jnp.zeros((1, CHUNK, D_MODEL), F32).at[0, :N_META].set(meta_tokens.astype(F32))
    hp = x_prompt.astype(F32)
    hs = x_sample.astype(F32)[:, 0, :]
    zero_s = jnp.zeros((4, GROUP_W, GROUP_W), F32)
    zero_ctx = jnp.zeros((2, 8, CONV_DIM), F32)

    gctx = jnp.swapaxes(state_gdn_conv.astype(F32), 1, 2)
    sctx = jnp.swapaxes(state_ssd_conv.astype(F32), 1, 2)

    seq_minor = lambda a: jnp.transpose(a.astype(F32), (0, 2, 3, 4, 1))
    dec_states = [seq_minor(a) for a in (state_hgrn, state_gdn, state_ssd, state_ret)]
    dec_new = [jnp.zeros_like(dec_states[0]), jnp.zeros_like(dec_states[1]), dec_states[2],
               jnp.zeros_like(dec_states[3])]

    p_states, p_ctx = [], []
    s_gctx, s_sctx = [], []
    for l in range(depth):
        final = l == depth - 1
        hm, sm, cm, *_ = _prompt_layer(l, hm, cos_m, sin_m, win, wout, prm, expand, zero_s, zero_ctx,
                                       tile=CHUNK, valid=N_META, final=False)
        hp, _, cp, *head_states = _prompt_layer(l, hp, cos_p, sin_p, win, wout, prm, expand, sm[0], cm[0],
                                                tile=tile, valid=tile, final=final)
        p_states.append(head_states)
        p_ctx.append(cp)

        vt, post, ug, us = _dec_pre(l, hs, cos_s, sin_s, win, prm, expand, gctx, sctx)
        ot, *dec_new = _dec_state(l, vt, [dec_states[0], dec_states[1], dec_new[2], dec_states[3]],
                                  [dec_new[0], dec_new[1], dec_new[3]])
        hs = _dec_post(l, hs, ot.reshape(4 * GROUP_W, nb), post, wout, prm, final)
        s_gctx.append(jnp.concatenate([state_gdn_conv[l, :, 1:].astype(F32), ug[:, None, :]], axis=1))
        s_sctx.append(jnp.concatenate([state_ssd_conv[l, :, 1:].astype(F32), us[:, None, :]], axis=1))

    ph, pg, ps, pr = [jnp.stack([layer_states[i] for layer_states in p_states]) for i in range(4)]
    cp = jnp.stack(p_ctx)
    seq_major = lambda a: jnp.transpose(a, (0, 4, 1, 2, 3))
    return (hp.astype(x_prompt.dtype), hs[:, None, :].astype(x_sample.dtype),
            ph.astype(state_hgrn.dtype), pg.astype(state_gdn.dtype),
            cp[:, :, 0, 8 - (CONV_W - 1):].astype(state_gdn_conv.dtype),
            ps.astype(state_ssd.dtype),
            cp[:, :, 1, 8 - (CONV_W - 1):].astype(state_ssd_conv.dtype),
            pr.astype(state_ret.dtype),
            seq_major(dec_new[0]).astype(state_hgrn.dtype), seq_major(dec_new[1]).astype(state_gdn.dtype),
            jnp.stack(s_gctx).astype(state_gdn_conv.dtype), seq_major(dec_new[2]).astype(state_ssd.dtype),
            jnp.stack(s_sctx).astype(state_ssd_conv.dtype), seq_major(dec_new[3]).astype(state_ret.dtype))
```

```python
import functools
import itertools

import numpy as np
import jax
import jax.numpy as jnp
from jax import lax
from jax.experimental import pallas as pl
from jax.experimental.pallas import tpu as pltpu

F32 = jnp.float32
BF16 = jnp.bfloat16

D_MODEL = 1024
GROUP_W = 256
HEAD_DIM = 64
N_HEADS = 4
CHUNK = 64
CONV_W = 4
CONV_DIM = 768
N_META = 16
PAST_LEN = 16384
ROPE_BASE = 10000.0
EPS = 1e-6
TINY = 1e-30
QK_SCALE = HEAD_DIM ** -0.5
DEPTH = 4

A_Q, A_F, A_I, A_Z = 0, 256, 512, 768
B_QKV, B_Z = 1024, 1792
C_XBC, C_Z = 2048, 2816
D_Q, D_K, D_V, D_Z = 3072, 3328, 3584, 3840
SMALL = 4096
NP = 4224

R_NORM, R_LB, R_HNORM = 0, 1, 5
R_GCONV, R_GNORM = 6, 12
R_SCONV, R_SCONVB, R_SD, R_SNORM = 13, 17, 20, 21
R_RNORM, R_RNORMB, R_RLOGG, R_FINAL = 22, 23, 24, 25
R_CBIAS, R_CALOG = 26, 27
N_PRM = 32

VMEM_LIMIT = 56 * 1024 * 1024
PROMPT_TILE = 256
HGRN_SAFE_SPAN = 80.0
FILL_PER_STATE_STEP = 3
FILL_PER_ROUND = 4
_DONE = object()


def _interleave(leads, fillers, fill_per_round):
    leads = list(leads)
    fillers = iter(fillers)
    fill_live = fill_per_round > 0
    while leads or fill_live:
        leads = [g for g in leads if next(g, _DONE) is not _DONE]
        for _ in range(fill_per_round):
            if fill_live and next(fillers, _DONE) is _DONE:
                fill_live = False


def _dot(a, b):
    return jnp.dot(a.astype(BF16), b.astype(BF16), preferred_element_type=F32)


def _dot_nt(a, b):
    return lax.dot_general(a.astype(BF16), b.astype(BF16), (((1,), (1,)), ((), ())),
                           preferred_element_type=F32)


def _dot_tn(a, b):
    return lax.dot_general(a.astype(BF16), b.astype(BF16), (((0,), (0,)), ((), ())),
                           preferred_element_type=F32)


def _split3(x):
    hi = x.astype(BF16)
    r1 = x - hi.astype(F32)
    mid = r1.astype(BF16)
    lo = (r1 - mid.astype(F32)).astype(BF16)
    return hi, mid, lo


def _dot_exact_rhs(x, m):
    hi, mid, lo = _split3(x)
    return (jnp.dot(hi, m, preferred_element_type=F32) + jnp.dot(mid, m, preferred_element_type=F32)
            + jnp.dot(lo, m, preferred_element_type=F32))


def _dot_exact_lhs(m, x):
    hi, mid, lo = _split3(x)
    return (jnp.dot(m, hi, preferred_element_type=F32) + jnp.dot(m, mid, preferred_element_type=F32)
            + jnp.dot(m, lo, preferred_element_type=F32))


def _sigmoid(x):
    return jax.nn.sigmoid(x)


def _silu(x):
    return x * _sigmoid(x)


def _softplus(x):
    return jnp.maximum(x, 0.0) + jnp.log(1.0 + jnp.exp(-jnp.abs(x)))


def _iota(shape, dim):
    return lax.broadcasted_iota(jnp.int32, shape, dim)


def _block_diag(x, mask01):
    xb = x.astype(BF16)
    return jnp.concatenate([xb] * N_HEADS, axis=0) * mask01


def _chunk_cumsum(x, row_in_chunk):
    sh = 1
    while sh < CHUNK:
        r = pltpu.roll(x, sh, axis=0)
        x = x + jnp.where(row_in_chunk >= sh, r, 0.0)
        sh *= 2
    return x


def _swap_halves(x):
    lane = _iota(x.shape, 1)
    fwd = pltpu.roll(x, GROUP_W - HEAD_DIM // 2, axis=1)
    bwd = pltpu.roll(x, HEAD_DIM // 2, axis=1)
    return jnp.where((lane & (HEAD_DIM - 1)) < HEAD_DIM // 2, fwd, bwd)


def _group_sums(xs, ones_bd, squares=()):
    parts, spans = [], []
    for i, x in enumerate(xs):
        hi = x.astype(BF16)
        terms = [hi] if i in squares else [hi, (x - hi.astype(F32)).astype(BF16)]
        spans.append((len(parts), len(terms)))
        parts += terms
    sums = jnp.dot(jnp.concatenate(parts, axis=0) if len(parts) > 1 else parts[0], ones_bd,
                   preferred_element_type=F32)
    n = xs[0].shape[0]
    out = []
    for first, count in spans:
        total = sums[first * n:(first + 1) * n]
        for j in range(1, count):
            total = total + sums[(first + j) * n:(first + j + 1) * n]
        out.append(total)
    return out


def _group_sum(x, ones_bd, square=False):
    return _group_sums([x], ones_bd, squares=(0,) if square else ())[0]


def _rms(x):
    return x * lax.rsqrt(jnp.mean(x * x, axis=-1, keepdims=True) + EPS)


def _hgrn_lower_bound(logits, layer):
    m = jnp.max(logits, axis=0, keepdims=True)
    e = jnp.exp(logits - m)
    w = e / jnp.sum(e, axis=0, keepdims=True)
    cum = jnp.sum(w[0:layer + 1], axis=0, keepdims=True) - w[0:1]
    return jnp.maximum(cum, 0.0)


def _prow(prm_ref, r, width=GROUP_W, n=1):
    return prm_ref[0, r:r + n, 0:width]


def _hgrn_inputs(proj, lb):
    aq = proj[:, A_Q:A_Q + GROUP_W]
    af = proj[:, A_F:A_F + GROUP_W]
    q = _silu(aq) * QK_SCALE
    k = (1.0 - lb) * _sigmoid(-af)
    f = jnp.maximum(lb + (1.0 - lb) * _sigmoid(af), TINY)
    return q, k, f, proj[:, A_I:A_I + GROUP_W]


def _compact_gates(c, prm_ref):
    sp = _softplus(c + _prow(prm_ref, R_CBIAS, 128))
    log_decay = -jnp.exp(_prow(prm_ref, R_CALOG, 128)) * sp
    lane = _iota(c.shape, 1)
    vals = jnp.where((lane >= N_HEADS) & (lane < 2 * N_HEADS), _sigmoid(c), sp)
    return log_decay, vals


def _l2n_pair(q, k, ones64):
    sq_q, sq_k = _group_sums([q * q, k * k], ones64, squares=(0, 1))
    return q * lax.rsqrt(sq_q + EPS), k * lax.rsqrt(sq_k + EPS)


def _rotary(x, cos, sin):
    return x * cos + _swap_halves(x) * sin


def _post_mix(o_a, o_b, o_c, o_d, z_a, z_b, z_c, z_d, x_ssd, prm_ref, ones64, ones128):
    inv64 = 1.0 / HEAD_DIM
    sq_a, sq_b, sum_d = _group_sums([o_a * o_a, o_b * o_b, o_d], ones64, squares=(0, 1))
    y_a = o_a * lax.rsqrt(sq_a * inv64 + EPS) * _prow(prm_ref, R_HNORM) * _silu(z_a)
    y_b = o_b * lax.rsqrt(sq_b * inv64 + EPS) * _prow(prm_ref, R_GNORM) * _silu(z_b)
    t_c = (o_c + _prow(prm_ref, R_SD) * x_ssd) * _silu(z_c)
    y_c = (t_c * lax.rsqrt(_group_sum(t_c * t_c, ones128, square=True) * (1.0 / 128.0) + EPS)
           * _prow(prm_ref, R_SNORM))
    xc = o_d - sum_d * inv64
    y_d = ((xc * lax.rsqrt(_group_sum(xc * xc, ones64, square=True) * inv64 + EPS) * _prow(prm_ref, R_RNORM)
            + _prow(prm_ref, R_RNORMB)) * _silu(z_d))
    return jnp.concatenate([y_a, y_b, y_c, y_d], axis=-1)


def _prompt_kernel(layer, tile, valid, final,
                   h_ref, cos_ref, sin_ref, win_ref, wout_ref, prm_ref, exp_ref, s0_ref, ctx0_ref,
                   hout_ref, sout_ref, ctxout_ref, hgo_ref, gdo_ref, sdo_ref, rto_ref,
                   proj_s, cbuf_g, cbuf_s, st_s, he_s,
                   hq_s, hk_s, hv_s, hg_s,
                   gq_s, gk_s, gv_s, gg_s, gb_s, gqk_s, gsv_s, gsk_s,
                   sx_s, sv_s, sb_s, sc_s, sg_s, sdt_s,
                   rq_s, rk_s, rv_s, rg_s, o_s, y_s, hn_s, hspan_s):
    t = pl.program_id(1)
    nt = pl.num_programs(1)
    n_chunks = tile // CHUNK
    C = CHUNK

    @pl.when(t == 0)
    def _():
        st_s[...] = s0_ref[...]
        cbuf_g[0:8, :] = ctx0_ref[0]
        cbuf_s[0:8, :] = ctx0_ref[1]

    hn_s[...] = (_rms(h_ref[0]) * _prow(prm_ref, R_NORM, D_MODEL)).astype(BF16)

    def project(*col_ranges):
        for lo, hi in col_ranges:
            proj_s[:, lo:hi] = jnp.dot(hn_s[...], win_ref[0, :, lo:hi], preferred_element_type=F32)
        yield

    row = _iota((tile, GROUP_W), 0)
    row_in_chunk = row & (C - 1)
    if valid < tile:
        live = row < valid
        keep = lambda a: jnp.where(live[:, 0:a.shape[1]], a, 0.0)
    else:
        keep = lambda a: a

    ri = _iota((GROUP_W, GROUP_W), 0)
    ci = _iota((GROUP_W, GROUP_W), 1)
    ones64 = jnp.where((ri >> 6) == (ci >> 6), 1.0, 0.0).astype(BF16)
    ones128 = jnp.where((ri >> 7) == (ci >> 7), 1.0, 0.0).astype(BF16)

    def conv_taps(cbuf, u, r_w):
        cbuf[8:8 + tile, :] = u
        cw = _prow(prm_ref, r_w, CONV_DIM, CONV_W)
        ext = cbuf[...]
        acc = cw[0:1] * ext
        for tap in range(1, CONV_W):
            acc = pltpu.roll(acc, 1, axis=0) + cw[tap:tap + 1] * ext
        cbuf[0:8, :] = cbuf[tile:tile + 8, :]
        return acc[8:8 + tile, :]

    def gdn_prep():
        log_decay, vals = _compact_gates(proj_s[:, SMALL:SMALL + 128], prm_ref)
        cum = _chunk_cumsum(keep(log_decay), row_in_chunk[:, 0:128])
        cum = _dot_exact_rhs(cum, exp_ref[:, 0:512])
        gg_s[...] = cum[:, 0:256]
        sg_s[...] = cum[:, 256:512]
        vals = _dot_exact_rhs(keep(vals), exp_ref[:, 512:1024])
        gb_s[...] = vals[:, 0:256]
        sdt_s[...] = vals[:, 256:512]
        yield
        act = _silu(conv_taps(cbuf_g, proj_s[:, B_QKV:B_QKV + CONV_DIM], R_GCONV))
        qn, kn = _l2n_pair(act[:, 0:256], act[:, 256:512], ones64)
        gq_s[...] = qn * QK_SCALE
        gk_s[...] = kn
        gv_s[...] = act[:, 512:768]
        yield

    def hgrn_prep():
        lb = _hgrn_lower_bound(_prow(prm_ref, R_LB, n=DEPTH), layer)
        q, k, f, v = _hgrn_inputs(proj_s[...], lb)
        hq_s[...] = q
        hk_s[...] = k
        hv_s[...] = keep(v)
        yield
        G = _chunk_cumsum(keep(jnp.log(f)), row_in_chunk)
        hg_s[...] = G
        span = jnp.zeros((1, GROUP_W), F32)
        for c in range(n_chunks):
            mid = G[c * C + C // 2 - 1:c * C + C // 2, :]
            span = jnp.maximum(span, jnp.maximum(G[c * C:c * C + 1, :] - mid, mid - G[c * C + C - 1:c * C + C, :]))
        hspan_s[...] = jnp.broadcast_to(span, hspan_s.shape)
        yield

    def hgrn_level_exps():
        G = hg_s[...]
        last = G
        s = 1
        lvl = 5
        while s < C:
            upper = (row & s) != 0
            ref_row = jnp.where(upper, pltpu.roll(last, s, axis=0), last)
            he_s[lvl] = jnp.exp(jnp.minimum(jnp.where(upper, G - ref_row, ref_row - G), 0.0))
            last = jnp.where(upper, last, pltpu.roll(last, tile - s, axis=0))
            s *= 2
            lvl -= 1

    def ssd_prep():
        act = _silu(conv_taps(cbuf_s, proj_s[:, C_XBC:C_XBC + CONV_DIM], R_SCONV)
                    + _prow(prm_ref, R_SCONVB, CONV_DIM))
        dt = sdt_s[...]
        sx_s[...] = act[:, 0:256]
        sv_s[...] = act[:, 0:256] * dt
        sb_s[...] = act[:, 256:512]
        sc_s[...] = act[:, 512:768]
        yield

    def ret_prep():
        cos = cos_ref[...]
        sin = sin_ref[...]
        rq_s[...] = _rotary(proj_s[:, D_Q:D_Q + GROUP_W], cos, sin)
        rk_s[...] = _rotary(proj_s[:, D_K:D_K + GROUP_W], cos, sin) * QK_SCALE
        rv_s[...] = keep(proj_s[:, D_V:D_V + GROUP_W])
        live_rows = jnp.clip(jnp.minimum(row_in_chunk + 1, valid - (row - row_in_chunk)), 0, C)
        rg_s[...] = live_rows.astype(F32) * _prow(prm_ref, R_RLOGG)
        yield

    ii = _iota((C, GROUP_W), 0)
    jj = _iota((C, GROUP_W), 1) & (C - 1)
    tri = jj <= ii
    strict = jj < ii
    eye = jj == ii
    eye_f = jnp.where(eye, 1.0, 0.0)
    sq_mask = (ri >> 6) == (ci >> 6)
    grp_mask = (ri >> 7) == (ci >> 7)
    bd_mask = ones64
    grp_row_mask = ones128

    def chunk_rows(c):
        return pl.ds(c * C, C) if isinstance(c, int) else pl.ds(pl.multiple_of(c * C, C), C)

    def decay_parts(G):
        g_row = jnp.sum(jnp.where(eye, G, 0.0), axis=0, keepdims=True)
        dm = jnp.where(tri, jnp.exp(jnp.minimum(G - g_row, 0.0)), 0.0)
        g_last = G[C - 1:C, :]
        return dm, g_last

    def hgrn_a(c):
        sl = chunk_rows(c)
        G = hg_s[sl, :]
        mid = hg_s[pl.ds(c * C + C // 2 - 1, 1), :]
        qf = hq_s[sl, :] * jnp.exp(G - mid)
        kf = hk_s[sl, :] * jnp.exp(mid - G)
        scores = jnp.where(tri, _dot_nt(qf, _block_diag(kf, bd_mask)), 0.0)
        yield
        o_s[sl, 0:256] = _dot(scores, _block_diag(hv_s[sl, :], bd_mask))
        yield

    def hgrn_a_any_decay(c, carry):
        sl = pl.ds(pl.multiple_of(c * C, C), C)
        q = hq_s[sl, :]
        k = hk_s[sl, :]
        scores = jnp.where(eye, _dot_nt(q, _block_diag(k, bd_mask)), 0.0)
        s = C // 2
        lvl = 0
        while s >= 1:
            e = he_s[lvl, sl, :]
            upper = (ii & s) != 0
            qs = jnp.where(upper, q * e, 0.0)
            ks = jnp.where(upper, 0.0, k * e)
            sc = _dot_nt(qs, _block_diag(ks, bd_mask))
            same_block = (ii & -(2 * s)) == (jj & -(2 * s))
            scores = scores + jnp.where(same_block, sc, 0.0)
            s //= 2
            lvl += 1
        o_s[sl, 0:256] = _dot(scores, _block_diag(hv_s[sl, :], bd_mask))
        return carry

    def gdn_a(c):
        sl = chunk_rows(c)
        q = gq_s[sl, :]
        k = gk_s[sl, :]
        G = gg_s[sl, :]
        beta = gb_s[sl, :]
        dm, _ = decay_parts(G)
        kq = _dot_nt(jnp.concatenate([k, q], axis=0), _block_diag(k, bd_mask))
        yield
        m = jnp.where(strict, kq[0:C] * dm * beta, 0.0)
        gqk_s[sl, :] = kq[C:2 * C] * dm
        p = _dot(m, _block_diag(m, bd_mask))
        yield
        tinv = eye_f - m
        n = 2
        while 2 * n < C:
            both = _dot(jnp.concatenate([p, tinv], axis=0), _block_diag(p, bd_mask))
            yield
            p = both[0:C]
            tinv = tinv + both[C:2 * C]
            n *= 2
        tinv = tinv + _dot(tinv, _block_diag(p, bd_mask))
        yield
        kb = k * beta * jnp.exp(G)
        rhs = jnp.concatenate([_block_diag(gv_s[sl, :] * beta, bd_mask), _block_diag(kb, bd_mask)], axis=1)
        sol = _dot(tinv, rhs)
        gsv_s[sl, :] = sol[:, 0:256]
        gsk_s[sl, :] = sol[:, 256:512]
        yield

    def ssd_a(c):
        sl = chunk_rows(c)
        bm = sb_s[sl, :]
        dm, _ = decay_parts(sg_s[sl, :])
        cb = _dot_nt(sc_s[sl, :], _block_diag(bm, grp_row_mask))
        yield
        o_s[sl, 512:768] = _dot(cb * dm, _block_diag(sv_s[sl, :], bd_mask))
        yield

    def ret_a(c):
        sl = chunk_rows(c)
        dm, _ = decay_parts(rg_s[sl, :])
        sc = _dot_nt(rq_s[sl, :], _block_diag(rk_s[sl, :], bd_mask)) * dm
        yield
        o_s[sl, 768:1024] = _dot(sc, _block_diag(rv_s[sl, :], bd_mask))
        yield

    def hgrn_b(c):
        sl = chunk_rows(c)
        G = hg_s[sl, :]
        g_last = G[C - 1:C, :]
        st = st_s[0]
        o_s[sl, 0:256] += _dot_nt(hq_s[sl, :] * jnp.exp(G), st)
        yield
        kt = hk_s[sl, :] * jnp.exp(g_last - G)
        st_s[0] = st * jnp.exp(g_last) + jnp.where(sq_mask, _dot_tn(hv_s[sl, :], kt), 0.0)
        yield

    def gdn_b(c):
        sl = chunk_rows(c)
        G = gg_s[sl, :]
        g_last = G[C - 1:C, :]
        st = st_s[1]
        tmp = _dot(jnp.concatenate([gsk_s[sl, :], gq_s[sl, :] * jnp.exp(G)], axis=0), st)
        yield
        u = gsv_s[sl, :] - tmp[0:C]
        kt = gk_s[sl, :] * jnp.exp(g_last - G)
        st_s[1] = st * jnp.exp(g_last) + jnp.where(sq_mask, _dot_tn(kt, u), 0.0)
        yield
        o_s[sl, 256:512] = tmp[C:2 * C] + _dot(gqk_s[sl, :], _block_diag(u, bd_mask))
        yield

    def ssd_b(c):
        sl = chunk_rows(c)
        G = sg_s[sl, :]
        g_last = G[C - 1:C, :]
        st = st_s[2]
        o_s[sl, 512:768] += jnp.exp(G) * _dot(sc_s[sl, :], st)
        yield
        vt = sv_s[sl, :] * jnp.exp(g_last - G)
        st_s[2] = st * jnp.exp(g_last) + jnp.where(grp_mask, _dot_tn(sb_s[sl, :], vt), 0.0)
        yield

    def ret_b(c):
        sl = chunk_rows(c)
        G = rg_s[sl, :]
        g_last = G[C - 1:C, :]
        st = st_s[3]
        o_s[sl, 768:1024] += jnp.exp(G) * _dot(rq_s[sl, :], st)
        yield
        vt = rv_s[sl, :] * jnp.exp(g_last - G)
        st_s[3] = st * jnp.exp(g_last) + jnp.where(sq_mask, _dot_tn(rk_s[sl, :], vt), 0.0)
        yield

    def post(c):
        sl = chunk_rows(c)
        y_s[sl, :] = _post_mix(o_s[sl, 0:256], o_s[sl, 256:512], o_s[sl, 512:768], o_s[sl, 768:1024],
                               proj_s[sl, A_Z:A_Z + GROUP_W], proj_s[sl, B_Z:B_Z + GROUP_W],
                               proj_s[sl, C_Z:C_Z + GROUP_W], proj_s[sl, D_Z:D_Z + GROUP_W],
                               sx_s[sl, :], prm_ref, ones64, ones128).astype(BF16)
        yield

    chunks = range(n_chunks)
    every = lambda *gens: itertools.chain.from_iterable(gens)
    z_cols = [(z, z + GROUP_W) for z in (A_Z, B_Z, C_Z, D_Z)]
    _interleave([every(project((B_QKV, B_Z), (SMALL, NP)), gdn_prep())], project((A_Q, A_Z)), 1)
    _interleave([gdn_a(c) for c in chunks],
                every(hgrn_prep(), project((D_Q, D_Z)), *[hgrn_a(c) for c in chunks],
                      ret_prep(), project((C_XBC, C_Z)), *[ret_a(c) for c in chunks],
                      ssd_prep(), project(*z_cols), *[ssd_a(c) for c in chunks]),
                FILL_PER_ROUND)

    @pl.when(jnp.max(hspan_s[0:1, :]) > HGRN_SAFE_SPAN)
    def _():
        hgrn_level_exps()
        lax.fori_loop(0, n_chunks, hgrn_a_any_decay, 0)

    def state_step_region(c, carry):
        _interleave([gdn_b(c)], every(hgrn_b(c), ssd_b(c), ret_b(c)), FILL_PER_STATE_STEP)
        _interleave([post(c)], (), 0)
        return carry

    lax.fori_loop(0, n_chunks, state_step_region, 0, unroll=2 if n_chunks % 2 == 0 else 1)

    out = h_ref[0] + jnp.dot(y_s[...], wout_ref[0], preferred_element_type=F32)
    if final:
        out = _rms(out) * _prow(prm_ref, R_FINAL, D_MODEL)
    hout_ref[0] = out

    @pl.when(t == nt - 1)
    def _():
        sout_ref[0] = st_s[...]
        ctxout_ref[0, 0] = cbuf_g[valid:valid + 8, :]
        ctxout_ref[0, 1] = cbuf_s[valid:valid + 8, :]
        hgrn = jnp.transpose(st_s[0])
        gdn = st_s[1]
        ssd = st_s[2]
        ret = st_s[3]
        for hd in range(N_HEADS):
            lo, hi = hd * HEAD_DIM, (hd + 1) * HEAD_DIM
            hgo_ref[0, hd] = hgrn[lo:hi, lo:hi]
            gdo_ref[0, hd] = gdn[lo:hi, lo:hi]
            sdo_ref[0, hd] = ssd[(hd // 2) * 128:(hd // 2 + 1) * 128, lo:hi]
            rto_ref[0, hd] = ret[lo:hi, lo:hi]


def _prompt_layer(layer, h, cos, sin, win, wout, prm, expand, s0, ctx0, *, tile, valid, final):
    bsz, seq, _ = h.shape
    nt = seq // tile
    kern = functools.partial(_prompt_kernel, layer, tile, valid, final)
    const2 = lambda b, t: (0, 0)
    slab = lambda: pltpu.VMEM((tile, GROUP_W), F32)
    sq_state = (N_HEADS, HEAD_DIM, HEAD_DIM)
    head_states = [sq_state, sq_state, (N_HEADS, 128, HEAD_DIM), sq_state]
    return pl.pallas_call(
        kern,
        grid=(bsz, nt),
        in_specs=[
            pl.BlockSpec((1, tile, D_MODEL), lambda b, t: (b, t, 0)),
            pl.BlockSpec((tile, GROUP_W), lambda b, t: (t, 0)),
            pl.BlockSpec((tile, GROUP_W), lambda b, t: (t, 0)),
            pl.BlockSpec((1, D_MODEL, NP), lambda b, t: (layer, 0, 0)),
            pl.BlockSpec((1, D_MODEL, D_MODEL), lambda b, t: (layer, 0, 0)),
            pl.BlockSpec((1, N_PRM, D_MODEL), lambda b, t: (layer, 0, 0)),
            pl.BlockSpec(expand.shape, const2),
            pl.BlockSpec((4, GROUP_W, GROUP_W), lambda b, t: (0, 0, 0)),
            pl.BlockSpec((2, 8, CONV_DIM), lambda b, t: (0, 0, 0)),
        ],
        out_specs=[
            pl.BlockSpec((1, tile, D_MODEL), lambda b, t: (b, t, 0)),
            pl.BlockSpec((1, 4, GROUP_W, GROUP_W), lambda b, t: (b, 0, 0, 0)),
            pl.BlockSpec((1, 2, 8, CONV_DIM), lambda b, t: (b, 0, 0, 0)),
        ] + [pl.BlockSpec((1,) + s, lambda b, t: (b, 0, 0, 0)) for s in head_states],
        out_shape=[
            jax.ShapeDtypeStruct((bsz, seq, D_MODEL), F32),
            jax.ShapeDtypeStruct((bsz, 4, GROUP_W, GROUP_W), F32),
            jax.ShapeDtypeStruct((bsz, 2, 8, CONV_DIM), F32),
        ] + [jax.ShapeDtypeStruct((bsz,) + s, F32) for s in head_states],
        scratch_shapes=[
            pltpu.VMEM((tile, NP), F32),
            pltpu.VMEM((tile + 8, CONV_DIM), F32),
            pltpu.VMEM((tile + 8, CONV_DIM), F32),
            pltpu.VMEM((4, GROUP_W, GROUP_W), F32),
            pltpu.VMEM((6, tile, GROUP_W), F32),
        ] + [slab() for _ in range(22)] + [pltpu.VMEM((tile, D_MODEL), F32), pltpu.VMEM((tile, D_MODEL), BF16),
                                             pltpu.VMEM((tile, D_MODEL), BF16), pltpu.VMEM((8, GROUP_W), F32)],
        compiler_params=pltpu.CompilerParams(
            dimension_semantics=("arbitrary", "arbitrary"), vmem_limit_bytes=VMEM_LIMIT),
        name=f"prompt_layer_t{tile}",
    )(h, cos, sin, win, wout, prm, expand, s0, ctx0)


(VT_HQ, VT_HK, VT_HF, VT_HV, VT_GQ, VT_GK, VT_GV, VT_GDEC, VT_GBETA,
 VT_SV, VT_SB, VT_SC, VT_SDEC, VT_RQ, VT_RK, VT_RV, VT_RDEC) = [GROUP_W * i for i in range(17)]
N_VT = 17 * GROUP_W
DEC_VBLOCK = 32


def _dec_pre_kernel(layer, hs_ref, cos_ref, sin_ref, win_ref, prm_ref, exp_ref, gctx_ref, sctx_ref,
                    vt_ref, post_ref, ug_ref, us_ref):
    x = hs_ref[...]
    nb = x.shape[0]
    hn = _rms(x) * _prow(prm_ref, R_NORM, D_MODEL)
    proj = jnp.dot(hn.astype(BF16), win_ref[0], preferred_element_type=F32)

    ri = _iota((GROUP_W, GROUP_W), 0)
    ci = _iota((GROUP_W, GROUP_W), 1)
    ones64 = jnp.where((ri >> 6) == (ci >> 6), 1.0, 0.0).astype(BF16)

    def put(off, a):
        vt_ref[off:off + GROUP_W, :] = jnp.transpose(a)

    lb = _hgrn_lower_bound(_prow(prm_ref, R_LB, n=DEPTH), layer)
    q, k, f, v = _hgrn_inputs(proj, lb)
    put(VT_HQ, q)
    put(VT_HK, k)
    put(VT_HF, f)
    put(VT_HV, v)

    log_decay, vals = _compact_gates(proj[:, SMALL:SMALL + 128], prm_ref)
    decay = jnp.exp(_dot_exact_rhs(log_decay, exp_ref[:, 0:512]))
    vals = _dot_exact_rhs(vals, exp_ref[:, 512:1024])

    u = proj[:, B_QKV:B_QKV + CONV_DIM]
    ug_ref[...] = u
    cw = _prow(prm_ref, R_GCONV, CONV_DIM, CONV_W)
    conv = cw[3:4] * u + cw[2:3] * gctx_ref[0, 2] + cw[1:2] * gctx_ref[0, 1] + cw[0:1] * gctx_ref[0, 0]
    act = _silu(conv)
    qn, kn = _l2n_pair(act[:, 0:256], act[:, 256:512], ones64)
    put(VT_GQ, qn * QK_SCALE)
    put(VT_GK, kn)
    put(VT_GV, act[:, 512:768])
    put(VT_GDEC, decay[:, 0:256])
    put(VT_GBETA, vals[:, 0:256])

    u = proj[:, C_XBC:C_XBC + CONV_DIM]
    us_ref[...] = u
    cw = _prow(prm_ref, R_SCONV, CONV_DIM, CONV_W)
    conv = (cw[3:4] * u + cw[2:3] * sctx_ref[0, 2] + cw[1:2] * sctx_ref[0, 1] + cw[0:1] * sctx_ref[0, 0]
            + _prow(prm_ref, R_SCONVB, CONV_DIM))
    act = _silu(conv)
    post_ref[:, POST_SX:POST_SX + 256] = act[:, 0:256]
    put(VT_SV, act[:, 0:256] * vals[:, 256:512])
    put(VT_SB, act[:, 256:512])
    put(VT_SC, act[:, 512:768])
    put(VT_SDEC, decay[:, 256:512])

    cos = cos_ref[...]
    sin = sin_ref[...]
    put(VT_RQ, _rotary(proj[:, D_Q:D_Q + GROUP_W], cos, sin))
    put(VT_RK, _rotary(proj[:, D_K:D_K + GROUP_W], cos, sin) * QK_SCALE)
    put(VT_RV, proj[:, D_V:D_V + GROUP_W])
    put(VT_RDEC, jnp.broadcast_to(jnp.exp(_prow(prm_ref, R_RLOGG)), (nb, GROUP_W)))

    post_ref[:, POST_AZ:POST_AZ + 256] = proj[:, A_Z:A_Z + GROUP_W]
    post_ref[:, POST_BZ:POST_BZ + 256] = proj[:, B_Z:B_Z + GROUP_W]
    post_ref[:, POST_CZ:POST_CZ + 256] = proj[:, C_Z:C_Z + GROUP_W]
    post_ref[:, POST_DZ:POST_DZ + 256] = proj[:, D_Z:D_Z + GROUP_W]


def _dec_pre(layer, hs, cos, sin, win, prm, expand, gctx, sctx):
    nb = hs.shape[0]
    full = lambda a: pl.BlockSpec(a.shape, lambda i: (0,) * a.ndim)
    shapes = [(N_VT, nb), (nb, N_POST), (nb, CONV_DIM), (nb, CONV_DIM)]
    return pl.pallas_call(
        functools.partial(_dec_pre_kernel, layer),
        grid=(1,),
        in_specs=[
            full(hs), full(cos), full(sin),
            pl.BlockSpec((1, D_MODEL, NP), lambda i: (layer, 0, 0)),
            pl.BlockSpec((1, N_PRM, D_MODEL), lambda i: (layer, 0, 0)),
            full(expand),
            pl.BlockSpec((1, 3, nb, CONV_DIM), lambda i: (layer, 0, 0, 0)),
            pl.BlockSpec((1, 3, nb, CONV_DIM), lambda i: (layer, 0, 0, 0)),
        ],
        out_specs=[pl.BlockSpec(s, lambda i: (0, 0)) for s in shapes],
        out_shape=[jax.ShapeDtypeStruct(s, F32) for s in shapes],
        compiler_params=pltpu.CompilerParams(dimension_semantics=("arbitrary",), vmem_limit_bytes=VMEM_LIMIT),
        name="decode_pre",
    )(hs, cos, sin, win, prm, expand, gctx, sctx)


def _dec_state_kernel(vt_ref, hg_ref, gd_ref, sd_ref, rt_ref, hg_acc, gd_acc, rt_acc,
                      o_ref, hg_out, gd_out, sd_out, rt_out):
    del hg_acc, gd_acc, rt_acc
    h = pl.program_id(0)
    j = pl.program_id(1)
    vb = DEC_VBLOCK
    nb = vt_ref.shape[1]
    head = h * HEAD_DIM

    def row(off, i):
        return vt_ref[pl.ds(off + i, 1), :]

    def slab(off):
        return vt_ref[pl.ds(pl.multiple_of(off + head + j * vb, 8), vb), :]

    zero = jnp.zeros((vb, nb), F32)

    v = slab(VT_HV)

    def hgrn_step(k, o):
        s = row(VT_HF + head, k) * hg_ref[0, 0, k] + row(VT_HK + head, k) * v
        hg_out[0, 0, k] = s
        return o + row(VT_HQ + head, k) * s

    o_ref[0, 0] = lax.fori_loop(0, HEAD_DIM, hgrn_step, zero, unroll=4)

    v = slab(VT_GV)
    dec = row(VT_GDEC + head, 0)
    ks = lax.fori_loop(0, HEAD_DIM, lambda k, a: a + row(VT_GK + head, k) * gd_ref[0, 0, k], zero, unroll=4)
    u = row(VT_GBETA + head, 0) * (v - ks * dec)

    def gdn_step(k, o):
        s = gd_ref[0, 0, k] * dec + row(VT_GK + head, k) * u
        gd_out[0, 0, k] = s
        return o + row(VT_GQ + head, k) * s

    o_ref[1, 0] = lax.fori_loop(0, HEAD_DIM, gdn_step, zero, unroll=4)

    v = slab(VT_SV)
    dec = row(VT_SDEC + head, 0)
    grp = (h >> 1) * 128

    def ssd_step(n, o):
        s = sd_ref[0, 0, n] * dec + row(VT_SB + grp, n) * v
        sd_out[0, 0, n] = s
        return o + row(VT_SC + grp, n) * s

    o_ref[2, 0] = lax.fori_loop(0, 128, ssd_step, zero, unroll=4)

    v = slab(VT_RV)
    dec = row(VT_RDEC + head, 0)

    def ret_step(k, o):
        s = rt_ref[0, 0, k] * dec + row(VT_RK + head, k) * v
        rt_out[0, 0, k] = s
        return o + row(VT_RQ + head, k) * s

    o_ref[3, 0] = lax.fori_loop(0, HEAD_DIM, ret_step, zero, unroll=4)


def _dec_state(layer, vt, states, accs):
    nb = vt.shape[1]
    vb = DEC_VBLOCK
    blk = lambda a: pl.BlockSpec((1, 1, a.shape[2], vb, nb), lambda h, j: (layer, h, 0, j, 0))
    return pl.pallas_call(
        _dec_state_kernel,
        grid=(N_HEADS, HEAD_DIM // vb),
        in_specs=[pl.BlockSpec(vt.shape, lambda h, j: (0, 0))] + [blk(a) for a in states]
        + [pl.BlockSpec(memory_space=pl.ANY)] * 3,
        out_specs=[pl.BlockSpec((4, 1, vb, nb), lambda h, j: (0, h, j, 0))] + [blk(a) for a in states],
        out_shape=[jax.ShapeDtypeStruct((4, N_HEADS, HEAD_DIM, nb), F32)]
        + [jax.ShapeDtypeStruct(a.shape, F32) for a in states],
        input_output_aliases={3: 3, 5: 1, 6: 2, 7: 4},
        compiler_params=pltpu.CompilerParams(dimension_semantics=("arbitrary", "arbitrary"),
                                             vmem_limit_bytes=VMEM_LIMIT),
        name="decode_state",
    )(vt, *states, *accs)


def _dec_post_kernel(final, hs_ref, ot_ref, post_ref, wout_ref, prm_ref, out_ref):
    ri = _iota((GROUP_W, GROUP_W), 0)
    ci = _iota((GROUP_W, GROUP_W), 1)
    ones64 = jnp.where((ri >> 6) == (ci >> 6), 1.0, 0.0).astype(BF16)
    ones128 = jnp.where((ri >> 7) == (ci >> 7), 1.0, 0.0).astype(BF16)
    gate = lambda off: post_ref[:, off:off + GROUP_W]
    o = [jnp.transpose(ot_ref[m * GROUP_W:(m + 1) * GROUP_W, :]) for m in range(4)]
    y = _post_mix(o[0], o[1], o[2], o[3],
                  gate(POST_AZ), gate(POST_BZ), gate(POST_CZ), gate(POST_DZ), gate(POST_SX),
                  prm_ref, ones64, ones128)
    out = hs_ref[...] + jnp.dot(y.astype(BF16), wout_ref[0], preferred_element_type=F32)
    if final:
        out = _rms(out) * _prow(prm_ref, R_FINAL, D_MODEL)
    out_ref[...] = out


def _dec_post(layer, hs, ot, post, wout, prm, final):
    nb = hs.shape[0]
    full = lambda a: pl.BlockSpec(a.shape, lambda i: (0,) * a.ndim)
    return pl.pallas_call(
        functools.partial(_dec_post_kernel, final),
        grid=(1,),
        in_specs=[full(hs), full(ot), full(post),
                  pl.BlockSpec((1, D_MODEL, D_MODEL), lambda i: (layer, 0, 0)),
                  pl.BlockSpec((1, N_PRM, D_MODEL), lambda i: (layer, 0, 0))],
        out_specs=pl.BlockSpec((nb, D_MODEL), lambda i: (0, 0)),
        out_shape=jax.ShapeDtypeStruct((nb, D_MODEL), F32),
        compiler_params=pltpu.CompilerParams(dimension_semantics=("arbitrary",), vmem_limit_bytes=VMEM_LIMIT),
        name="decode_post",
    )(hs, ot, post, wout, prm)


POST_AZ, POST_BZ, POST_CZ, POST_DZ, POST_SX = [GROUP_W * i for i in range(5)]
N_POST = 5 * GROUP_W


def _expand_matrix():
    e = np.zeros((128, 4 * GROUP_W), np.float32)
    for block, first_lane in enumerate((0, 2 * N_HEADS, N_HEADS, 2 * N_HEADS)):
        for h in range(N_HEADS):
            lo = block * GROUP_W + h * HEAD_DIM
            e[first_lane + h, lo:lo + HEAD_DIM] = 1.0
    return jnp.asarray(e, BF16)


def _relayout_w_in(w_in):
    sizes = (256, 256, 256, 256, 768, 256, 4, 4, 768, 256, 4, 256, 256, 256, 256)
    offs = np.concatenate([[0], np.cumsum(sizes)])
    w_t = jnp.transpose(w_in, (2, 0, 1))
    seg = lambda i: w_t[offs[i]:offs[i + 1]]
    pad = jnp.zeros((128 - 12,) + w_t.shape[1:], w_in.dtype)
    rows = [seg(0), seg(1), seg(2), seg(3), seg(4), seg(5), seg(8), seg(9), seg(11), seg(12), seg(13), seg(14),
            seg(6), seg(7), seg(10), pad]
    return jnp.transpose(jnp.concatenate(rows, axis=0), (1, 2, 0)).astype(BF16)


def _pack_params(norm_w, hgrn_lb_logits, hgrn_norm_w, gdn_conv_w, gdn_a_log, gdn_dt_bias, gdn_norm_w,
                 ssd_conv_w, ssd_conv_b, ssd_a_log, ssd_dt_bias, ssd_d, ssd_norm_w, ret_norm_w, ret_norm_b,
                 final_norm_w):
    depth = norm_w.shape[0]
    prm = jnp.zeros((depth, N_PRM, D_MODEL), F32)

    def put(p, r, a):
        a = a.astype(F32)
        if a.ndim == 2:
            a = a[:, None, :]
        return p.at[:, r:r + a.shape[1], 0:a.shape[2]].set(a)

    rep = lambda a: jnp.repeat(a, HEAD_DIM, axis=-1)
    prm = put(prm, R_NORM, norm_w)
    prm = put(prm, R_LB, jnp.broadcast_to(hgrn_lb_logits[None], (depth,) + hgrn_lb_logits.shape))
    prm = put(prm, R_HNORM, hgrn_norm_w)
    prm = put(prm, R_GCONV, gdn_conv_w)
    prm = put(prm, R_GNORM, gdn_norm_w)
    prm = put(prm, R_SCONV, ssd_conv_w)
    prm = put(prm, R_SCONVB, ssd_conv_b)
    prm = put(prm, R_SD, rep(ssd_d))
    gap = jnp.zeros((depth, N_HEADS), F32)
    prm = put(prm, R_CBIAS, jnp.concatenate([gdn_dt_bias.astype(F32), gap, ssd_dt_bias.astype(F32)], axis=-1))
    prm = put(prm, R_CALOG, jnp.concatenate([gdn_a_log.astype(F32), gap, ssd_a_log.astype(F32)], axis=-1))
    prm = put(prm, R_SNORM, ssd_norm_w)
    prm = put(prm, R_RNORM, ret_norm_w)
    prm = put(prm, R_RNORMB, ret_norm_b)
    ret_logg = jnp.log1p(-jnp.exp2(-5.0 - jnp.arange(N_HEADS, dtype=F32)))
    prm = put(prm, R_RLOGG, jnp.broadcast_to(rep(ret_logg)[None], (depth, GROUP_W)))
    prm = put(prm, R_FINAL, jnp.broadcast_to(final_norm_w[None], (depth, D_MODEL)))
    return prm


def _rope_tables(pos):
    half = HEAD_DIM // 2
    inv_freq = 1.0 / (ROPE_BASE ** jnp.linspace(0.0, 1.0, half, dtype=F32))
    ang = pos[:, None] * inv_freq[None, :]
    cos, sin = jnp.cos(ang), jnp.sin(ang)
    cos_h = jnp.concatenate([cos, cos], axis=-1)
    sin_h = jnp.concatenate([-sin, sin], axis=-1)
    return jnp.tile(cos_h, (1, N_HEADS)), jnp.tile(sin_h, (1, N_HEADS))


def kernel(x_prompt, x_sample, state_hgrn, state_gdn, state_gdn_conv, state_ssd, state_ssd_conv, state_ret,
           meta_tokens, norm_w, w_in, hgrn_lb_logits, hgrn_norm_w, gdn_conv_w, gdn_a_log, gdn_dt_bias, gdn_norm_w,
           ssd_conv_w, ssd_conv_b, ssd_a_log, ssd_dt_bias, ssd_d, ssd_norm_w, ret_norm_w, ret_norm_b,
           w_out, final_norm_w):
    depth = w_in.shape[0]
    bsz, seq, _ = x_prompt.shape
    nb = x_sample.shape[0]
    tile = PROMPT_TILE if seq % PROMPT_TILE == 0 else CHUNK

    win = _relayout_w_in(w_in)
    wout = w_out.astype(BF16)
    prm = _pack_params(norm_w, hgrn_lb_logits, hgrn_norm_w, gdn_conv_w, gdn_a_log, gdn_dt_bias, gdn_norm_w,
                       ssd_conv_w, ssd_conv_b, ssd_a_log, ssd_dt_bias, ssd_d, ssd_norm_w, ret_norm_w, ret_norm_b,
                       final_norm_w)
    expand = _expand_matrix()

    cos_m, sin_m = _rope_tables(jnp.arange(CHUNK, dtype=F32))
    cos_p, sin_p = _rope_tables(N_META + jnp.arange(seq, dtype=F32))
    cos_s, sin_s = _rope_tables(PAST_LEN + jnp.arange(1, dtype=F32))

    hm = jnp.zeros((1, CHUNK, D_MODEL), F32).at[0, :N_META].set(meta_tokens.astype(F32))
    hp = x_prompt.astype(F32)
    hs = x_sample.astype(F32)[:, 0, :]
    zero_s = jnp.zeros((4, GROUP_W, GROUP_W), F32)
    zero_ctx = jnp.zeros((2, 8, CONV_DIM), F32)

    gctx = jnp.swapaxes(state_gdn_conv.astype(F32), 1, 2)
    sctx = jnp.swapaxes(state_ssd_conv.astype(F32), 1, 2)

    seq_minor = lambda a: jnp.transpose(a.astype(F32), (0, 2, 3, 4, 1))
    dec_states = [seq_minor(a) for a in (state_hgrn, state_gdn, state_ssd, state_ret)]
    dec_new = [jnp.zeros_like(dec_states[0]), jnp.zeros_like(dec_states[1]), dec_states[2],
               jnp.zeros_like(dec_states[3])]

    p_states, p_ctx = [], []
    s_gctx, s_sctx = [], []
    for l in range(depth):
        final = l == depth - 1
        hm, sm, cm, *_ = _prompt_layer(l, hm, cos_m, sin_m, win, wout, prm, expand, zero_s, zero_ctx,
                                       tile=CHUNK, valid=N_META, final=False)
        hp, _, cp, *head_states = _prompt_layer(l, hp, cos_p, sin_p, win, wout, prm, expand, sm[0], cm[0],
                                                tile=tile, valid=tile, final=final)
        p_states.append(head_states)
        p_ctx.append(cp)

        vt, post, ug, us = _dec_pre(l, hs, cos_s, sin_s, win, prm, expand, gctx, sctx)
        ot, *dec_new = _dec_state(l, vt, [dec_states[0], dec_states[1], dec_new[2], dec_states[3]],
                                  [dec_new[0], dec_new[1], dec_new[3]])
        hs = _dec_post(l, hs, ot.reshape(4 * GROUP_W, nb), post, wout, prm, final)
        s_gctx.append(jnp.concatenate([state_gdn_conv[l, :, 1:].astype(F32), ug[:, None, :]], axis=1))
        s_sctx.append(jnp.concatenate([state_ssd_conv[l, :, 1:].astype(F32), us[:, None, :]], axis=1))

    ph, pg, ps, pr = [jnp.stack([layer_states[i] for layer_states in p_states]) for i in range(4)]
    cp = jnp.stack(p_ctx)
    seq_major = lambda a: jnp.transpose(a, (0, 4, 1, 2, 3))
    return (hp.astype(x_prompt.dtype), hs[:, None, :].astype(x_sample.dtype),
            ph.astype(state_hgrn.dtype), pg.astype(state_gdn.dtype),
            cp[:, :, 0, 8 - (CONV_W - 1):].astype(state_gdn_conv.dtype),
            ps.astype(state_ssd.dtype),
            cp[:, :, 1, 8 - (CONV_W - 1):].astype(state_ssd_conv.dtype),
            pr.astype(state_ret.dtype),
            seq_major(dec_new[0]).astype(state_hgrn.dtype), seq_major(dec_new[1]).astype(state_gdn.dtype),
            jnp.stack(s_gctx).astype(state_gdn_conv.dtype), seq_major(dec_new[2]).astype(state_ssd.dtype),
            jnp.stack(s_sctx).astype(state_ssd_conv.dtype), seq_major(dec_new[3]).astype(state_ret.dtype))
```
